```python
import math
import jax, jax.numpy as jnp
from jax import lax
import numpy as np

D_MODEL = 2048
BATCH = 8
SEQ = 2048
DEPTH = 4
DEC_BATCH = 16
DEC_SEQ = 2048
PAST_LEN = 128

D_MIX = D_MODEL
POOL_DIM = D_MIX // 4
POOL_WINDOWS = (2, 4, 8, 16)
POOL_GROUP = POOL_DIM // len(POOL_WINDOWS)
MLA_HEADS = 8
QK_NOPE = 128
QK_ROPE = 64
QK_HEAD = QK_NOPE + QK_ROPE
V_HEAD = 128
MLA_DIM = MLA_HEADS * V_HEAD
Q_LORA = D_MODEL // 4
KV_LORA = D_MODEL // 8
ROPE_THETA = 10000.0
Q_BLOCK = 128
HY_DIM = D_MIX - POOL_DIM - MLA_DIM
HY_ORDER = 2
N_BANDS = 8
N_POS_FEAT = 1 + 2 * N_BANDS
FILT_HIDDEN = 64
HY_FAST_DECAY = 0.3
HY_SLOW_DECAY = 1.5
HY_DECAY_TARGET = 1e-2
D_FF = 5632
NORM_EPS = 1e-6
D_IN = POOL_DIM + Q_LORA + KV_LORA + QK_ROPE + (HY_ORDER + 1) * HY_DIM

kernel_name = 'hybrid_pool_mla_hyena_encoder'


def _rms_norm(x, g):
    xf = x.astype(jnp.float32)
    y = xf * lax.rsqrt(jnp.mean(xf * xf, axis=-1, keepdims=True) + NORM_EPS)
    return (y * g.astype(jnp.float32)).astype(x.dtype)


def _dwconv3(x, w, b):
    xp = jnp.pad(x, ((0, 0), (1, 1), (0, 0)))
    return xp[:, :-2] * w[0] + xp[:, 1:-1] * w[1] + xp[:, 2:] * w[2] + b


def _pool_mixer(u, pool_w, pool_scale):
    B, L, _ = u.shape
    uf = u.astype(jnp.float32)
    csum = jnp.concatenate([jnp.zeros((B, 1, POOL_DIM), jnp.float32), jnp.cumsum(uf, axis=1)], axis=1)
    pos = jnp.arange(L)
    outs = []
    for g, w in enumerate(POOL_WINDOWS):
        half = w // 2
        lo = jnp.maximum(pos - half, 0)
        hi = jnp.minimum(pos + half, L)
        sl = slice(g * POOL_GROUP, (g + 1) * POOL_GROUP)
        cg = csum[..., sl]
        mean = (cg[:, hi] - cg[:, lo]) / (hi - lo).astype(jnp.float32)[:, None]
        pooled = (mean - uf[..., sl]).astype(u.dtype)
        outs.append(pooled @ pool_w[g])
    return jnp.concatenate(outs, axis=-1) * pool_scale


def _rope(x, L):
    freqs = ROPE_THETA ** (-jnp.arange(0, QK_ROPE, 2, dtype=jnp.float32) / QK_ROPE)
    ang = jnp.arange(L, dtype=jnp.float32)[:, None] * freqs[None, :]
    cos = jnp.cos(ang)[None, :, None, :]
    sin = jnp.sin(ang)[None, :, None, :]
    xf = x.astype(jnp.float32)
    x1, x2 = xf[..., :QK_ROPE // 2], xf[..., QK_ROPE // 2:]
    return jnp.concatenate([x1 * cos - x2 * sin, x1 * sin + x2 * cos], axis=-1).astype(x.dtype)


def _mla(c_q, c_kv, k_pe, q_norm_g, w_uq, kv_norm_g, w_ukv, qh_g, kh_g):
    B, L, _ = c_q.shape
    q = (_rms_norm(c_q, q_norm_g) @ w_uq).reshape(B, L, MLA_HEADS, QK_HEAD)
    kv = (_rms_norm(c_kv, kv_norm_g) @ w_ukv).reshape(B, L, MLA_HEADS, QK_NOPE + V_HEAD)
    k_nope, v = kv[..., :QK_NOPE], kv[..., QK_NOPE:]
    k = jnp.concatenate([k_nope, jnp.broadcast_to(k_pe[:, :, None, :], (B, L, MLA_HEADS, QK_ROPE))], axis=-1)
    q = _rms_norm(q, qh_g)
    k = _rms_norm(k, kh_g)
    q = jnp.concatenate([q[..., :QK_NOPE], _rope(q[..., QK_NOPE:], L)], axis=-1)
    k = jnp.concatenate([k[..., :QK_NOPE], _rope(k[..., QK_NOPE:], L)], axis=-1)
    scale = QK_HEAD ** -0.5
    nblk = L // Q_BLOCK
    qb = q.reshape(B, nblk, Q_BLOCK, MLA_HEADS, QK_HEAD).transpose(1, 0, 2, 3, 4)

    def attend(q_blk):
        s = jnp.einsum('bqhd,bkhd->bhqk', q_blk, k, preferred_element_type=jnp.float32) * scale
        p = jax.nn.softmax(s, axis=-1).astype(v.dtype)
        return jnp.einsum('bhqk,bkhd->bqhd', p, v)

    o = lax.map(attend, qb)
    return o.transpose(1, 0, 2, 3, 4).reshape(B, L, MLA_DIM)


def _hyena_filter(L, w1, b1, freq, w2, b2, w3, b3):
    f32 = jnp.float32
    t = jnp.linspace(0.0, 1.0, L, dtype=f32)[:, None]
    w = 2.0 * math.pi * jnp.arange(L, dtype=f32)[:, None] / L
    bands = jnp.linspace(1e-4, N_BANDS - 1, N_BANDS, dtype=f32)[None, :]
    z = jnp.concatenate([t, jnp.cos(bands * w), -jnp.sin(bands * w)], axis=-1)
    fr = freq.astype(f32)
    h = jnp.sin(fr * (z @ w1.astype(f32) + b1.astype(f32)))
    h = jnp.sin(fr * (h @ w2.astype(f32) + b2.astype(f32)))
    h = (h @ w3.astype(f32) + b3.astype(f32)).reshape(L, 2, HY_DIM)
    max_decay = math.log(HY_DECAY_TARGET) / HY_FAST_DECAY
    min_decay = math.log(HY_DECAY_TARGET) / HY_SLOW_DECAY
    deltas = jnp.linspace(min_decay, max_decay, HY_DIM, dtype=f32)
    h = h * jnp.exp(-t * jnp.abs(deltas))[:, None, :]
    return jnp.concatenate([h[:, 0], jnp.zeros((1, HY_DIM), f32), h[:0:-1, 1]], axis=0)


def _hyena(u, conv_w, conv_b, bias, w1, b1, freq, w2, b2, w3, b3):
    B, L, _ = u.shape
    u = _dwconv3(u, conv_w, conv_b)
    x0, x1, v = u[..., :HY_DIM], u[..., HY_DIM:2 * HY_DIM], u[..., 2 * HY_DIM:]
    z = (v * x1).astype(jnp.float32)
    kern = _hyena_filter(L, w1, b1, freq, w2, b2, w3, b3)
    K = jnp.fft.rfft(kern, n=2 * L, axis=0)
    Z = jnp.fft.rfft(z, n=2 * L, axis=1)
    y = jnp.fft.irfft(Z * K[None], n=2 * L, axis=1)[:, :L] + z * bias.astype(jnp.float32)
    return (y * x0.astype(jnp.float32)).astype(u.dtype)


def _layer(x, attn_norm_g, w_in, pool_w, pool_scale, mla_q_norm_g, mla_w_uq, mla_kv_norm_g, mla_w_ukv,
           mla_q_head_norm_g, mla_k_head_norm_g, hy_conv_w, hy_conv_b, hy_filt_w1, hy_filt_b1, hy_filt_freq,
           hy_filt_w2, hy_filt_b2, hy_filt_w3, hy_filt_b3, hy_bias, grp_norm_g, w_out, ffn_norm_g, ffn_w_up,
           ffn_conv_w, ffn_conv_b, ffn_w_down):
    h = _rms_norm(x, attn_norm_g)
    proj = h @ w_in
    o0 = POOL_DIM
    o1 = o0 + Q_LORA
    o2 = o1 + KV_LORA
    o3 = o2 + QK_ROPE
    y_a = _pool_mixer(proj[..., :o0], pool_w, pool_scale)
    y_b = _mla(proj[..., o0:o1], proj[..., o1:o2], proj[..., o2:o3], mla_q_norm_g, mla_w_uq,
               mla_kv_norm_g, mla_w_ukv, mla_q_head_norm_g, mla_k_head_norm_g)
    y_c = _hyena(proj[..., o3:], hy_conv_w, hy_conv_b, hy_bias, hy_filt_w1, hy_filt_b1, hy_filt_freq,
                 hy_filt_w2, hy_filt_b2, hy_filt_w3, hy_filt_b3)
    mixed = jnp.concatenate([
        _rms_norm(y_a, grp_norm_g[:POOL_DIM]),
        _rms_norm(y_b, grp_norm_g[POOL_DIM:POOL_DIM + MLA_DIM]),
        _rms_norm(y_c, grp_norm_g[POOL_DIM + MLA_DIM:]),
    ], axis=-1)
    x = x + (mixed @ w_out).astype(x.dtype)
    h = _rms_norm(x, ffn_norm_g)
    up = h @ ffn_w_up
    gate = _dwconv3(up[..., :D_FF], ffn_conv_w, ffn_conv_b)
    act = jax.nn.silu(gate) * up[..., D_FF:]
    return x + (act @ ffn_w_down).astype(x.dtype)


def setup_inputs(seed: int = 0) -> dict:
    key = jax.random.key(seed)
    k = jax.random.split(key, 32)

    def nrm(i, shape, scale):
        return jax.random.normal(k[i], shape, jnp.float32) * scale

    def gain(i, shape):
        return 1.0 + nrm(i, shape, 0.02)

    L_ = DEPTH
    return {
        'x_prompt': nrm(0, (BATCH, SEQ, D_MODEL), 1.0),
        'x_sample': nrm(1, (DEC_BATCH, DEC_SEQ, D_MODEL), 1.0),
        'attn_norm_g': gain(2, (L_, D_MODEL)),
        'w_in': nrm(3, (L_, D_MODEL, D_IN), D_MODEL ** -0.5),
        'pool_w': nrm(4, (L_, len(POOL_WINDOWS), POOL_GROUP, POOL_GROUP), POOL_GROUP ** -0.5),
        'pool_scale': 1.0 + nrm(5, (L_, POOL_DIM), 0.1),
        'mla_q_norm_g': gain(6, (L_, Q_LORA)),
        'mla_w_uq': nrm(7, (L_, Q_LORA, MLA_HEADS * QK_HEAD), Q_LORA ** -0.5),
        'mla_kv_norm_g': gain(8, (L_, KV_LORA)),
        'mla_w_ukv': nrm(9, (L_, KV_LORA, MLA_HEADS * (QK_NOPE + V_HEAD)), KV_LORA ** -0.5),
        'mla_q_head_norm_g': gain(10, (L_, QK_HEAD)),
        'mla_k_head_norm_g': gain(11, (L_, QK_HEAD)),
        'hy_conv_w': nrm(12, (L_, 3, (HY_ORDER + 1) * HY_DIM), 3 ** -0.5),
        'hy_conv_b': nrm(13, (L_, (HY_ORDER + 1) * HY_DIM), 0.02),
        'hy_filt_w1': nrm(14, (L_, N_POS_FEAT, FILT_HIDDEN), N_POS_FEAT ** -0.5),
        'hy_filt_b1': nrm(15, (L_, FILT_HIDDEN), 0.1),
        'hy_filt_freq': 1.0 + nrm(16, (L_, FILT_HIDDEN), 0.1),
        'hy_filt_w2': nrm(17, (L_, FILT_HIDDEN, FILT_HIDDEN), FILT_HIDDEN ** -0.5),
        'hy_filt_b2': nrm(18, (L_, FILT_HIDDEN), 0.1),
        'hy_filt_w3': nrm(19, (L_, FILT_HIDDEN, 2 * HY_DIM), 0.1 * FILT_HIDDEN ** -0.5),
        'hy_filt_b3': nrm(20, (L_, 2 * HY_DIM), 0.01),
        'hy_bias': nrm(21, (L_, HY_DIM), 0.5),
        'grp_norm_g': gain(22, (L_, D_MIX)),
        'w_out': nrm(23, (L_, D_MIX, D_MODEL), D_MIX ** -0.5),
        'ffn_norm_g': gain(24, (L_, D_MODEL)),
        'ffn_w_up': nrm(25, (L_, D_MODEL, 2 * D_FF), D_MODEL ** -0.5),
        'ffn_conv_w': nrm(26, (L_, 3, D_FF), 3 ** -0.5),
        'ffn_conv_b': nrm(27, (L_, D_FF), 0.02),
        'ffn_w_down': nrm(28, (L_, D_FF, D_MODEL), D_FF ** -0.5),
    }


def reference(x_prompt, x_sample, attn_norm_g, w_in, pool_w, pool_scale, mla_q_norm_g, mla_w_uq,
              mla_kv_norm_g, mla_w_ukv, mla_q_head_norm_g, mla_k_head_norm_g, hy_conv_w, hy_conv_b,
              hy_filt_w1, hy_filt_b1, hy_filt_freq, hy_filt_w2, hy_filt_b2, hy_filt_w3, hy_filt_b3, hy_bias,
              grp_norm_g, w_out, ffn_norm_g, ffn_w_up, ffn_conv_w, ffn_conv_b, ffn_w_down):
    layer_params = (attn_norm_g, w_in, pool_w, pool_scale, mla_q_norm_g, mla_w_uq, mla_kv_norm_g, mla_w_ukv,
                    mla_q_head_norm_g, mla_k_head_norm_g, hy_conv_w, hy_conv_b, hy_filt_w1, hy_filt_b1,
                    hy_filt_freq, hy_filt_w2, hy_filt_b2, hy_filt_w3, hy_filt_b3, hy_bias, grp_norm_g, w_out,
                    ffn_norm_g, ffn_w_up, ffn_conv_w, ffn_conv_b, ffn_w_down)

    def run(x):
        for l in range(DEPTH):
            x = _layer(x, *[p[l] for p in layer_params])
        return x

    y_prompt = run(x_prompt)
    y_sample = run(x_sample)
    return (y_prompt, y_sample)
```

```python
import functools
import math

import jax
import jax.numpy as jnp
from jax import lax
from jax.experimental import pallas as pl
from jax.experimental.pallas import tpu as pltpu

F32 = jnp.float32
BF16 = jnp.bfloat16

D_MODEL = 2048
DEPTH = 4
POOL_DIM = 512
POOL_WINDOWS = (2, 4, 8, 16)
POOL_GROUP = 128
MLA_HEADS = 8
QK_NOPE = 128
QK_ROPE = 64
QK_HEAD = QK_NOPE + QK_ROPE
V_HEAD = 128
MLA_DIM = MLA_HEADS * V_HEAD
Q_LORA = 512
KV_LORA = 256
ROPE_THETA = 10000.0
HY_DIM = 512
N_BANDS = 8
N_POS_FEAT = 1 + 2 * N_BANDS
FILT_HIDDEN = 64
HY_FAST_DECAY = 0.3
HY_SLOW_DECAY = 1.5
HY_DECAY_TARGET = 1e-2
D_FF = 5632
NORM_EPS = 1e-6

LANES = 128
HEAD_PAD = 2 * LANES
ROPE_HALF = QK_ROPE // 2
D_IN_P = 3 * HY_DIM + POOL_DIM + Q_LORA + KV_LORA + LANES
COL_POOL = 3 * HY_DIM
COL_CQ = COL_POOL + POOL_DIM
COL_CKV = COL_CQ + Q_LORA
COL_ROPE = COL_CKV + KV_LORA
VMEM_LIMIT = 56 * 1024 * 1024


def _params(*sem):
    return pltpu.CompilerParams(dimension_semantics=sem, vmem_limit_bytes=VMEM_LIMIT)


def _resident(shape):
    nd = len(shape)
    return pl.BlockSpec(shape, lambda *_: (0,) * nd, pipeline_mode=pl.Buffered(1))


def _rms(x, g, n=None):
    n = x.shape[-1] if n is None else n
    ms = jnp.sum(x * x, axis=-1, keepdims=True) * (1.0 / n)
    return x * lax.rsqrt(ms + NORM_EPS) * g


def _shift_down(x, k, row):
    return jnp.where(row >= k, pltpu.roll(x, k, 0), 0.0)


def _shift_up(x, k, row):
    n = x.shape[0]
    return jnp.where(row < n - k, pltpu.roll(x, n - k, 0), 0.0)


def _in_proj_kernel(x_ref, g_ref, w_ref, o_ref):
    h = _rms(x_ref[...], g_ref[...]).astype(BF16)
    o_ref[...] = jnp.dot(h, w_ref[...], preferred_element_type=F32).astype(BF16)


def _in_proj(x2, g, w, tm):
    m, d = x2.shape
    n = w.shape[1]
    return pl.pallas_call(
        _in_proj_kernel,
        grid=(m // tm,),
        in_specs=[pl.BlockSpec((tm, d), lambda i: (i, 0)), _resident((1, d)), _resident((d, n))],
        out_specs=pl.BlockSpec((tm, n), lambda i: (i, 0)),
        out_shape=jax.ShapeDtypeStruct((m, n), BF16),
        compiler_params=_params("parallel"),
        name="in_proj",
    )(x2, g, w)


def _pool_kernel(u_ref, w_ref, sc_ref, g_ref, o_ref, y_s):
    seq = u_ref.shape[1]
    row = lax.broadcasted_iota(jnp.int32, (seq, 1), 0)
    for gi, win in enumerate(POOL_WINDOWS):
        half = win // 2
        cols = slice(gi * POOL_GROUP, (gi + 1) * POOL_GROUP)
        x = u_ref[0, :, cols].astype(F32)
        ahead = x
        behind = _shift_down(x, 1, row)
        span = 1
        while span < half:
            ahead = ahead + _shift_up(ahead, span, row)
            behind = behind + _shift_down(behind, span, row)
            span *= 2
        tot = ahead + behind
        cnt = (jnp.minimum(row + half, seq) - jnp.maximum(row - half, 0)).astype(F32)
        pooled = (tot / cnt - x).astype(BF16)
        y_s[:, cols] = jnp.dot(pooled, w_ref[gi], preferred_element_type=F32) * sc_ref[:, cols]
    o_ref[0] = _rms(y_s[...], g_ref[...]).astype(BF16)


def _pool(proj3, pool_w, pool_scale, g):
    b, seq, _ = proj3.shape
    return pl.pallas_call(
        _pool_kernel,
        grid=(b,),
        in_specs=[pl.BlockSpec((1, seq, POOL_DIM), lambda i: (i, 0, COL_POOL // POOL_DIM)),
                  _resident(pool_w.shape), _resident((1, POOL_DIM)), _resident((1, POOL_DIM))],
        out_specs=pl.BlockSpec((1, seq, POOL_DIM), lambda i: (i, 0, 0)),
        out_shape=jax.ShapeDtypeStruct((b, seq, POOL_DIM), BF16),
        scratch_shapes=[pltpu.VMEM((seq, POOL_DIM), F32)],
        compiler_params=_params("parallel"),
        name="pool",
    )(proj3, pool_w, pool_scale, g)


def _rope(r, cos_t, sin_t):
    return r * cos_t + pltpu.roll(r, LANES // 2, 1) * sin_t


def _mla_qkv_kernel(cq_ref, ckv_ref, kr_ref, cos_ref, sin_ref, qn_ref, wq_ref, kvn_ref, wk_ref, wv_ref,
                    qhg_ref, khn_ref, khr_ref, q_ref, k_ref, v_ref):
    cos_t = cos_ref[...]
    sin_t = sin_ref[...]
    scale = QK_HEAD ** -0.5
    cq = _rms(cq_ref[...].astype(F32), qn_ref[...]).astype(BF16)
    qf = jnp.dot(cq, wq_ref[...], preferred_element_type=F32)
    for h in range(MLA_HEADS):
        qh = _rms(qf[:, h * HEAD_PAD:(h + 1) * HEAD_PAD], qhg_ref[...], QK_HEAD)
        q_ref[:, h * HEAD_PAD:h * HEAD_PAD + LANES] = (qh[:, :LANES] * scale).astype(BF16)
        q_ref[:, h * HEAD_PAD + LANES:(h + 1) * HEAD_PAD] = (_rope(qh[:, LANES:], cos_t, sin_t) * scale).astype(BF16)
    ckv = _rms(ckv_ref[...].astype(F32), kvn_ref[...]).astype(BF16)
    v_ref[...] = jnp.dot(ckv, wv_ref[...], preferred_element_type=F32).astype(BF16)
    kn = jnp.dot(ckv, wk_ref[...], preferred_element_type=F32)
    kr = kr_ref[...].astype(F32)
    ss_r = jnp.sum(kr * kr, axis=-1, keepdims=True)
    kr_rot = _rope(kr * khr_ref[...], cos_t, sin_t)
    for h in range(MLA_HEADS):
        knh = kn[:, h * LANES:(h + 1) * LANES]
        ms = (jnp.sum(knh * knh, axis=-1, keepdims=True) + ss_r) * (1.0 / QK_HEAD)
        rs = lax.rsqrt(ms + NORM_EPS)
        k_ref[:, h * HEAD_PAD:h * HEAD_PAD + LANES] = (knh * rs * khn_ref[...]).astype(BF16)
        k_ref[:, h * HEAD_PAD + LANES:(h + 1) * HEAD_PAD] = (kr_rot * rs).astype(BF16)


def _mla_qkv(proj2, seq, cos_t, sin_t, qn, wq, kvn, wk, wv, qhg, khn, khr, tl):
    m = proj2.shape[0]
    nseq = seq // tl
    row = lambda width, col: pl.BlockSpec((tl, width), lambda i: (i, col // width))
    tab = pl.BlockSpec((tl, LANES), lambda i: (i % nseq, 0))
    hp = MLA_HEADS * HEAD_PAD
    return pl.pallas_call(
        _mla_qkv_kernel,
        grid=(m // tl,),
        in_specs=[row(Q_LORA, COL_CQ), row(KV_LORA, COL_CKV), row(LANES, COL_ROPE), tab, tab,
                  _resident(qn.shape), _resident(wq.shape), _resident(kvn.shape), _resident(wk.shape),
                  _resident(wv.shape), _resident(qhg.shape), _resident(khn.shape), _resident(khr.shape)],
        out_specs=[pl.BlockSpec((tl, hp), lambda i: (i, 0)), pl.BlockSpec((tl, hp), lambda i: (i, 0)),
                   pl.BlockSpec((tl, MLA_DIM), lambda i: (i, 0))],
        out_shape=[jax.ShapeDtypeStruct((m, hp), BF16), jax.ShapeDtypeStruct((m, hp), BF16),
                   jax.ShapeDtypeStruct((m, MLA_DIM), BF16)],
        compiler_params=_params("parallel"),
        name="mla_qkv",
    )(proj2, proj2, proj2, cos_t, sin_t, qn, wq, kvn, wk, wv, qhg, khn, khr)


def _attn_kernel(q_ref, k_ref, v_ref, g_ref, o_ref, y_s):
    for h in range(MLA_HEADS):
        q = q_ref[0, :, h * HEAD_PAD:(h + 1) * HEAD_PAD]
        k = k_ref[0, :, h * HEAD_PAD:(h + 1) * HEAD_PAD]
        s = lax.dot_general(q, k, (((1,), (1,)), ((), ())), preferred_element_type=F32)
        p = jnp.exp(s - jnp.max(s, axis=-1, keepdims=True))
        den = jnp.sum(p, axis=-1, keepdims=True)
        o = jnp.dot(p.astype(BF16), v_ref[0, :, h * V_HEAD:(h + 1) * V_HEAD], preferred_element_type=F32)
        y_s[:, h * V_HEAD:(h + 1) * V_HEAD] = o / den
    o_ref[0] = _rms(y_s[...], g_ref[...]).astype(BF16)


def _attention(q3, k3, v3, g, tq):
    b, seq, hp = q3.shape
    return pl.pallas_call(
        _attn_kernel,
        grid=(b, seq // tq),
        in_specs=[pl.BlockSpec((1, tq, hp), lambda i, j: (i, j, 0)),
                  pl.BlockSpec((1, seq, hp), lambda i, j: (i, 0, 0)),
                  pl.BlockSpec((1, seq, MLA_DIM), lambda i, j: (i, 0, 0)),
                  _resident((1, MLA_DIM))],
        out_specs=pl.BlockSpec((1, tq, MLA_DIM), lambda i, j: (i, j, 0)),
        out_shape=jax.ShapeDtypeStruct((b, seq, MLA_DIM), BF16),
        scratch_shapes=[pltpu.VMEM((tq, MLA_DIM), F32)],
        compiler_params=_params("parallel", "arbitrary"),
        name="attention",
    )(q3, k3, v3, g)


def _hy_filter_kernel(feat_ref, w1_ref, b1_ref, fr_ref, w2_ref, b2_ref, w3_ref, b3_ref, dl_ref, hs_ref, hd_ref):
    hi = lax.Precision.HIGHEST
    feat = feat_ref[...]
    fr = fr_ref[...]
    h = jnp.sin(fr * (jnp.dot(feat, w1_ref[...], precision=hi, preferred_element_type=F32) + b1_ref[...]))
    h = jnp.sin(fr * (jnp.dot(h, w2_ref[...], precision=hi, preferred_element_type=F32) + b2_ref[...]))
    h = jnp.dot(h, w3_ref[...], precision=hi, preferred_element_type=F32) + b3_ref[...]
    decay = jnp.exp(-feat[:, 0:1] * dl_ref[...])
    fwd = h[:, :HY_DIM] * decay
    row = lax.broadcasted_iota(jnp.int32, (feat.shape[0], 1), 0) + pl.program_id(0) * feat.shape[0]
    bwd = jnp.where(row > 0, h[:, HY_DIM:] * decay, 0.0)
    hs_ref[...] = (fwd + bwd).astype(BF16)
    hd_ref[...] = (bwd - fwd).astype(BF16)


def _hy_filter(feat, w1, b1, fr, w2, b2, w3, b3, dl, tl):
    seq = feat.shape[0]
    out = jax.ShapeDtypeStruct((seq, HY_DIM), BF16)
    return pl.pallas_call(
        _hy_filter_kernel,
        grid=(seq // tl,),
        in_specs=[pl.BlockSpec((tl, LANES), lambda i: (i, 0))] + [_resident(a.shape) for a in (w1, b1, fr, w2, b2, w3, b3, dl)],
        out_specs=[pl.BlockSpec((tl, HY_DIM), lambda i: (i, 0))] * 2,
        out_shape=[out, out],
        compiler_params=_params("parallel"),
        name="hy_filter",
    )(feat, w1, b1, fr, w2, b2, w3, b3, dl)


def _hy_spec_kernel(c_ref, s_ref, hs_ref, hd_ref, kr_ref, ki_ref):
    norm = 1.0 / c_ref.shape[1]
    kr_ref[...] = jnp.dot(c_ref[...], hs_ref[...], preferred_element_type=F32) * norm
    ki_ref[...] = jnp.dot(s_ref[...], hd_ref[...], preferred_element_type=F32) * norm


def _hy_spec(cmat, smat, hs, hd, tf):
    seq = cmat.shape[0]
    out = jax.ShapeDtypeStruct((seq, HY_DIM), F32)
    return pl.pallas_call(
        _hy_spec_kernel,
        grid=(seq // tf,),
        in_specs=[pl.BlockSpec((tf, seq), lambda i: (i, 0))] * 2 + [_resident(hs.shape), _resident(hd.shape)],
        out_specs=[pl.BlockSpec((tf, HY_DIM), lambda i: (i, 0))] * 2,
        out_shape=[out, out],
        compiler_params=_params("parallel"),
        name="hy_spec",
    )(cmat, smat, hs, hd)


def _hy_pre_kernel(u_ref, cw_ref, cb_ref, z_ref, x0_ref):
    seq = u_ref.shape[1]
    row = lax.broadcasted_iota(jnp.int32, (seq, 1), 0)

    def conv(col):
        cols = slice(col, col + LANES)
        x = u_ref[0, :, cols].astype(F32)
        w = cw_ref[:, cols]
        return _shift_down(x, 1, row) * w[0:1] + x * w[1:2] + _shift_up(x, 1, row) * w[2:3] + cb_ref[:, cols]

    for c in range(HY_DIM // LANES):
        cols = slice(c * LANES, (c + 1) * LANES)
        x0_ref[0, :, cols] = conv(c * LANES).astype(BF16)
        z_ref[0, :, cols] = (conv(2 * HY_DIM + c * LANES) * conv(HY_DIM + c * LANES)).astype(BF16)


def _hy_pre(proj3, cw, cb):
    b, seq, _ = proj3.shape
    out = jax.ShapeDtypeStruct((b, seq, HY_DIM), BF16)
    return pl.pallas_call(
        _hy_pre_kernel,
        grid=(b,),
        in_specs=[pl.BlockSpec((1, seq, 3 * HY_DIM), lambda i: (i, 0, 0)), _resident(cw.shape), _resident(cb.shape)],
        out_specs=[pl.BlockSpec((1, seq, HY_DIM), lambda i: (i, 0, 0))] * 2,
        out_shape=[out, out],
        compiler_params=_params("parallel"),
        name="hy_pre",
    )(proj3, cw, cb)


def _hy_fwd_kernel(c_ref, s_ref, z_ref, kr_ref, ki_ref, yr_ref, yi_ref):
    z = z_ref[0]
    u = jnp.dot(c_ref[...], z, preferred_element_type=F32)
    v = jnp.dot(s_ref[...], z, preferred_element_type=F32)
    kr = kr_ref[...]
    ki = ki_ref[...]
    yr_ref[0] = (u * kr + v * ki).astype(BF16)
    yi_ref[0] = (v * kr - u * ki).astype(BF16)


def _hy_fwd(cmat, smat, z, kr, ki, tf):
    b, seq, _ = z.shape
    out = jax.ShapeDtypeStruct((b, seq, HY_DIM), BF16)
    return pl.pallas_call(
        _hy_fwd_kernel,
        grid=(seq // tf, b),
        in_specs=[pl.BlockSpec((tf, seq), lambda f, i: (f, 0)), pl.BlockSpec((tf, seq), lambda f, i: (f, 0)),
                  pl.BlockSpec((1, seq, HY_DIM), lambda f, i: (i, 0, 0)),
                  pl.BlockSpec((tf, HY_DIM), lambda f, i: (f, 0)), pl.BlockSpec((tf, HY_DIM), lambda f, i: (f, 0))],
        out_specs=[pl.BlockSpec((1, tf, HY_DIM), lambda f, i: (i, f, 0))] * 2,
        out_shape=[out, out],
        compiler_params=_params("parallel", "arbitrary"),
        name="hy_fwd",
    )(cmat, smat, z, kr, ki)


def _hy_inv_kernel(ct_ref, st_ref, yr_ref, yi_ref, z_ref, x0_ref, b_ref, g_ref, o_ref):
    y = jnp.dot(ct_ref[...], yr_ref[0], preferred_element_type=F32)
    y = y + jnp.dot(st_ref[...], yi_ref[0], preferred_element_type=F32)
    y = (y + z_ref[0].astype(F32) * b_ref[...]) * x0_ref[0].astype(F32)
    o_ref[0] = _rms(y, g_ref[...]).astype(BF16)


def _hy_inv(cmat_t, smat_t, yr, yi, z, x0, bias, g, tt):
    b, seq, _ = z.shape
    tile = pl.BlockSpec((1, tt, HY_DIM), lambda t, i: (i, t, 0))
    full = pl.BlockSpec((1, seq, HY_DIM), lambda t, i: (i, 0, 0))
    mat = pl.BlockSpec((tt, seq), lambda t, i: (t, 0))
    return pl.pallas_call(
        _hy_inv_kernel,
        grid=(seq // tt, b),
        in_specs=[mat, mat, full, full, tile, tile, _resident((1, HY_DIM)), _resident((1, HY_DIM))],
        out_specs=tile,
        out_shape=jax.ShapeDtypeStruct((b, seq, HY_DIM), BF16),
        compiler_params=_params("parallel", "arbitrary"),
        name="hy_inv",
    )(cmat_t, smat_t, yr, yi, z, x0, bias, g)


def _out_proj_kernel(x_ref, a_ref, b_ref, c_ref, w_ref, o_ref):
    acc = jnp.dot(a_ref[...], w_ref[0:POOL_DIM, :], preferred_element_type=F32)
    acc = acc + jnp.dot(b_ref[...], w_ref[POOL_DIM:POOL_DIM + MLA_DIM, :], preferred_element_type=F32)
    acc = acc + jnp.dot(c_ref[...], w_ref[POOL_DIM + MLA_DIM:, :], preferred_element_type=F32)
    o_ref[...] = x_ref[...] + acc


def _out_proj(x2, ma, mb, mc, w, tm):
    m, d = x2.shape
    row = lambda width: pl.BlockSpec((tm, width), lambda i: (i, 0))
    return pl.pallas_call(
        _out_proj_kernel,
        grid=(m // tm,),
        in_specs=[row(d), row(POOL_DIM), row(MLA_DIM), row(HY_DIM), _resident(w.shape)],
        out_specs=row(d),
        out_shape=jax.ShapeDtypeStruct((m, d), F32),
        compiler_params=_params("parallel"),
        name="out_proj",
    )(x2, ma, mb, mc, w)


HALO = 16


def _ffn_kernel(x_ref, xp_ref, xn_ref, g_ref, wg_ref, wv_ref, cw_ref, cb_ref, wd_ref, o_ref, h_s, up_s, *, blocks_per_seq):
    tm = x_ref.shape[0]
    f = pl.program_id(1)

    @pl.when(f == 0)
    def _():
        x = x_ref[...]
        g = g_ref[...]
        o_ref[...] = x
        h_s[HALO:HALO + tm, :] = _rms(x, g).astype(BF16)
        pos = pl.program_id(0) % blocks_per_seq
        pad = jnp.zeros((HALO - 8, x.shape[1]), F32)
        hp = jnp.where(pos == 0, 0.0, _rms(xp_ref[...], g))
        hn = jnp.where(pos == blocks_per_seq - 1, 0.0, _rms(xn_ref[...], g))
        h_s[0:HALO, :] = jnp.concatenate([pad, hp], axis=0).astype(BF16)
        h_s[HALO + tm:, :] = jnp.concatenate([hn, pad], axis=0).astype(BF16)

    up_s[...] = jnp.dot(h_s[...], wg_ref[...], preferred_element_type=F32)
    val = jnp.dot(h_s[HALO:HALO + tm, :], wv_ref[...], preferred_element_type=F32)
    cw = cw_ref[...]
    gate = (up_s[HALO - 1:HALO - 1 + tm, :] * cw[0:1] + up_s[HALO:HALO + tm, :] * cw[1:2]
            + up_s[HALO + 1:HALO + 1 + tm, :] * cw[2:3] + cb_ref[...])
    act = (gate * jax.nn.sigmoid(gate) * val).astype(BF16)
    o_ref[...] += jnp.dot(act, wd_ref[...], preferred_element_type=F32)


def _ffn(x2, seq, g, w_up, cw, cb, w_down, tm, tf):
    m, d = x2.shape
    nf = D_FF // tf
    r8 = tm // 8
    last8 = m // 8 - 1
    kern = functools.partial(_ffn_kernel, blocks_per_seq=seq // tm)
    return pl.pallas_call(
        kern,
        grid=(m // tm, nf),
        in_specs=[pl.BlockSpec((tm, d), lambda i, f: (i, 0)),
                  pl.BlockSpec((8, d), lambda i, f: (jnp.maximum(i * r8 - 1, 0), 0)),
                  pl.BlockSpec((8, d), lambda i, f: (jnp.minimum((i + 1) * r8, last8), 0)),
                  _resident((1, d)),
                  pl.BlockSpec((d, tf), lambda i, f: (0, f)),
                  pl.BlockSpec((d, tf), lambda i, f: (0, f + nf)),
                  pl.BlockSpec((3, tf), lambda i, f: (0, f)),
                  pl.BlockSpec((1, tf), lambda i, f: (0, f)),
                  pl.BlockSpec((tf, d), lambda i, f: (f, 0))],
        out_specs=pl.BlockSpec((tm, d), lambda i, f: (i, 0)),
        out_shape=jax.ShapeDtypeStruct((m, d), F32),
        scratch_shapes=[pltpu.VMEM((tm + 2 * HALO, d), BF16), pltpu.VMEM((tm + 2 * HALO, tf), F32)],
        compiler_params=_params("parallel", "arbitrary"),
        name="ffn",
    )(x2, x2, x2, g, w_up, w_up, cw, cb, w_down)


def _rope_tables(seq):
    freqs = ROPE_THETA ** (-jnp.arange(0, QK_ROPE, 2, dtype=F32) / QK_ROPE)
    ang = jnp.arange(seq, dtype=F32)[:, None] * freqs[None, :]
    cos, sin, zero = jnp.cos(ang), jnp.sin(ang), jnp.zeros((seq, ROPE_HALF), F32)
    return (jnp.concatenate([cos, zero, cos, zero], axis=1), jnp.concatenate([-sin, zero, sin, zero], axis=1))


def _dft_tables(seq):
    f = jnp.arange(seq, dtype=jnp.int32)
    ph = ((2 * f[:, None] + 1) * f[None, :]) % (4 * seq)
    ang = ph.astype(F32) * (math.pi / (2 * seq))
    c, s = jnp.cos(ang).astype(BF16), jnp.sin(ang).astype(BF16)
    return c, s, c.T, s.T


def _filter_features(seq):
    t = jnp.linspace(0.0, 1.0, seq, dtype=F32)[:, None]
    w = 2.0 * math.pi * jnp.arange(seq, dtype=F32)[:, None] / seq
    bands = jnp.linspace(1e-4, N_BANDS - 1, N_BANDS, dtype=F32)[None, :]
    z = jnp.concatenate([t, jnp.cos(bands * w), -jnp.sin(bands * w)], axis=-1)
    return jnp.pad(z, ((0, 0), (0, LANES - N_POS_FEAT)))


def _decay_rates():
    max_decay = math.log(HY_DECAY_TARGET) / HY_FAST_DECAY
    min_decay = math.log(HY_DECAY_TARGET) / HY_SLOW_DECAY
    return jnp.abs(jnp.linspace(min_decay, max_decay, HY_DIM, dtype=F32))[None, :]


def _spread_rope(a):
    zero = jnp.zeros(a.shape[:-1] + (ROPE_HALF,), a.dtype)
    return jnp.concatenate([a[..., :ROPE_HALF], zero, a[..., ROPE_HALF:], zero], axis=-1)


def _pad_to(a, shape):
    return jnp.pad(a, [(0, t - s) for s, t in zip(a.shape, shape)])


def _layer_weights(p, l):
    row = lambda a: a[l][None, :].astype(F32)
    w_in = p["w_in"][l]
    o1, o2, o3 = POOL_DIM + Q_LORA, POOL_DIM + Q_LORA + KV_LORA, POOL_DIM + Q_LORA + KV_LORA + QK_ROPE
    w_in_p = jnp.concatenate([w_in[:, o3:], w_in[:, :o2], _spread_rope(w_in[:, o2:o3])], axis=1).astype(BF16)
    wq = p["mla_w_uq"][l].reshape(Q_LORA, MLA_HEADS, QK_HEAD)
    wq_p = jnp.concatenate([wq[..., :QK_NOPE], _spread_rope(wq[..., QK_NOPE:])], axis=-1)
    wkv = p["mla_w_ukv"][l].reshape(KV_LORA, MLA_HEADS, QK_NOPE + V_HEAD)
    qhg, khg = p["mla_q_head_norm_g"][l], p["mla_k_head_norm_g"][l]
    hid = (LANES, LANES)
    return dict(
        attn_g=row(p["attn_norm_g"]), w_in=w_in_p,
        pool_w=p["pool_w"][l].astype(BF16), pool_scale=row(p["pool_scale"]),
        qn=row(p["mla_q_norm_g"]), wq=wq_p.reshape(Q_LORA, MLA_HEADS * HEAD_PAD).astype(BF16),
        kvn=row(p["mla_kv_norm_g"]),
        wk=wkv[..., :QK_NOPE].reshape(KV_LORA, MLA_HEADS * QK_NOPE).astype(BF16),
        wv=wkv[..., QK_NOPE:].reshape(KV_LORA, MLA_DIM).astype(BF16),
        qhg=jnp.concatenate([qhg[:QK_NOPE], _spread_rope(qhg[QK_NOPE:])])[None, :],
        khn=khg[None, :QK_NOPE], khr=_spread_rope(khg[QK_NOPE:])[None, :],
        hy_cw=p["hy_conv_w"][l], hy_cb=row(p["hy_conv_b"]), hy_bias=row(p["hy_bias"]),
        f_w1=_pad_to(p["hy_filt_w1"][l], hid), f_b1=_pad_to(row(p["hy_filt_b1"]), (1, LANES)),
        f_fr=_pad_to(row(p["hy_filt_freq"]), (1, LANES)),
        f_w2=_pad_to(p["hy_filt_w2"][l], hid), f_b2=_pad_to(row(p["hy_filt_b2"]), (1, LANES)),
        f_w3=_pad_to(p["hy_filt_w3"][l], (LANES, 2 * HY_DIM)), f_b3=row(p["hy_filt_b3"]),
        g_pool=p["grp_norm_g"][l][None, :POOL_DIM], g_mla=p["grp_norm_g"][l][None, POOL_DIM:POOL_DIM + MLA_DIM],
        g_hy=p["grp_norm_g"][l][None, POOL_DIM + MLA_DIM:],
        w_out=p["w_out"][l].astype(BF16),
        ffn_g=row(p["ffn_norm_g"]), w_up=p["ffn_w_up"][l].astype(BF16), ffn_cw=p["ffn_conv_w"][l],
        ffn_cb=row(p["ffn_conv_b"]), w_down=p["ffn_w_down"][l].astype(BF16),
    )


def _tile(n, want):
    return min(n, want)


def _layer(x2, b, seq, w, kr, ki, consts):
    cos_t, sin_t, cmat, smat, cmat_t, smat_t = consts
    m = b * seq
    proj2 = _in_proj(x2, w["attn_g"], w["w_in"], _tile(seq, 512))
    proj3 = proj2.reshape(b, seq, D_IN_P)
    m_pool = _pool(proj3, w["pool_w"], w["pool_scale"], w["g_pool"])
    q2, k2, v2 = _mla_qkv(proj2, seq, cos_t, sin_t, w["qn"], w["wq"], w["kvn"], w["wk"], w["wv"],
                          w["qhg"], w["khn"], w["khr"], _tile(seq, 512))
    hp = MLA_HEADS * HEAD_PAD
    m_mla = _attention(q2.reshape(b, seq, hp), k2.reshape(b, seq, hp), v2.reshape(b, seq, MLA_DIM),
                       w["g_mla"], _tile(seq, 256))
    z, x0 = _hy_pre(proj3, w["hy_cw"], w["hy_cb"])
    yr, yi = _hy_fwd(cmat, smat, z, kr, ki, _tile(seq, 1024))
    m_hy = _hy_inv(cmat_t, smat_t, yr, yi, z, x0, w["hy_bias"], w["g_hy"], _tile(seq, 1024))
    x2 = _out_proj(x2, m_pool.reshape(m, POOL_DIM), m_mla.reshape(m, MLA_DIM), m_hy.reshape(m, HY_DIM),
                   w["w_out"], _tile(seq, 512))
    return _ffn(x2, seq, w["ffn_g"], w["w_up"], w["ffn_cw"], w["ffn_cb"], w["w_down"], _tile(seq, 512), 512)


def kernel(x_prompt, x_sample, attn_norm_g, w_in, pool_w, pool_scale, mla_q_norm_g, mla_w_uq, mla_kv_norm_g, mla_w_ukv, mla_q_head_norm_g, mla_k_head_norm_g, hy_conv_w, hy_conv_b, hy_filt_w1, hy_filt_b1, hy_filt_freq, hy_filt_w2, hy_filt_b2, hy_filt_w3, hy_filt_b3, hy_bias, grp_norm_g, w_out, ffn_norm_g, ffn_w_up, ffn_conv_w, ffn_conv_b, ffn_w_down):
    p = dict(attn_norm_g=attn_norm_g, w_in=w_in, pool_w=pool_w, pool_scale=pool_scale, mla_q_norm_g=mla_q_norm_g,
             mla_w_uq=mla_w_uq, mla_kv_norm_g=mla_kv_norm_g, mla_w_ukv=mla_w_ukv,
             mla_q_head_norm_g=mla_q_head_norm_g, mla_k_head_norm_g=mla_k_head_norm_g, hy_conv_w=hy_conv_w,
             hy_conv_b=hy_conv_b, hy_filt_w1=hy_filt_w1, hy_filt_b1=hy_filt_b1, hy_filt_freq=hy_filt_freq,
             hy_filt_w2=hy_filt_w2, hy_filt_b2=hy_filt_b2, hy_filt_w3=hy_filt_w3, hy_filt_b3=hy_filt_b3,
             hy_bias=hy_bias, grp_norm_g=grp_norm_g, w_out=w_out, ffn_norm_g=ffn_norm_g, ffn_w_up=ffn_w_up,
             ffn_conv_w=ffn_conv_w, ffn_conv_b=ffn_conv_b, ffn_w_down=ffn_w_down)
    depth = attn_norm_g.shape[0]
    seq = x_prompt.shape[1]
    assert x_sample.shape[1] == seq and x_prompt.shape[2] == D_MODEL
    consts = _rope_tables(seq) + _dft_tables(seq)
    feat = _filter_features(seq)
    rates = _decay_rates()
    xs = [x.reshape(-1, D_MODEL) for x in (x_prompt, x_sample)]
    for l in range(depth):
        w = _layer_weights(p, l)
        hs, hd = _hy_filter(feat, w["f_w1"], w["f_b1"], w["f_fr"], w["f_w2"], w["f_b2"], w["f_w3"], w["f_b3"],
                            rates, _tile(seq, 512))
        kr, ki = _hy_spec(consts[2], consts[3], hs, hd, _tile(seq, 512))
        xs = [_layer(x2, x.shape[0], seq, w, kr, ki, consts) for x2, x in zip(xs, (x_prompt, x_sample))]
    return (xs[0].reshape(x_prompt.shape), xs[1].reshape(x_sample.shape))
```

```python
import functools
import math

import jax
import jax.numpy as jnp
from jax import lax
from jax.experimental import pallas as pl
from jax.experimental.pallas import tpu as pltpu

F32 = jnp.float32
BF16 = jnp.bfloat16

D_MODEL = 2048
DEPTH = 4
POOL_DIM = 512
POOL_WINDOWS = (2, 4, 8, 16)
POOL_GROUP = 128
MLA_HEADS = 8
QK_NOPE = 128
QK_ROPE = 64
QK_HEAD = QK_NOPE + QK_ROPE
V_HEAD = 128
MLA_DIM = MLA_HEADS * V_HEAD
Q_LORA = 512
KV_LORA = 256
ROPE_THETA = 10000.0
HY_DIM = 512
N_BANDS = 8
N_POS_FEAT = 1 + 2 * N_BANDS
FILT_HIDDEN = 64
HY_FAST_DECAY = 0.3
HY_SLOW_DECAY = 1.5
HY_DECAY_TARGET = 1e-2
D_FF = 5632
NORM_EPS = 1e-6

LANES = 128
HEAD_PAD = 2 * LANES
ROPE_HALF = QK_ROPE // 2
D_IN_P = 3 * HY_DIM + POOL_DIM + Q_LORA + KV_LORA + LANES
COL_POOL = 3 * HY_DIM
COL_CQ = COL_POOL + POOL_DIM
COL_CKV = COL_CQ + Q_LORA
COL_ROPE = COL_CKV + KV_LORA
VMEM_LIMIT = 56 * 1024 * 1024


def _params(*sem):
    return pltpu.CompilerParams(dimension_semantics=sem, vmem_limit_bytes=VMEM_LIMIT)


def _resident(shape):
    nd = len(shape)
    return pl.BlockSpec(shape, lambda *_: (0,) * nd, pipeline_mode=pl.Buffered(1))


def _rms(x, g, n=None):
    n = x.shape[-1] if n is None else n
    ms = jnp.sum(x * x, axis=-1, keepdims=True) * (1.0 / n)
    return x * lax.rsqrt(ms + NORM_EPS) * g


def _shift_down(x, k, row):
    return jnp.where(row >= k, pltpu.roll(x, k, 0), 0.0)


def _shift_up(x, k, row):
    n = x.shape[0]
    return jnp.where(row < n - k, pltpu.roll(x, n - k, 0), 0.0)


def _in_proj_kernel(x_ref, g_ref, w_ref, o_ref):
    h = _rms(x_ref[...], g_ref[...]).astype(BF16)
    o_ref[...] = jnp.dot(h, w_ref[...], preferred_element_type=F32).astype(BF16)


def _in_proj(x2, g, w, tm):
    m, d = x2.shape
    n = w.shape[1]
    return pl.pallas_call(
        _in_proj_kernel,
        grid=(m // tm,),
        in_specs=[pl.BlockSpec((tm, d), lambda i: (i, 0)), _resident((1, d)), _resident((d, n))],
        out_specs=pl.BlockSpec((tm, n), lambda i: (i, 0)),
        out_shape=jax.ShapeDtypeStruct((m, n), BF16),
        compiler_params=_params("parallel"),
        name="in_proj",
    )(x2, g, w)


def _pool_kernel(u_ref, w_ref, sc_ref, g_ref, o_ref, y_s):
    seq = u_ref.shape[1]
    row = lax.broadcasted_iota(jnp.int32, (seq, 1), 0)
    for gi, win in enumerate(POOL_WINDOWS):
        half = win // 2
        cols = slice(gi * POOL_GROUP, (gi + 1) * POOL_GROUP)
        x = u_ref[0, :, cols].astype(F32)
        ahead = x
        behind = _shift_down(x, 1, row)
        span = 1
        while span < half:
            ahead = ahead + _shift_up(ahead, span, row)
            behind = behind + _shift_down(behind, span, row)
            span *= 2
        tot = ahead + behind
        cnt = (jnp.minimum(row + half, seq) - jnp.maximum(row - half, 0)).astype(F32)
        pooled = (tot / cnt - x).astype(BF16)
        y_s[:, cols] = jnp.dot(pooled, w_ref[gi], preferred_element_type=F32) * sc_ref[:, cols]
    o_ref[0] = _rms(y_s[...], g_ref[...]).astype(BF16)


def _pool(proj3, pool_w, pool_scale, g):
    b, seq, _ = proj3.shape
    return pl.pallas_call(
        _pool_kernel,
        grid=(b,),
        in_specs=[pl.BlockSpec((1, seq, POOL_DIM), lambda i: (i, 0, COL_POOL // POOL_DIM)),
                  _resident(pool_w.shape), _resident((1, POOL_DIM)), _resident((1, POOL_DIM))],
        out_specs=pl.BlockSpec((1, seq, POOL_DIM), lambda i: (i, 0, 0)),
        out_shape=jax.ShapeDtypeStruct((b, seq, POOL_DIM), BF16),
        scratch_shapes=[pltpu.VMEM((seq, POOL_DIM), F32)],
        compiler_params=_params("parallel"),
        name="pool",
    )(proj3, pool_w, pool_scale, g)


def _rope(r, cos_t, sin_t):
    return r * cos_t + pltpu.roll(r, LANES // 2, 1) * sin_t


def _mla_qkv_kernel(cq_ref, ckv_ref, kr_ref, cos_ref, sin_ref, qn_ref, wq_ref, kvn_ref, wk_ref, wv_ref,
                    qhg_ref, khn_ref, khr_ref, q_ref, k_ref, v_ref):
    cos_t = cos_ref[...]
    sin_t = sin_ref[...]
    scale = QK_HEAD ** -0.5 * math.log2(math.e)
    cq =_rms(cq_ref[...].astype(F32), qn_ref[...]).astype(BF16)
    qf = jnp.dot(cq, wq_ref[...], preferred_element_type=F32)
    for h in range(MLA_HEADS):
        qh = _rms(qf[:, h * HEAD_PAD:(h + 1) * HEAD_PAD], qhg_ref[...], QK_HEAD)
        q_ref[:, h * HEAD_PAD:h * HEAD_PAD + LANES] = (qh[:, :LANES] * scale).astype(BF16)
        q_ref[:, h * HEAD_PAD + LANES:(h + 1) * HEAD_PAD] = (_rope(qh[:, LANES:], cos_t, sin_t) * scale).astype(BF16)
    ckv = _rms(ckv_ref[...].astype(F32), kvn_ref[...]).astype(BF16)
    vv = jnp.dot(ckv, wv_ref[...], preferred_element_type=F32)
    ones_col = (lax.broadcasted_iota(jnp.int32, (vv.shape[0], LANES), 1) == 0).astype(BF16)
    for h in range(MLA_HEADS):
        v_ref[:, h * HEAD_PAD:h * HEAD_PAD + LANES] = vv[:, h * V_HEAD:(h + 1) * V_HEAD].astype(BF16)
        v_ref[:, h * HEAD_PAD + LANES:(h + 1) * HEAD_PAD] = ones_col
    kn =jnp.dot(ckv, wk_ref[...], preferred_element_type=F32)
    kr = kr_ref[...].astype(F32)
    ss_r = jnp.sum(kr * kr, axis=-1, keepdims=True)
    kr_rot = _rope(kr * khr_ref[...], cos_t, sin_t)
    for h in range(MLA_HEADS):
        knh = kn[:, h * LANES:(h + 1) * LANES]
        ms = (jnp.sum(knh * knh, axis=-1, keepdims=True) + ss_r) * (1.0 / QK_HEAD)
        rs = lax.rsqrt(ms + NORM_EPS)
        k_ref[:, h * HEAD_PAD:h * HEAD_PAD + LANES] = (knh * rs * khn_ref[...]).astype(BF16)
        k_ref[:, h * HEAD_PAD + LANES:(h + 1) * HEAD_PAD] = (kr_rot * rs).astype(BF16)


def _mla_qkv(proj2, seq, cos_t, sin_t, qn, wq, kvn, wk, wv, qhg, khn, khr, tl):
    m = proj2.shape[0]
    nseq = seq // tl
    row = lambda width, col: pl.BlockSpec((tl, width), lambda i: (i, col // width))
    tab = pl.BlockSpec((tl, LANES), lambda i: (i % nseq, 0))
    hp = MLA_HEADS * HEAD_PAD
    return pl.pallas_call(
        _mla_qkv_kernel,
        grid=(m // tl,),
        in_specs=[row(Q_LORA, COL_CQ), row(KV_LORA, COL_CKV), row(LANES, COL_ROPE), tab, tab,
                  _resident(qn.shape), _resident(wq.shape), _resident(kvn.shape), _resident(wk.shape),
                  _resident(wv.shape), _resident(qhg.shape), _resident(khn.shape), _resident(khr.shape)],
        out_specs=[pl.BlockSpec((tl, hp), lambda i: (i, 0)), pl.BlockSpec((tl, hp), lambda i: (i, 0)),
                   pl.BlockSpec((tl, hp), lambda i: (i, 0))],
        out_shape=[jax.ShapeDtypeStruct((m, hp), BF16)] * 3,
        compiler_params=_params("parallel"),
        name="mla_qkv",
    )(proj2, proj2, proj2, cos_t, sin_t, qn, wq, kvn, wk, wv, qhg, khn, khr)


def _attn_kernel(q_ref, k_ref, v_ref, g_ref, o_ref, y_s):
    for h in range(MLA_HEADS):
        q = q_ref[0, :, h * HEAD_PAD:(h + 1) * HEAD_PAD]
        k = k_ref[0, :, h * HEAD_PAD:(h + 1) * HEAD_PAD]
        s = lax.dot_general(q, k, (((1,), (1,)), ((), ())), preferred_element_type=F32)
        p = jnp.exp2(s - jnp.max(s, axis=-1, keepdims=True)).astype(BF16)
        o = jnp.dot(p, v_ref[0, :, h * HEAD_PAD:(h + 1) * HEAD_PAD], preferred_element_type=F32)
        y_s[:, h * V_HEAD:(h + 1) * V_HEAD] = o[:, :V_HEAD] / o[:, V_HEAD:V_HEAD + 1]
    o_ref[0] = _rms(y_s[...], g_ref[...]).astype(BF16)


def _attention(q3, k3, v3, g, tq):
    b, seq, hp = q3.shape
    return pl.pallas_call(
        _attn_kernel,
        grid=(b, seq // tq),
        in_specs=[pl.BlockSpec((1, tq, hp), lambda i, j: (i, j, 0)),
                  pl.BlockSpec((1, seq, hp), lambda i, j: (i, 0, 0)),
                  pl.BlockSpec((1, seq, hp), lambda i, j: (i, 0, 0)),
                  _resident((1, MLA_DIM))],
        out_specs=pl.BlockSpec((1, tq, MLA_DIM), lambda i, j: (i, j, 0)),
        out_shape=jax.ShapeDtypeStruct((b, seq, MLA_DIM), BF16),
        scratch_shapes=[pltpu.VMEM((tq, MLA_DIM), F32)],
        compiler_params=_params("parallel", "arbitrary"),
        name="attention",
    )(q3, k3, v3, g)


def _hy_filter_kernel(feat_ref, w1_ref, b1_ref, fr_ref, w2_ref, b2_ref, w3_ref, b3_ref, dl_ref, hs_ref, hd_ref):
    hi = lax.Precision.HIGHEST
    feat = feat_ref[...]
    fr = fr_ref[...]
    h = jnp.sin(fr * (jnp.dot(feat, w1_ref[...], precision=hi, preferred_element_type=F32) + b1_ref[...]))
    h = jnp.sin(fr * (jnp.dot(h, w2_ref[...], precision=hi, preferred_element_type=F32) + b2_ref[...]))
    h = jnp.dot(h, w3_ref[...], precision=hi, preferred_element_type=F32) + b3_ref[...]
    decay = jnp.exp(-feat[:, 0:1] * dl_ref[...])
    fwd = h[:, :HY_DIM] * decay
    row = lax.broadcasted_iota(jnp.int32, (feat.shape[0], 1), 0) + pl.program_id(0) * feat.shape[0]
    bwd = jnp.where(row > 0, h[:, HY_DIM:] * decay, 0.0)
    hs_ref[...] = (fwd + bwd).astype(BF16)
    hd_ref[...] = (bwd - fwd).astype(BF16)


def _hy_filter(feat, w1, b1, fr, w2, b2, w3, b3, dl, tl):
    seq = feat.shape[0]
    out = jax.ShapeDtypeStruct((seq, HY_DIM), BF16)
    return pl.pallas_call(
        _hy_filter_kernel,
        grid=(seq // tl,),
        in_specs=[pl.BlockSpec((tl, LANES), lambda i: (i, 0))] + [_resident(a.shape) for a in (w1, b1, fr, w2, b2, w3, b3, dl)],
        out_specs=[pl.BlockSpec((tl, HY_DIM), lambda i: (i, 0))] * 2,
        out_shape=[out, out],
        compiler_params=_params("parallel"),
        name="hy_filter",
    )(feat, w1, b1, fr, w2, b2, w3, b3, dl)


def _hy_spec_kernel(c_ref, s_ref, hs_ref, hd_ref, kr_ref, ki_ref):
    norm = 1.0 / c_ref.shape[1]
    kr_ref[...] = jnp.dot(c_ref[...], hs_ref[...], preferred_element_type=F32) * norm
    ki_ref[...] = jnp.dot(s_ref[...], hd_ref[...], preferred_element_type=F32) * norm


def _hy_spec(cmat, smat, hs, hd, tf):
    seq = cmat.shape[0]
    out = jax.ShapeDtypeStruct((seq, HY_DIM), F32)
    return pl.pallas_call(
        _hy_spec_kernel,
        grid=(seq // tf,),
        in_specs=[pl.BlockSpec((tf, seq), lambda i: (i, 0))] * 2 + [_resident(hs.shape), _resident(hd.shape)],
        out_specs=[pl.BlockSpec((tf, HY_DIM), lambda i: (i, 0))] * 2,
        out_shape=[out, out],
        compiler_params=_params("parallel"),
        name="hy_spec",
    )(cmat, smat, hs, hd)


def _hy_pre_kernel(u_ref, cw_ref, cb_ref, z_ref, x0_ref):
    seq = u_ref.shape[1]
    row = lax.broadcasted_iota(jnp.int32, (seq, 1), 0)

    def conv(col):
        cols = slice(col, col + LANES)
        x = u_ref[0, :, cols].astype(F32)
        w = cw_ref[:, cols]
        return _shift_down(x, 1, row) * w[0:1] + x * w[1:2] + _shift_up(x, 1, row) * w[2:3] + cb_ref[:, cols]

    for c in range(HY_DIM // LANES):
        cols = slice(c * LANES, (c + 1) * LANES)
        x0_ref[0, :, cols] = conv(c * LANES).astype(BF16)
        z_ref[0, :, cols] = (conv(2 * HY_DIM + c * LANES) * conv(HY_DIM + c * LANES)).astype(BF16)


def _hy_pre(proj3, cw, cb):
    b, seq, _ = proj3.shape
    out = jax.ShapeDtypeStruct((b, seq, HY_DIM), BF16)
    return pl.pallas_call(
        _hy_pre_kernel,
        grid=(b,),
        in_specs=[pl.BlockSpec((1, seq, 3 * HY_DIM), lambda i: (i, 0, 0)), _resident(cw.shape), _resident(cb.shape)],
        out_specs=[pl.BlockSpec((1, seq, HY_DIM), lambda i: (i, 0, 0))] * 2,
        out_shape=[out, out],
        compiler_params=_params("parallel"),
        name="hy_pre",
    )(proj3, cw, cb)


def _hy_fwd_kernel(c_ref, s_ref, z_ref, kr_ref, ki_ref, yr_ref, yi_ref):
    z = z_ref[0]
    u = jnp.dot(c_ref[...], z, preferred_element_type=F32)
    v = jnp.dot(s_ref[...], z, preferred_element_type=F32)
    kr = kr_ref[...]
    ki = ki_ref[...]
    yr_ref[0] = (u * kr + v * ki).astype(BF16)
    yi_ref[0] = (v * kr - u * ki).astype(BF16)


def _hy_fwd(cmat, smat, z, kr, ki, tf):
    b, seq, _ = z.shape
    out = jax.ShapeDtypeStruct((b, seq, HY_DIM), BF16)
    return pl.pallas_call(
        _hy_fwd_kernel,
        grid=(seq // tf, b),
        in_specs=[pl.BlockSpec((tf, seq), lambda f, i: (f, 0)), pl.BlockSpec((tf, seq), lambda f, i: (f, 0)),
                  pl.BlockSpec((1, seq, HY_DIM), lambda f, i: (i, 0, 0)),
                  pl.BlockSpec((tf, HY_DIM), lambda f, i: (f, 0)), pl.BlockSpec((tf, HY_DIM), lambda f, i: (f, 0))],
        out_specs=[pl.BlockSpec((1, tf, HY_DIM), lambda f, i: (i, f, 0))] * 2,
        out_shape=[out, out],
        compiler_params=_params("parallel", "arbitrary"),
        name="hy_fwd",
    )(cmat, smat, z, kr, ki)


def _hy_inv_kernel(ct_ref, st_ref, yr_ref, yi_ref, z_ref, x0_ref, b_ref, g_ref, o_ref):
    y = jnp.dot(ct_ref[...], yr_ref[0], preferred_element_type=F32)
    y = y + jnp.dot(st_ref[...], yi_ref[0], preferred_element_type=F32)
    y = (y + z_ref[0].astype(F32) * b_ref[...]) * x0_ref[0].astype(F32)
    o_ref[0] = _rms(y, g_ref[...]).astype(BF16)


def _hy_inv(cmat_t, smat_t, yr, yi, z, x0, bias, g, tt):
    b, seq, _ = z.shape
    tile = pl.BlockSpec((1, tt, HY_DIM), lambda t, i: (i, t, 0))
    full = pl.BlockSpec((1, seq, HY_DIM), lambda t, i: (i, 0, 0))
    mat = pl.BlockSpec((tt, seq), lambda t, i: (t, 0))
    return pl.pallas_call(
        _hy_inv_kernel,
        grid=(seq // tt, b),
        in_specs=[mat, mat, full, full, tile, tile, _resident((1, HY_DIM)), _resident((1, HY_DIM))],
        out_specs=tile,
        out_shape=jax.ShapeDtypeStruct((b, seq, HY_DIM), BF16),
        compiler_params=_params("parallel", "arbitrary"),
        name="hy_inv",
    )(cmat_t, smat_t, yr, yi, z, x0, bias, g)


def _out_proj_kernel(x_ref, a_ref, b_ref, c_ref, w_ref, o_ref):
    acc = jnp.dot(a_ref[...], w_ref[0:POOL_DIM, :], preferred_element_type=F32)
    acc = acc + jnp.dot(b_ref[...], w_ref[POOL_DIM:POOL_DIM + MLA_DIM, :], preferred_element_type=F32)
    acc = acc + jnp.dot(c_ref[...], w_ref[POOL_DIM + MLA_DIM:, :], preferred_element_type=F32)
    o_ref[...] = x_ref[...] + acc


def _out_proj(x2, ma, mb, mc, w, tm):
    m, d = x2.shape
    row = lambda width: pl.BlockSpec((tm, width), lambda i: (i, 0))
    return pl.pallas_call(
        _out_proj_kernel,
        grid=(m // tm,),
        in_specs=[row(d), row(POOL_DIM), row(MLA_DIM), row(HY_DIM), _resident(w.shape)],
        out_specs=row(d),
        out_shape=jax.ShapeDtypeStruct((m, d), F32),
        compiler_params=_params("parallel"),
        name="out_proj",
    )(x2, ma, mb, mc, w)


HALO = 16


def _ffn_kernel(x_ref, xp_ref, xn_ref, g_ref, wg_ref, wv_ref, cw_ref, cb_ref, wd_ref, o_ref, h_s, up_s, *, blocks_per_seq):
    tm = x_ref.shape[0]
    f = pl.program_id(1)

    @pl.when(f == 0)
    def _():
        x = x_ref[...]
        g = g_ref[...]
        o_ref[...] = x
        h_s[HALO:HALO + tm, :] = _rms(x, g).astype(BF16)
        pos = pl.program_id(0) % blocks_per_seq
        pad = jnp.zeros((HALO - 8, x.shape[1]), F32)
        hp = jnp.where(pos == 0, 0.0, _rms(xp_ref[...], g))
        hn = jnp.where(pos == blocks_per_seq - 1, 0.0, _rms(xn_ref[...], g))
        h_s[0:HALO, :] = jnp.concatenate([pad, hp], axis=0).astype(BF16)
        h_s[HALO + tm:, :] = jnp.concatenate([hn, pad], axis=0).astype(BF16)

    up_s[...] = jnp.dot(h_s[...], wg_ref[...], preferred_element_type=F32)
    val = jnp.dot(h_s[HALO:HALO + tm, :], wv_ref[...], preferred_element_type=F32)
    cw = cw_ref[...]
    gate = (up_s[HALO - 1:HALO - 1 + tm, :] * cw[0:1] + up_s[HALO:HALO + tm, :] * cw[1:2]
            + up_s[HALO + 1:HALO + 1 + tm, :] * cw[2:3] + cb_ref[...])
    act = (gate * jax.nn.sigmoid(gate) * val).astype(BF16)
    o_ref[...] += jnp.dot(act, wd_ref[...], preferred_element_type=F32)


def _ffn(x2, seq, g, w_up, cw, cb, w_down, tm, tf):
    m, d = x2.shape
    nf = D_FF // tf
    r8 = tm // 8
    last8 = m // 8 - 1
    kern = functools.partial(_ffn_kernel, blocks_per_seq=seq // tm)
    return pl.pallas_call(
        kern,
        grid=(m // tm, nf),
        in_specs=[pl.BlockSpec((tm, d), lambda i, f: (i, 0)),
                  pl.BlockSpec((8, d), lambda i, f: (jnp.maximum(i * r8 - 1, 0), 0)),
                  pl.BlockSpec((8, d), lambda i, f: (jnp.minimum((i + 1) * r8, last8), 0)),
                  _resident((1, d)),
                  pl.BlockSpec((d, tf), lambda i, f: (0, f)),
                  pl.BlockSpec((d, tf), lambda i, f: (0, f + nf)),
                  pl.BlockSpec((3, tf), lambda i, f: (0, f)),
                  pl.BlockSpec((1, tf), lambda i, f: (0, f)),
                  pl.BlockSpec((tf, d), lambda i, f: (f, 0))],
        out_specs=pl.BlockSpec((tm, d), lambda i, f: (i, 0)),
        out_shape=jax.ShapeDtypeStruct((m, d), F32),
        scratch_shapes=[pltpu.VMEM((tm + 2 * HALO, d), BF16), pltpu.VMEM((tm + 2 * HALO, tf), F32)],
        compiler_params=_params("parallel", "arbitrary"),
        name="ffn",
    )(x2, x2, x2, g, w_up, w_up, cw, cb, w_down)


def _rope_tables(seq):
    freqs = ROPE_THETA ** (-jnp.arange(0, QK_ROPE, 2, dtype=F32) / QK_ROPE)
    ang = jnp.arange(seq, dtype=F32)[:, None] * freqs[None, :]
    cos, sin, zero = jnp.cos(ang), jnp.sin(ang), jnp.zeros((seq, ROPE_HALF), F32)
    return (jnp.concatenate([cos, zero, cos, zero], axis=1), jnp.concatenate([-sin, zero, sin, zero], axis=1))


def _dft_tables(seq):
    f = jnp.arange(seq, dtype=jnp.int32)
    ph = ((2 * f[:, None] + 1) * f[None, :]) % (4 * seq)
    ang = ph.astype(F32) * (math.pi / (2 * seq))
    c, s = jnp.cos(ang).astype(BF16), jnp.sin(ang).astype(BF16)
    return c, s, c.T, s.T


def _filter_features(seq):
    t = jnp.linspace(0.0, 1.0, seq, dtype=F32)[:, None]
    w = 2.0 * math.pi * jnp.arange(seq, dtype=F32)[:, None] / seq
    bands = jnp.linspace(1e-4, N_BANDS - 1, N_BANDS, dtype=F32)[None, :]
    z = jnp.concatenate([t, jnp.cos(bands * w), -jnp.sin(bands * w)], axis=-1)
    return jnp.pad(z, ((0, 0), (0, LANES - N_POS_FEAT)))


def _decay_rates():
    max_decay = math.log(HY_DECAY_TARGET) / HY_FAST_DECAY
    min_decay = math.log(HY_DECAY_TARGET) / HY_SLOW_DECAY
    return jnp.abs(jnp.linspace(min_decay, max_decay, HY_DIM, dtype=F32))[None, :]


def _spread_rope(a):
    zero = jnp.zeros(a.shape[:-1] + (ROPE_HALF,), a.dtype)
    return jnp.concatenate([a[..., :ROPE_HALF], zero, a[..., ROPE_HALF:], zero], axis=-1)


def _pad_to(a, shape):
    return jnp.pad(a, [(0, t - s) for s, t in zip(a.shape, shape)])


def _layer_weights(p, l):
    row = lambda a: a[l][None, :].astype(F32)
    w_in = p["w_in"][l]
    o1, o2, o3 = POOL_DIM + Q_LORA, POOL_DIM + Q_LORA + KV_LORA, POOL_DIM + Q_LORA + KV_LORA + QK_ROPE
    w_in_p = jnp.concatenate([w_in[:, o3:], w_in[:, :o2], _spread_rope(w_in[:, o2:o3])], axis=1).astype(BF16)
    wq = p["mla_w_uq"][l].reshape(Q_LORA, MLA_HEADS, QK_HEAD)
    wq_p = jnp.concatenate([wq[..., :QK_NOPE], _spread_rope(wq[..., QK_NOPE:])], axis=-1)
    wkv = p["mla_w_ukv"][l].reshape(KV_LORA, MLA_HEADS, QK_NOPE + V_HEAD)
    qhg, khg = p["mla_q_head_norm_g"][l], p["mla_k_head_norm_g"][l]
    hid = (LANES, LANES)
    return dict(
        attn_g=row(p["attn_norm_g"]), w_in=w_in_p,
        pool_w=p["pool_w"][l].astype(BF16), pool_scale=row(p["pool_scale"]),
        qn=row(p["mla_q_norm_g"]), wq=wq_p.reshape(Q_LORA, MLA_HEADS * HEAD_PAD).astype(BF16),
        kvn=row(p["mla_kv_norm_g"]),
        wk=wkv[..., :QK_NOPE].reshape(KV_LORA, MLA_HEADS * QK_NOPE).astype(BF16),
        wv=wkv[..., QK_NOPE:].reshape(KV_LORA, MLA_DIM).astype(BF16),
        qhg=jnp.concatenate([qhg[:QK_NOPE], _spread_rope(qhg[QK_NOPE:])])[None, :],
        khn=khg[None, :QK_NOPE], khr=_spread_rope(khg[QK_NOPE:])[None, :],
        hy_cw=p["hy_conv_w"][l], hy_cb=row(p["hy_conv_b"]), hy_bias=row(p["hy_bias"]),
        f_w1=_pad_to(p["hy_filt_w1"][l], hid), f_b1=_pad_to(row(p["hy_filt_b1"]), (1, LANES)),
        f_fr=_pad_to(row(p["hy_filt_freq"]), (1, LANES)),
        f_w2=_pad_to(p["hy_filt_w2"][l], hid), f_b2=_pad_to(row(p["hy_filt_b2"]), (1, LANES)),
        f_w3=_pad_to(p["hy_filt_w3"][l], (LANES, 2 * HY_DIM)), f_b3=row(p["hy_filt_b3"]),
        g_pool=p["grp_norm_g"][l][None, :POOL_DIM], g_mla=p["grp_norm_g"][l][None, POOL_DIM:POOL_DIM + MLA_DIM],
        g_hy=p["grp_norm_g"][l][None, POOL_DIM + MLA_DIM:],
        w_out=p["w_out"][l].astype(BF16),
        ffn_g=row(p["ffn_norm_g"]), w_up=p["ffn_w_up"][l].astype(BF16), ffn_cw=p["ffn_conv_w"][l],
        ffn_cb=row(p["ffn_conv_b"]), w_down=p["ffn_w_down"][l].astype(BF16),
    )


def _tile(n, want):
    return min(n, want)


def _layer(x2, b, seq, w, kr, ki, consts):
    cos_t, sin_t, cmat, smat, cmat_t, smat_t = consts
    m = b * seq
    proj2 = _in_proj(x2, w["attn_g"], w["w_in"], _tile(seq, 512))
    proj3 = proj2.reshape(b, seq, D_IN_P)
    m_pool = _pool(proj3, w["pool_w"], w["pool_scale"], w["g_pool"])
    q2, k2, v2 = _mla_qkv(proj2, seq, cos_t, sin_t, w["qn"], w["wq"], w["kvn"], w["wk"], w["wv"],
                          w["qhg"], w["khn"], w["khr"], _tile(seq, 512))
    hp = MLA_HEADS * HEAD_PAD
    m_mla = _attention(q2.reshape(b, seq, hp), k2.reshape(b, seq, hp), v2.reshape(b, seq, hp),
                       w["g_mla"], _tile(seq, 256))
    z, x0 = _hy_pre(proj3, w["hy_cw"], w["hy_cb"])
    yr, yi = _hy_fwd(cmat, smat, z, kr, ki, _tile(seq, 1024))
    m_hy = _hy_inv(cmat_t, smat_t, yr, yi, z, x0, w["hy_bias"], w["g_hy"], _tile(seq, 1024))
    x2 = _out_proj(x2, m_pool.reshape(m, POOL_DIM), m_mla.reshape(m, MLA_DIM), m_hy.reshape(m, HY_DIM),
                   w["w_out"], _tile(seq, 512))
    return _ffn(x2, seq, w["ffn_g"], w["w_up"], w["ffn_cw"], w["ffn_cb"], w["w_down"], _tile(seq, 1024), 512)


def kernel(x_prompt, x_sample, attn_norm_g, w_in, pool_w, pool_scale, mla_q_norm_g, mla_w_uq, mla_kv_norm_g, mla_w_ukv, mla_q_head_norm_g, mla_k_head_norm_g, hy_conv_w, hy_conv_b, hy_filt_w1, hy_filt_b1, hy_filt_freq, hy_filt_w2, hy_filt_b2, hy_filt_w3, hy_filt_b3, hy_bias, grp_norm_g, w_out, ffn_norm_g, ffn_w_up, ffn_conv_w, ffn_conv_b, ffn_w_down):
    p = dict(attn_norm_g=attn_norm_g, w_in=w_in, pool_w=pool_w, pool_scale=pool_scale, mla_q_norm_g=mla_q_norm_g,
             mla_w_uq=mla_w_uq, mla_kv_norm_g=mla_kv_norm_g, mla_w_ukv=mla_w_ukv,
             mla_q_head_norm_g=mla_q_head_norm_g, mla_k_head_norm_g=mla_k_head_norm_g, hy_conv_w=hy_conv_w,
             hy_conv_b=hy_conv_b, hy_filt_w1=hy_filt_w1, hy_filt_b1=hy_filt_b1, hy_filt_freq=hy_filt_freq,
             hy_filt_w2=hy_filt_w2, hy_filt_b2=hy_filt_b2, hy_filt_w3=hy_filt_w3, hy_filt_b3=hy_filt_b3,
             hy_bias=hy_bias, grp_norm_g=grp_norm_g, w_out=w_out, ffn_norm_g=ffn_norm_g, ffn_w_up=ffn_w_up,
             ffn_conv_w=ffn_conv_w, ffn_conv_b=ffn_conv_b, ffn_w_down=ffn_w_down)
    depth = attn_norm_g.shape[0]
    seq = x_prompt.shape[1]
    assert x_sample.shape[1] == seq and x_prompt.shape[2] == D_MODEL
    consts = _rope_tables(seq) + _dft_tables(seq)
    feat = _filter_features(seq)
    rates = _decay_rates()
    xs = [x.reshape(-1, D_MODEL) for x in (x_prompt, x_sample)]
    for l in range(depth):
        w = _layer_weights(p, l)
        hs, hd = _hy_filter(feat, w["f_w1"], w["f_b1"], w["f_fr"], w["f_w2"], w["f_b2"], w["f_w3"], w["f_b3"],
                            rates, _tile(seq, 512))
        kr, ki = _hy_spec(consts[2], consts[3], hs, hd, _tile(seq, 512))
        xs = [_layer(x2, x.shape[0], seq, w, kr, ki, consts) for x2, x in zip(xs, (x_prompt, x_sample))]
    return (xs[0].reshape(x_prompt.shape), xs[1].reshape(x_sample.shape))
```

```python
import functools
import math

import jax
import jax.numpy as jnp
from jax import lax
from jax.experimental import pallas as pl
from jax.experimental.pallas import tpu as pltpu

F32 = jnp.float32
BF16 = jnp.bfloat16

D_MODEL = 2048
DEPTH = 4
POOL_DIM = 512
POOL_WINDOWS = (2, 4, 8, 16)
POOL_GROUP = 128
MLA_HEADS = 8
QK_NOPE = 128
QK_ROPE = 64
QK_HEAD = QK_NOPE + QK_ROPE
V_HEAD = 128
MLA_DIM = MLA_HEADS * V_HEAD
Q_LORA = 512
KV_LORA = 256
ROPE_THETA = 10000.0
HY_DIM = 512
N_BANDS = 8
N_POS_FEAT = 1 + 2 * N_BANDS
FILT_HIDDEN = 64
HY_FAST_DECAY = 0.3
HY_SLOW_DECAY = 1.5
HY_DECAY_TARGET = 1e-2
D_FF = 5632
NORM_EPS = 1e-6

LANES = 128
HEAD_PAD = 2 * LANES
ROPE_HALF = QK_ROPE // 2
HALF_CH = 2 * LANES
D_IN_P = 3 * HY_DIM + POOL_DIM + Q_LORA + KV_LORA + LANES
COL_POOL = 3 * HY_DIM
COL_CQ = COL_POOL + POOL_DIM
COL_CKV = COL_CQ + Q_LORA
COL_ROPE = COL_CKV + KV_LORA
VMEM_LIMIT = 56 * 1024 * 1024


def _params(*sem):
    return pltpu.CompilerParams(dimension_semantics=sem, vmem_limit_bytes=VMEM_LIMIT)


def _resident(shape):
    nd = len(shape)
    return pl.BlockSpec(shape, lambda *_: (0,) * nd, pipeline_mode=pl.Buffered(1))


def _rms(x, g, n=None):
    n = x.shape[-1] if n is None else n
    ms = jnp.sum(x * x, axis=-1, keepdims=True) * (1.0 / n)
    return x * lax.rsqrt(ms + NORM_EPS) * g


def _shift_down(x, k, row):
    return jnp.where(row >= k, pltpu.roll(x, k, 0), 0.0)


def _shift_up(x, k, row):
    n = x.shape[0]
    return jnp.where(row < n - k, pltpu.roll(x, n - k, 0), 0.0)


def _in_proj_kernel(x_ref, g_ref, w_ref, o_ref):
    h = _rms(x_ref[...], g_ref[...]).astype(BF16)
    o_ref[...] = jnp.dot(h, w_ref[...], preferred_element_type=F32).astype(BF16)


def _in_proj(x2, g, w, tm):
    m, d = x2.shape
    n = w.shape[1]
    return pl.pallas_call(
        _in_proj_kernel,
        grid=(m // tm,),
        in_specs=[pl.BlockSpec((tm, d), lambda i: (i, 0)), _resident((1, d)), _resident((d, n))],
        out_specs=pl.BlockSpec((tm, n), lambda i: (i, 0)),
        out_shape=jax.ShapeDtypeStruct((m, n), BF16),
        compiler_params=_params("parallel"),
        name="in_proj",
    )(x2, g, w)


def _pool_kernel(u_ref, w_ref, sc_ref, g_ref, o_ref, y_s):
    seq = u_ref.shape[1]
    row = lax.broadcasted_iota(jnp.int32, (seq, 1), 0)
    for gi, win in enumerate(POOL_WINDOWS):
        half = win // 2
        cols = slice(gi * POOL_GROUP, (gi + 1) * POOL_GROUP)
        x = u_ref[0, :, cols].astype(F32)
        ahead = x
        behind = _shift_down(x, 1, row)
        span = 1
        while span < half:
            ahead = ahead + _shift_up(ahead, span, row)
            behind = behind + _shift_down(behind, span, row)
            span *= 2
        tot = ahead + behind
        cnt = (jnp.minimum(row + half, seq) - jnp.maximum(row - half, 0)).astype(F32)
        pooled = (tot / cnt - x).astype(BF16)
        y_s[:, cols] = jnp.dot(pooled, w_ref[gi], preferred_element_type=F32) * sc_ref[:, cols]
    o_ref[0] = _rms(y_s[...], g_ref[...]).astype(BF16)


def _pool(proj3, pool_w, pool_scale, g):
    b, seq, _ = proj3.shape
    return pl.pallas_call(
        _pool_kernel,
        grid=(b,),
        in_specs=[pl.BlockSpec((1, seq, POOL_DIM), lambda i: (i, 0, COL_POOL // POOL_DIM)),
                  _resident(pool_w.shape), _resident((1, POOL_DIM)), _resident((1, POOL_DIM))],
        out_specs=pl.BlockSpec((1, seq, POOL_DIM), lambda i: (i, 0, 0)),
        out_shape=jax.ShapeDtypeStruct((b, seq, POOL_DIM), BF16),
        scratch_shapes=[pltpu.VMEM((seq, POOL_DIM), F32)],
        compiler_params=_params("parallel"),
        name="pool",
    )(proj3, pool_w, pool_scale, g)


def _rope(r, cos_t, sin_t):
    return r * cos_t + pltpu.roll(r, LANES // 2, 1) * sin_t


def _mla_qkv_kernel(cq_ref, ckv_ref, kr_ref, cos_ref, sin_ref, qn_ref, wq_ref, kvn_ref, wk_ref, wv_ref,
                    qhg_ref, khn_ref, khr_ref, q_ref, k_ref, v_ref):
    cos_t = cos_ref[...]
    sin_t = sin_ref[...]
    scale = QK_HEAD ** -0.5 * math.log2(math.e)
    cq =_rms(cq_ref[...].astype(F32), qn_ref[...]).astype(BF16)
    qf = jnp.dot(cq, wq_ref[...], preferred_element_type=F32)
    for h in range(MLA_HEADS):
        qh = _rms(qf[:, h * HEAD_PAD:(h + 1) * HEAD_PAD], qhg_ref[...], QK_HEAD)
        q_ref[:, h * HEAD_PAD:h * HEAD_PAD + LANES] = (qh[:, :LANES] * scale).astype(BF16)
        q_ref[:, h * HEAD_PAD + LANES:(h + 1) * HEAD_PAD] = (_rope(qh[:, LANES:], cos_t, sin_t) * scale).astype(BF16)
    ckv = _rms(ckv_ref[...].astype(F32), kvn_ref[...]).astype(BF16)
    vv = jnp.dot(ckv, wv_ref[...], preferred_element_type=F32)
    ones_col = (lax.broadcasted_iota(jnp.int32, (vv.shape[0], LANES), 1) == 0).astype(BF16)
    for h in range(MLA_HEADS):
        v_ref[:, h * HEAD_PAD:h * HEAD_PAD + LANES] = vv[:, h * V_HEAD:(h + 1) * V_HEAD].astype(BF16)
        v_ref[:, h * HEAD_PAD + LANES:(h + 1) * HEAD_PAD] = ones_col
    kn =jnp.dot(ckv, wk_ref[...], preferred_element_type=F32)
    kr = kr_ref[...].astype(F32)
    ss_r = jnp.sum(kr * kr, axis=-1, keepdims=True)
    kr_rot = _rope(kr * khr_ref[...], cos_t, sin_t)
    for h in range(MLA_HEADS):
        knh = kn[:, h * LANES:(h + 1) * LANES]
        ms = (jnp.sum(knh * knh, axis=-1, keepdims=True) + ss_r) * (1.0 / QK_HEAD)
        rs = lax.rsqrt(ms + NORM_EPS)
        k_ref[:, h * HEAD_PAD:h * HEAD_PAD + LANES] = (knh * rs * khn_ref[...]).astype(BF16)
        k_ref[:, h * HEAD_PAD + LANES:(h + 1) * HEAD_PAD] = (kr_rot * rs).astype(BF16)


def _mla_qkv(proj2, seq, cos_t, sin_t, qn, wq, kvn, wk, wv, qhg, khn, khr, tl):
    m = proj2.shape[0]
    nseq = seq // tl
    row = lambda width, col: pl.BlockSpec((tl, width), lambda i: (i, col // width))
    tab = pl.BlockSpec((tl, LANES), lambda i: (i % nseq, 0))
    hp = MLA_HEADS * HEAD_PAD
    return pl.pallas_call(
        _mla_qkv_kernel,
        grid=(m // tl,),
        in_specs=[row(Q_LORA, COL_CQ), row(KV_LORA, COL_CKV), row(LANES, COL_ROPE), tab, tab,
                  _resident(qn.shape), _resident(wq.shape), _resident(kvn.shape), _resident(wk.shape),
                  _resident(wv.shape), _resident(qhg.shape), _resident(khn.shape), _resident(khr.shape)],
        out_specs=[pl.BlockSpec((tl, hp), lambda i: (i, 0)), pl.BlockSpec((tl, hp), lambda i: (i, 0)),
                   pl.BlockSpec((tl, hp), lambda i: (i, 0))],
        out_shape=[jax.ShapeDtypeStruct((m, hp), BF16)] * 3,
        compiler_params=_params("parallel"),
        name="mla_qkv",
    )(proj2, proj2, proj2, cos_t, sin_t, qn, wq, kvn, wk, wv, qhg, khn, khr)


def _attn_kernel(q_ref, k_ref, v_ref, g_ref, o_ref, y_s):
    for h in range(MLA_HEADS):
        q = q_ref[0, :, h * HEAD_PAD:(h + 1) * HEAD_PAD]
        k = k_ref[0, :, h * HEAD_PAD:(h + 1) * HEAD_PAD]
        s = lax.dot_general(q, k, (((1,), (1,)), ((), ())), preferred_element_type=F32)
        p = jnp.exp2(s - jnp.max(s, axis=-1, keepdims=True)).astype(BF16)
        o = jnp.dot(p, v_ref[0, :, h * HEAD_PAD:(h + 1) * HEAD_PAD], preferred_element_type=F32)
        y_s[:, h * V_HEAD:(h + 1) * V_HEAD] = o[:, :V_HEAD] / o[:, V_HEAD:V_HEAD + 1]
    o_ref[0] = _rms(y_s[...], g_ref[...]).astype(BF16)


def _attention(q3, k3, v3, g, tq):
    b, seq, hp = q3.shape
    return pl.pallas_call(
        _attn_kernel,
        grid=(b, seq // tq),
        in_specs=[pl.BlockSpec((1, tq, hp), lambda i, j: (i, j, 0)),
                  pl.BlockSpec((1, seq, hp), lambda i, j: (i, 0, 0)),
                  pl.BlockSpec((1, seq, hp), lambda i, j: (i, 0, 0)),
                  _resident((1, MLA_DIM))],
        out_specs=pl.BlockSpec((1, tq, MLA_DIM), lambda i, j: (i, j, 0)),
        out_shape=jax.ShapeDtypeStruct((b, seq, MLA_DIM), BF16),
        scratch_shapes=[pltpu.VMEM((tq, MLA_DIM), F32)],
        compiler_params=_params("parallel", "arbitrary"),
        name="attention",
    )(q3, k3, v3, g)


def _hy_filter_kernel(feat_ref, w1_ref, b1_ref, fr_ref, w2_ref, b2_ref, w3_ref, b3_ref, dl_ref, hs_ref, hd_ref):
    hi = lax.Precision.HIGHEST
    feat = feat_ref[...]
    fr = fr_ref[...]
    h = jnp.sin(fr * (jnp.dot(feat, w1_ref[...], precision=hi, preferred_element_type=F32) + b1_ref[...]))
    h = jnp.sin(fr * (jnp.dot(h, w2_ref[...], precision=hi, preferred_element_type=F32) + b2_ref[...]))
    h = jnp.dot(h, w3_ref[...], precision=hi, preferred_element_type=F32) + b3_ref[...]
    decay = jnp.exp(-feat[:, 0:1] * dl_ref[...])
    fwd = h[:, :HY_DIM] * decay
    row = lax.broadcasted_iota(jnp.int32, (feat.shape[0], 1), 0) + pl.program_id(0) * feat.shape[0]
    bwd = jnp.where(row > 0, h[:, HY_DIM:] * decay, 0.0)
    hs_ref[...] = (fwd + bwd).astype(BF16)
    hd_ref[...] = (bwd - fwd).astype(BF16)


def _hy_filter(feat, w1, b1, fr, w2, b2, w3, b3, dl, tl):
    seq = feat.shape[0]
    out = jax.ShapeDtypeStruct((seq, HY_DIM), BF16)
    return pl.pallas_call(
        _hy_filter_kernel,
        grid=(seq // tl,),
        in_specs=[pl.BlockSpec((tl, LANES), lambda i: (i, 0))] + [_resident(a.shape) for a in (w1, b1, fr, w2, b2, w3, b3, dl)],
        out_specs=[pl.BlockSpec((tl, HY_DIM), lambda i: (i, 0))] * 2,
        out_shape=[out, out],
        compiler_params=_params("parallel"),
        name="hy_filter",
    )(feat, w1, b1, fr, w2, b2, w3, b3, dl)


def _mm(a, b):
    return jnp.dot(a, b, preferred_element_type=F32)


def _hy_spec_kernel(ce_ref, co_ref, se_ref, so_ref, hs_ref, hd_ref, kr_ref, ki_ref):
    norm = 0.5 / ce_ref.shape[1]
    ae, ao = _mm(ce_ref[...], hs_ref[0]), _mm(co_ref[...], hs_ref[1])
    be, bo = _mm(se_ref[...], hd_ref[0]), _mm(so_ref[...], hd_ref[1])
    kr_ref[0] = (ae + ao) * norm
    kr_ref[1] = (ae - ao) * norm
    ki_ref[0] = (be + bo) * norm
    ki_ref[1] = (bo - be) * norm


def _hy_spec(tables, hs, hd):
    half = hs.shape[1]
    blk = pl.BlockSpec((2, half, HALF_CH), lambda c: (0, 0, c))
    out = jax.ShapeDtypeStruct((2, half, HY_DIM), F32)
    return pl.pallas_call(
        _hy_spec_kernel,
        grid=(HY_DIM // HALF_CH,),
        in_specs=[_resident(t.shape) for t in tables[:4]] + [blk, blk],
        out_specs=[blk, blk],
        out_shape=[out, out],
        compiler_params=_params("parallel"),
        name="hy_spec",
    )(*tables[:4], hs, hd)


def _hy_pre_kernel(u_ref, cw_ref, cb_ref, z_ref, x0_ref, t_s):
    seq = u_ref.shape[1]
    half = seq // 2
    row = lax.broadcasted_iota(jnp.int32, (seq, 1), 0)

    def conv(col):
        cols = slice(col, col + LANES)
        x = u_ref[0, :, cols].astype(F32)
        w = cw_ref[:, cols]
        return _shift_down(x, 1, row) * w[0:1] + x * w[1:2] + _shift_up(x, 1, row) * w[2:3] + cb_ref[:, cols]

    def store_split(dst_ref, cols, val):
        t_s[...] = val
        dst_ref[0, 0, :, cols] = t_s[pl.ds(0, half, stride=2), :].astype(BF16)
        dst_ref[0, 1, :, cols] = t_s[pl.ds(1, half, stride=2), :].astype(BF16)

    for c in range(HY_DIM // LANES):
        cols = slice(c * LANES, (c + 1) * LANES)
        store_split(x0_ref, cols, conv(c * LANES))
        store_split(z_ref, cols, conv(2 * HY_DIM + c * LANES) * conv(HY_DIM + c * LANES))


def _hy_pre(proj3, cw, cb):
    b, seq, _ = proj3.shape
    out = jax.ShapeDtypeStruct((b, 2, seq // 2, HY_DIM), BF16)
    return pl.pallas_call(
        _hy_pre_kernel,
        grid=(b,),
        in_specs=[pl.BlockSpec((1, seq, 3 * HY_DIM), lambda i: (i, 0, 0)), _resident(cw.shape), _resident(cb.shape)],
        out_specs=[pl.BlockSpec((1, 2, seq // 2, HY_DIM), lambda i: (i, 0, 0, 0))] * 2,
        out_shape=[out, out],
        scratch_shapes=[pltpu.VMEM((seq, LANES), F32)],
        compiler_params=_params("parallel"),
        name="hy_pre",
    )(proj3, cw, cb)


def _hy_conv_kernel(z_ref, x0_ref, ce_ref, co_ref, se_ref, so_ref, cet_ref, cot_ref, set_ref, sot_ref,
                    kr_ref, ki_ref, b_ref, g_ref, o_ref, y_s):
    c = pl.program_id(1)
    half = z_ref.shape[2]
    ze, zo = z_ref[0, 0], z_ref[0, 1]
    ae, ao = _mm(ce_ref[...], ze), _mm(co_ref[...], zo)
    be, bo = _mm(se_ref[...], ze), _mm(so_ref[...], zo)
    u_lo, u_hi, v_lo, v_hi = ae + ao, ae - ao, be + bo, bo - be
    kr_lo, kr_hi, ki_lo, ki_hi = kr_ref[0], kr_ref[1], ki_ref[0], ki_ref[1]
    yr_lo, w_lo = u_lo * kr_lo + v_lo * ki_lo, v_lo * kr_lo - u_lo * ki_lo
    yr_hi, w_hi = u_hi * kr_hi + v_hi * ki_hi, v_hi * kr_hi - u_hi * ki_hi
    ye = _mm(cet_ref[...], (yr_lo + yr_hi).astype(BF16)) + _mm(set_ref[...], (w_lo - w_hi).astype(BF16))
    yo = _mm(cot_ref[...], (yr_lo - yr_hi).astype(BF16)) + _mm(sot_ref[...], (w_lo + w_hi).astype(BF16))
    bias = b_ref[...]
    ye = (ye + ze.astype(F32) * bias) * x0_ref[0, 0].astype(F32)
    yo = (yo + zo.astype(F32) * bias) * x0_ref[0, 1].astype(F32)
    per = HALF_CH // LANES
    for k in range(per):
        y_s[c * per + k, pl.ds(0, half, stride=2), :] = ye[:, k * LANES:(k + 1) * LANES]
        y_s[c * per + k, pl.ds(1, half, stride=2), :] = yo[:, k * LANES:(k + 1) * LANES]

    @pl.when(c == pl.num_programs(1) - 1)
    def _():
        slabs = [y_s[k] for k in range(HY_DIM // LANES)]
        ms = sum(jnp.sum(y * y, axis=-1, keepdims=True) for y in slabs) * (1.0 / HY_DIM)
        rs = lax.rsqrt(ms + NORM_EPS)
        for k, y in enumerate(slabs):
            cols = slice(k * LANES, (k + 1) * LANES)
            o_ref[0, :, cols] = (y * rs * g_ref[:, cols]).astype(BF16)


def _hy_conv(tables, z, x0, kr, ki, bias, g):
    b, _, half, _ = z.shape
    data = pl.BlockSpec((1, 2, half, HALF_CH), lambda i, c: (i, 0, 0, c))
    spec = pl.BlockSpec((2, half, HALF_CH), lambda i, c: (0, 0, c))
    return pl.pallas_call(
        _hy_conv_kernel,
        grid=(b, HY_DIM // HALF_CH),
        in_specs=[data, data] + [_resident(t.shape) for t in tables] + [spec, spec,
                  pl.BlockSpec((1, HALF_CH), lambda i, c: (0, c)), _resident((1, HY_DIM))],
        out_specs=pl.BlockSpec((1, 2 * half, HY_DIM), lambda i, c: (i, 0, 0)),
        out_shape=jax.ShapeDtypeStruct((b, 2 * half, HY_DIM), BF16),
        scratch_shapes=[pltpu.VMEM((HY_DIM // LANES, 2 * half, LANES), F32)],
        compiler_params=_params("parallel", "arbitrary"),
        name="hy_conv",
    )(z, x0, *tables, kr, ki, bias, g)


def _out_proj_kernel(x_ref, a_ref, b_ref, c_ref, w_ref, o_ref):
    acc = jnp.dot(a_ref[...], w_ref[0:POOL_DIM, :], preferred_element_type=F32)
    acc = acc + jnp.dot(b_ref[...], w_ref[POOL_DIM:POOL_DIM + MLA_DIM, :], preferred_element_type=F32)
    acc = acc + jnp.dot(c_ref[...], w_ref[POOL_DIM + MLA_DIM:, :], preferred_element_type=F32)
    o_ref[...] = x_ref[...] + acc


def _out_proj(x2, ma, mb, mc, w, tm):
    m, d = x2.shape
    row = lambda width: pl.BlockSpec((tm, width), lambda i: (i, 0))
    return pl.pallas_call(
        _out_proj_kernel,
        grid=(m // tm,),
        in_specs=[row(d), row(POOL_DIM), row(MLA_DIM), row(HY_DIM), _resident(w.shape)],
        out_specs=row(d),
        out_shape=jax.ShapeDtypeStruct((m, d), F32),
        compiler_params=_params("parallel"),
        name="out_proj",
    )(x2, ma, mb, mc, w)


HALO = 16


def _ffn_kernel(x_ref, xp_ref, xn_ref, g_ref, wg_ref, wv_ref, cw_ref, cb_ref, wd_ref, o_ref, h_s, up_s, *, blocks_per_seq):
    tm = x_ref.shape[0]
    f = pl.program_id(1)

    @pl.when(f == 0)
    def _():
        x = x_ref[...]
        g = g_ref[...]
        o_ref[...] = x
        h_s[HALO:HALO + tm, :] = _rms(x, g).astype(BF16)
        pos = pl.program_id(0) % blocks_per_seq
        pad = jnp.zeros((HALO - 8, x.shape[1]), F32)
        hp = jnp.where(pos == 0, 0.0, _rms(xp_ref[...], g))
        hn = jnp.where(pos == blocks_per_seq - 1, 0.0, _rms(xn_ref[...], g))
        h_s[0:HALO, :] = jnp.concatenate([pad, hp], axis=0).astype(BF16)
        h_s[HALO + tm:, :] = jnp.concatenate([hn, pad], axis=0).astype(BF16)

    up_s[...] = jnp.dot(h_s[...], wg_ref[...], preferred_element_type=F32)
    val = jnp.dot(h_s[HALO:HALO + tm, :], wv_ref[...], preferred_element_type=F32)
    cw = cw_ref[...]
    gate = (up_s[HALO - 1:HALO - 1 + tm, :] * cw[0:1] + up_s[HALO:HALO + tm, :] * cw[1:2]
            + up_s[HALO + 1:HALO + 1 + tm, :] * cw[2:3] + cb_ref[...])
    act = (gate * jax.nn.sigmoid(gate) * val).astype(BF16)
    o_ref[...] += jnp.dot(act, wd_ref[...], preferred_element_type=F32)


def _ffn(x2, seq, g, w_up, cw, cb, w_down, tm, tf):
    m, d = x2.shape
    nf = D_FF // tf
    r8 = tm // 8
    last8 = m // 8 - 1
    kern = functools.partial(_ffn_kernel, blocks_per_seq=seq // tm)
    return pl.pallas_call(
        kern,
        grid=(m // tm, nf),
        in_specs=[pl.BlockSpec((tm, d), lambda i, f: (i, 0)),
                  pl.BlockSpec((8, d), lambda i, f: (jnp.maximum(i * r8 - 1, 0), 0)),
                  pl.BlockSpec((8, d), lambda i, f: (jnp.minimum((i + 1) * r8, last8), 0)),
                  _resident((1, d)),
                  pl.BlockSpec((d, tf), lambda i, f: (0, f)),
                  pl.BlockSpec((d, tf), lambda i, f: (0, f + nf)),
                  pl.BlockSpec((3, tf), lambda i, f: (0, f)),
                  pl.BlockSpec((1, tf), lambda i, f: (0, f)),
                  pl.BlockSpec((tf, d), lambda i, f: (f, 0))],
        out_specs=pl.BlockSpec((tm, d), lambda i, f: (i, 0)),
        out_shape=jax.ShapeDtypeStruct((m, d), F32),
        scratch_shapes=[pltpu.VMEM((tm + 2 * HALO, d), BF16), pltpu.VMEM((tm + 2 * HALO, tf), F32)],
        compiler_params=_params("parallel", "arbitrary"),
        name="ffn",
    )(x2, x2, x2, g, w_up, w_up, cw, cb, w_down)


def _rope_tables(seq):
    freqs = ROPE_THETA ** (-jnp.arange(0, QK_ROPE, 2, dtype=F32) / QK_ROPE)
    ang = jnp.arange(seq, dtype=F32)[:, None] * freqs[None, :]
    cos, sin, zero = jnp.cos(ang), jnp.sin(ang), jnp.zeros((seq, ROPE_HALF), F32)
    return (jnp.concatenate([cos, zero, cos, zero], axis=1), jnp.concatenate([-sin, zero, sin, zero], axis=1))


def _dft_tables(seq):
    half = seq // 2
    f = jnp.arange(half, dtype=jnp.int32)[:, None]
    j = jnp.arange(half, dtype=jnp.int32)[None, :]
    out = []
    for tok in (2 * j, 2 * j + 1):
        ang = (((2 * f + 1) * tok) % (4 * seq)).astype(F32) * (math.pi / (2 * seq))
        out.append((jnp.cos(ang).astype(BF16), jnp.sin(ang).astype(BF16)))
    (ce, se), (co, so) = out
    return ce, co, se, so, ce.T, co.T, se.T, so.T


def _filter_features(seq):
    t = jnp.linspace(0.0, 1.0, seq, dtype=F32)[:, None]
    w = 2.0 * math.pi * jnp.arange(seq, dtype=F32)[:, None] / seq
    bands = jnp.linspace(1e-4, N_BANDS - 1, N_BANDS, dtype=F32)[None, :]
    z = jnp.concatenate([t, jnp.cos(bands * w), -jnp.sin(bands * w)], axis=-1)
    z = jnp.concatenate([z[0::2], z[1::2]], axis=0)
    return jnp.pad(z, ((0, 0), (0, LANES - N_POS_FEAT)))


def _decay_rates():
    max_decay = math.log(HY_DECAY_TARGET) / HY_FAST_DECAY
    min_decay = math.log(HY_DECAY_TARGET) / HY_SLOW_DECAY
    return jnp.abs(jnp.linspace(min_decay, max_decay, HY_DIM, dtype=F32))[None, :]


def _spread_rope(a):
    zero = jnp.zeros(a.shape[:-1] + (ROPE_HALF,), a.dtype)
    return jnp.concatenate([a[..., :ROPE_HALF], zero, a[..., ROPE_HALF:], zero], axis=-1)


def _pad_to(a, shape):
    return jnp.pad(a, [(0, t - s) for s, t in zip(a.shape, shape)])


def _layer_weights(p, l):
    row = lambda a: a[l][None, :].astype(F32)
    w_in = p["w_in"][l]
    o1, o2, o3 = POOL_DIM + Q_LORA, POOL_DIM + Q_LORA + KV_LORA, POOL_DIM + Q_LORA + KV_LORA + QK_ROPE
    w_in_p = jnp.concatenate([w_in[:, o3:], w_in[:, :o2], _spread_rope(w_in[:, o2:o3])], axis=1).astype(BF16)
    wq = p["mla_w_uq"][l].reshape(Q_LORA, MLA_HEADS, QK_HEAD)
    wq_p = jnp.concatenate([wq[..., :QK_NOPE], _spread_rope(wq[..., QK_NOPE:])], axis=-1)
    wkv = p["mla_w_ukv"][l].reshape(KV_LORA, MLA_HEADS, QK_NOPE + V_HEAD)
    qhg, khg = p["mla_q_head_norm_g"][l], p["mla_k_head_norm_g"][l]
    hid = (LANES, LANES)
    return dict(
        attn_g=row(p["attn_norm_g"]), w_in=w_in_p,
        pool_w=p["pool_w"][l].astype(BF16), pool_scale=row(p["pool_scale"]),
        qn=row(p["mla_q_norm_g"]), wq=wq_p.reshape(Q_LORA, MLA_HEADS * HEAD_PAD).astype(BF16),
        kvn=row(p["mla_kv_norm_g"]),
        wk=wkv[..., :QK_NOPE].reshape(KV_LORA, MLA_HEADS * QK_NOPE).astype(BF16),
        wv=wkv[..., QK_NOPE:].reshape(KV_LORA, MLA_DIM).astype(BF16),
        qhg=jnp.concatenate([qhg[:QK_NOPE], _spread_rope(qhg[QK_NOPE:])])[None, :],
        khn=khg[None, :QK_NOPE], khr=_spread_rope(khg[QK_NOPE:])[None, :],
        hy_cw=p["hy_conv_w"][l], hy_cb=row(p["hy_conv_b"]), hy_bias=row(p["hy_bias"]),
        f_w1=_pad_to(p["hy_filt_w1"][l], hid), f_b1=_pad_to(row(p["hy_filt_b1"]), (1, LANES)),
        f_fr=_pad_to(row(p["hy_filt_freq"]), (1, LANES)),
        f_w2=_pad_to(p["hy_filt_w2"][l], hid), f_b2=_pad_to(row(p["hy_filt_b2"]), (1, LANES)),
        f_w3=_pad_to(p["hy_filt_w3"][l], (LANES, 2 * HY_DIM)), f_b3=row(p["hy_filt_b3"]),
        g_pool=p["grp_norm_g"][l][None, :POOL_DIM], g_mla=p["grp_norm_g"][l][None, POOL_DIM:POOL_DIM + MLA_DIM],
        g_hy=p["grp_norm_g"][l][None, POOL_DIM + MLA_DIM:],
        w_out=p["w_out"][l].astype(BF16),
        ffn_g=row(p["ffn_norm_g"]), w_up=p["ffn_w_up"][l].astype(BF16), ffn_cw=p["ffn_conv_w"][l],
        ffn_cb=row(p["ffn_conv_b"]), w_down=p["ffn_w_down"][l].astype(BF16),
    )


def _tile(n, want):
    return min(n, want)


def _layer(x2, b, seq, w, kr, ki, rope, tables):
    cos_t, sin_t = rope
    m = b * seq
    proj2 = _in_proj(x2, w["attn_g"], w["w_in"], _tile(seq, 512))
    proj3 = proj2.reshape(b, seq, D_IN_P)
    m_pool = _pool(proj3, w["pool_w"], w["pool_scale"], w["g_pool"])
    q2, k2, v2 = _mla_qkv(proj2, seq, cos_t, sin_t, w["qn"], w["wq"], w["kvn"], w["wk"], w["wv"],
                          w["qhg"], w["khn"], w["khr"], _tile(seq, 512))
    hp = MLA_HEADS * HEAD_PAD
    m_mla = _attention(q2.reshape(b, seq, hp), k2.reshape(b, seq, hp), v2.reshape(b, seq, hp),
                       w["g_mla"], _tile(seq, 256))
    z, x0 = _hy_pre(proj3, w["hy_cw"], w["hy_cb"])
    m_hy = _hy_conv(tables, z, x0, kr, ki, w["hy_bias"], w["g_hy"])
    x2 = _out_proj(x2, m_pool.reshape(m, POOL_DIM), m_mla.reshape(m, MLA_DIM), m_hy.reshape(m, HY_DIM),
                   w["w_out"], _tile(seq, 512))
    return _ffn(x2, seq, w["ffn_g"], w["w_up"], w["ffn_cw"], w["ffn_cb"], w["w_down"], _tile(seq, 1024), 512)


def kernel(x_prompt, x_sample, attn_norm_g, w_in, pool_w, pool_scale, mla_q_norm_g, mla_w_uq, mla_kv_norm_g, mla_w_ukv, mla_q_head_norm_g, mla_k_head_norm_g, hy_conv_w, hy_conv_b, hy_filt_w1, hy_filt_b1, hy_filt_freq, hy_filt_w2, hy_filt_b2, hy_filt_w3, hy_filt_b3, hy_bias, grp_norm_g, w_out, ffn_norm_g, ffn_w_up, ffn_conv_w, ffn_conv_b, ffn_w_down):
    p = dict(attn_norm_g=attn_norm_g, w_in=w_in, pool_w=pool_w, pool_scale=pool_scale, mla_q_norm_g=mla_q_norm_g,
             mla_w_uq=mla_w_uq, mla_kv_norm_g=mla_kv_norm_g, mla_w_ukv=mla_w_ukv,
             mla_q_head_norm_g=mla_q_head_norm_g, mla_k_head_norm_g=mla_k_head_norm_g, hy_conv_w=hy_conv_w,
             hy_conv_b=hy_conv_b, hy_filt_w1=hy_filt_w1, hy_filt_b1=hy_filt_b1, hy_filt_freq=hy_filt_freq,
             hy_filt_w2=hy_filt_w2, hy_filt_b2=hy_filt_b2, hy_filt_w3=hy_filt_w3, hy_filt_b3=hy_filt_b3,
             hy_bias=hy_bias, grp_norm_g=grp_norm_g, w_out=w_out, ffn_norm_g=ffn_norm_g, ffn_w_up=ffn_w_up,
             ffn_conv_w=ffn_conv_w, ffn_conv_b=ffn_conv_b, ffn_w_down=ffn_w_down)
    depth = attn_norm_g.shape[0]
    seq = x_prompt.shape[1]
    assert x_sample.shape[1] == seq and x_prompt.shape[2] == D_MODEL
    rope = _rope_tables(seq)
    tables = _dft_tables(seq)
    feat = _filter_features(seq)
    rates = _decay_rates()
    xs = [x.reshape(-1, D_MODEL) for x in (x_prompt, x_sample)]
    for l in range(depth):
        w = _layer_weights(p, l)
        hs, hd = _hy_filter(feat, w["f_w1"], w["f_b1"], w["f_fr"], w["f_w2"], w["f_b2"], w["f_w3"], w["f_b3"],
                            rates, _tile(seq, 512))
        half = (2, seq // 2, HY_DIM)
        kr, ki = _hy_spec(tables, hs.reshape(half), hd.reshape(half))
        xs = [_layer(x2, x.shape[0], seq, w, kr, ki, rope, tables) for x2, x in zip(xs, (x_prompt, x_sample))]
    return (xs[0].reshape(x_prompt.shape), xs[1].reshape(x_sample.shape))
```

```python
import functools
import math

import jax
import jax.numpy as jnp
from jax import lax
from jax.experimental import pallas as pl
from jax.experimental.pallas import tpu as pltpu

F32 = jnp.float32
BF16 = jnp.bfloat16

D_MODEL = 2048
DEPTH = 4
POOL_DIM = 512
POOL_WINDOWS = (2, 4, 8, 16)
POOL_GROUP = 128
MLA_HEADS = 8
QK_NOPE = 128
QK_ROPE = 64
QK_HEAD = QK_NOPE + QK_ROPE
V_HEAD = 128
MLA_DIM = MLA_HEADS * V_HEAD
Q_LORA = 512
KV_LORA = 256
ROPE_THETA = 10000.0
HY_DIM = 512
N_BANDS = 8
N_POS_FEAT = 1 + 2 * N_BANDS
FILT_HIDDEN = 64
HY_FAST_DECAY = 0.3
HY_SLOW_DECAY = 1.5
HY_DECAY_TARGET = 1e-2
D_FF = 5632
NORM_EPS = 1e-6

LANES = 128
HEAD_PAD = 2 * LANES
ROPE_HALF = QK_ROPE // 2
VT_ROWS = V_HEAD + 16
HALF_CH = 2 * LANES
D_IN_P = 3 * HY_DIM + POOL_DIM + Q_LORA + KV_LORA + LANES
COL_POOL = 3 * HY_DIM
COL_CQ = COL_POOL + POOL_DIM
COL_CKV = COL_CQ + Q_LORA
COL_ROPE = COL_CKV + KV_LORA
VMEM_LIMIT = 56 * 1024 * 1024


def _params(*sem):
    return pltpu.CompilerParams(dimension_semantics=sem, vmem_limit_bytes=VMEM_LIMIT)


def _resident(shape):
    nd = len(shape)
    return pl.BlockSpec(shape, lambda *_: (0,) * nd, pipeline_mode=pl.Buffered(1))


def _rms(x, g, n=None):
    n = x.shape[-1] if n is None else n
    ms = jnp.sum(x * x, axis=-1, keepdims=True) * (1.0 / n)
    return x * lax.rsqrt(ms + NORM_EPS) * g


def _shift_down(x, k, row):
    return jnp.where(row >= k, pltpu.roll(x, k, 0), 0.0)


def _shift_up(x, k, row):
    n = x.shape[0]
    return jnp.where(row < n - k, pltpu.roll(x, n - k, 0), 0.0)


def _in_proj_kernel(x_ref, g_ref, w_ref, o_ref):
    h = _rms(x_ref[...], g_ref[...]).astype(BF16)
    o_ref[...] = jnp.dot(h, w_ref[...], preferred_element_type=F32).astype(BF16)


def _in_proj(x2, g, w, tm):
    m, d = x2.shape
    n = w.shape[1]
    return pl.pallas_call(
        _in_proj_kernel,
        grid=(m // tm,),
        in_specs=[pl.BlockSpec((tm, d), lambda i: (i, 0)), _resident((1, d)), _resident((d, n))],
        out_specs=pl.BlockSpec((tm, n), lambda i: (i, 0)),
        out_shape=jax.ShapeDtypeStruct((m, n), BF16),
        compiler_params=_params("parallel"),
        name="in_proj",
    )(x2, g, w)


def _pool_kernel(u_ref, w_ref, sc_ref, g_ref, o_ref, y_s):
    seq = u_ref.shape[1]
    row = lax.broadcasted_iota(jnp.int32, (seq, 1), 0)
    for gi, win in enumerate(POOL_WINDOWS):
        half = win // 2
        cols = slice(gi * POOL_GROUP, (gi + 1) * POOL_GROUP)
        x = u_ref[0, :, cols].astype(F32)
        ahead = x
        behind = _shift_down(x, 1, row)
        span = 1
        while span < half:
            ahead = ahead + _shift_up(ahead, span, row)
            behind = behind + _shift_down(behind, span, row)
            span *= 2
        tot = ahead + behind
        cnt = (jnp.minimum(row + half, seq) - jnp.maximum(row - half, 0)).astype(F32)
        pooled = (tot / cnt - x).astype(BF16)
        y_s[:, cols] = jnp.dot(pooled, w_ref[gi], preferred_element_type=F32) * sc_ref[:, cols]
    o_ref[0] = _rms(y_s[...], g_ref[...]).astype(BF16)


def _pool(proj3, pool_w, pool_scale, g):
    b, seq, _ = proj3.shape
    return pl.pallas_call(
        _pool_kernel,
        grid=(b,),
        in_specs=[pl.BlockSpec((1, seq, POOL_DIM), lambda i: (i, 0, COL_POOL // POOL_DIM)),
                  _resident(pool_w.shape), _resident((1, POOL_DIM)), _resident((1, POOL_DIM))],
        out_specs=pl.BlockSpec((1, seq, POOL_DIM), lambda i: (i, 0, 0)),
        out_shape=jax.ShapeDtypeStruct((b, seq, POOL_DIM), BF16),
        scratch_shapes=[pltpu.VMEM((seq, POOL_DIM), F32)],
        compiler_params=_params("parallel"),
        name="pool",
    )(proj3, pool_w, pool_scale, g)


def _rope(r, cos_t, sin_t):
    return r * cos_t + pltpu.roll(r, LANES // 2, 1) * sin_t


def _mla_qkv_kernel(cq_ref, ckv_ref, kr_ref, cos_ref, sin_ref, qn_ref, wq_ref, kvn_ref, wk_ref, wv_ref,
                    qhg_ref, khn_ref, khr_ref, q_ref, k_ref, v_ref):
    cos_t = cos_ref[...]
    sin_t = sin_ref[...]
    scale = QK_HEAD ** -0.5 * math.log2(math.e)
    cq =_rms(cq_ref[...].astype(F32), qn_ref[...]).astype(BF16)
    qf = jnp.dot(cq, wq_ref[...], preferred_element_type=F32)
    for h in range(MLA_HEADS):
        qh = _rms(qf[:, h * HEAD_PAD:(h + 1) * HEAD_PAD], qhg_ref[...], QK_HEAD)
        q_ref[:, h * HEAD_PAD:h * HEAD_PAD + LANES] = (qh[:, :LANES] * scale).astype(BF16)
        q_ref[:, h * HEAD_PAD + LANES:(h + 1) * HEAD_PAD] = (_rope(qh[:, LANES:], cos_t, sin_t) * scale).astype(BF16)
    ckv = _rms(ckv_ref[...].astype(F32), kvn_ref[...]).astype(BF16)
    vt = lax.dot_general(wv_ref[...], ckv, (((1,), (1,)), ((), ())), preferred_element_type=F32)
    ones_rows = (lax.broadcasted_iota(jnp.int32, (VT_ROWS - V_HEAD, vt.shape[1]), 0) == 0).astype(BF16)
    for h in range(MLA_HEADS):
        v_ref[0, h * VT_ROWS:h * VT_ROWS + V_HEAD, :] = vt[h * V_HEAD:(h + 1) * V_HEAD, :].astype(BF16)
        v_ref[0, h * VT_ROWS + V_HEAD:(h + 1) * VT_ROWS, :] = ones_rows
    kn =jnp.dot(ckv, wk_ref[...], preferred_element_type=F32)
    kr = kr_ref[...].astype(F32)
    ss_r = jnp.sum(kr * kr, axis=-1, keepdims=True)
    kr_rot = _rope(kr * khr_ref[...], cos_t, sin_t)
    for h in range(MLA_HEADS):
        knh = kn[:, h * LANES:(h + 1) * LANES]
        ms = (jnp.sum(knh * knh, axis=-1, keepdims=True) + ss_r) * (1.0 / QK_HEAD)
        rs = lax.rsqrt(ms + NORM_EPS)
        k_ref[:, h * HEAD_PAD:h * HEAD_PAD + LANES] = (knh * rs * khn_ref[...]).astype(BF16)
        k_ref[:, h * HEAD_PAD + LANES:(h + 1) * HEAD_PAD] = (kr_rot * rs).astype(BF16)


def _mla_qkv(proj2, seq, cos_t, sin_t, qn, wq, kvn, wk, wv, qhg, khn, khr, tl):
    m = proj2.shape[0]
    nseq = seq // tl
    row = lambda width, col: pl.BlockSpec((tl, width), lambda i: (i, col // width))
    tab = pl.BlockSpec((tl, LANES), lambda i: (i % nseq, 0))
    hp = MLA_HEADS * HEAD_PAD
    return pl.pallas_call(
        _mla_qkv_kernel,
        grid=(m // tl,),
        in_specs=[row(Q_LORA, COL_CQ), row(KV_LORA, COL_CKV), row(LANES, COL_ROPE), tab, tab,
                  _resident(qn.shape), _resident(wq.shape), _resident(kvn.shape), _resident(wk.shape),
                  _resident(wv.shape), _resident(qhg.shape), _resident(khn.shape), _resident(khr.shape)],
        out_specs=[pl.BlockSpec((tl, hp), lambda i: (i, 0)), pl.BlockSpec((tl, hp), lambda i: (i, 0)),
                   pl.BlockSpec((1, MLA_HEADS * VT_ROWS, tl), lambda i: (i // nseq, 0, i % nseq))],
        out_shape=[jax.ShapeDtypeStruct((m, hp), BF16), jax.ShapeDtypeStruct((m, hp), BF16),
                   jax.ShapeDtypeStruct((m // seq, MLA_HEADS * VT_ROWS, seq), BF16)],
        compiler_params=_params("parallel"),
        name="mla_qkv",
    )(proj2, proj2, proj2, cos_t, sin_t, qn, wq, kvn, wk, wv, qhg, khn, khr)


def _attn_kernel(q_ref, k_ref, v_ref, g_ref, o_ref, y_s):
    def scores(h):
        q = q_ref[0, :, h * HEAD_PAD:(h + 1) * HEAD_PAD]
        k = k_ref[0, :, h * HEAD_PAD:(h + 1) * HEAD_PAD]
        return lax.dot_general(k, q, (((1,), (1,)), ((), ())), preferred_element_type=F32)

    st_next = scores(0)
    for h in range(MLA_HEADS):
        st = st_next
        if h + 1 < MLA_HEADS:
            st_next = scores(h + 1)
        pt = jnp.exp2(st - jnp.max(st, axis=0, keepdims=True)).astype(BF16)
        ot = jnp.dot(v_ref[0, h * VT_ROWS:(h + 1) * VT_ROWS, :], pt, preferred_element_type=F32)
        yt = ot[:V_HEAD, :] / ot[V_HEAD:V_HEAD + 1, :]
        y_s[:, h * V_HEAD:(h + 1) * V_HEAD] = yt.T
    o_ref[0] = _rms(y_s[...], g_ref[...]).astype(BF16)


def _attention(q3, k3, v3, g, tq):
    b, seq, hp = q3.shape
    return pl.pallas_call(
        _attn_kernel,
        grid=(b, seq // tq),
        in_specs=[pl.BlockSpec((1, tq, hp), lambda i, j: (i, j, 0)),
                  pl.BlockSpec((1, seq, hp), lambda i, j: (i, 0, 0)),
                  pl.BlockSpec((1, MLA_HEADS * VT_ROWS, seq), lambda i, j: (i, 0, 0)),
                  _resident((1, MLA_DIM))],
        out_specs=pl.BlockSpec((1, tq, MLA_DIM), lambda i, j: (i, j, 0)),
        out_shape=jax.ShapeDtypeStruct((b, seq, MLA_DIM), BF16),
        scratch_shapes=[pltpu.VMEM((tq, MLA_DIM), F32)],
        compiler_params=_params("parallel", "arbitrary"),
        name="attention",
    )(q3, k3, v3, g)


def _hy_filter_kernel(feat_ref, w1_ref, b1_ref, fr_ref, w2_ref, b2_ref, w3_ref, b3_ref, dl_ref, hs_ref, hd_ref):
    hi = lax.Precision.HIGHEST
    feat = feat_ref[...]
    fr = fr_ref[...]
    h = jnp.sin(fr * (jnp.dot(feat, w1_ref[...], precision=hi, preferred_element_type=F32) + b1_ref[...]))
    h = jnp.sin(fr * (jnp.dot(h, w2_ref[...], precision=hi, preferred_element_type=F32) + b2_ref[...]))
    h = jnp.dot(h, w3_ref[...], precision=hi, preferred_element_type=F32) + b3_ref[...]
    decay = jnp.exp(-feat[:, 0:1] * dl_ref[...])
    fwd = h[:, :HY_DIM] * decay
    row = lax.broadcasted_iota(jnp.int32, (feat.shape[0], 1), 0) + pl.program_id(0) * feat.shape[0]
    bwd = jnp.where(row > 0, h[:, HY_DIM:] * decay, 0.0)
    hs_ref[...] = (fwd + bwd).astype(BF16)
    hd_ref[...] = (bwd - fwd).astype(BF16)


def _hy_filter(feat, w1, b1, fr, w2, b2, w3, b3, dl, tl):
    seq = feat.shape[0]
    out = jax.ShapeDtypeStruct((seq, HY_DIM), BF16)
    return pl.pallas_call(
        _hy_filter_kernel,
        grid=(seq // tl,),
        in_specs=[pl.BlockSpec((tl, LANES), lambda i: (i, 0))] + [_resident(a.shape) for a in (w1, b1, fr, w2, b2, w3, b3, dl)],
        out_specs=[pl.BlockSpec((tl, HY_DIM), lambda i: (i, 0))] * 2,
        out_shape=[out, out],
        compiler_params=_params("parallel"),
        name="hy_filter",
    )(feat, w1, b1, fr, w2, b2, w3, b3, dl)


def _mm(a, b):
    return jnp.dot(a, b, preferred_element_type=F32)


def _hy_spec_kernel(ce_ref, co_ref, se_ref, so_ref, hs_ref, hd_ref, kr_ref, ki_ref):
    norm = 0.5 / ce_ref.shape[1]
    ae, ao = _mm(ce_ref[...], hs_ref[0]), _mm(co_ref[...], hs_ref[1])
    be, bo = _mm(se_ref[...], hd_ref[0]), _mm(so_ref[...], hd_ref[1])
    kr_ref[0] = (ae + ao) * norm
    kr_ref[1] = (ae - ao) * norm
    ki_ref[0] = (be + bo) * norm
    ki_ref[1] = (bo - be) * norm


def _hy_spec(tables, hs, hd):
    half = hs.shape[1]
    blk = pl.BlockSpec((2, half, HALF_CH), lambda c: (0, 0, c))
    out = jax.ShapeDtypeStruct((2, half, HY_DIM), F32)
    return pl.pallas_call(
        _hy_spec_kernel,
        grid=(HY_DIM // HALF_CH,),
        in_specs=[_resident(t.shape) for t in tables[:4]] + [blk, blk],
        out_specs=[blk, blk],
        out_shape=[out, out],
        compiler_params=_params("parallel"),
        name="hy_spec",
    )(*tables[:4], hs, hd)


def _hy_pre_kernel(u_ref, cw_ref, cb_ref, z_ref, x0_ref, t_s):
    seq = u_ref.shape[1]
    half = seq // 2
    row = lax.broadcasted_iota(jnp.int32, (seq, 1), 0)

    def conv(col):
        cols = slice(col, col + LANES)
        x = u_ref[0, :, cols].astype(F32)
        w = cw_ref[:, cols]
        return _shift_down(x, 1, row) * w[0:1] + x * w[1:2] + _shift_up(x, 1, row) * w[2:3] + cb_ref[:, cols]

    def store_split(dst_ref, cols, val):
        t_s[...] = val
        dst_ref[0, 0, :, cols] = t_s[pl.ds(0, half, stride=2), :].astype(BF16)
        dst_ref[0, 1, :, cols] = t_s[pl.ds(1, half, stride=2), :].astype(BF16)

    for c in range(HY_DIM // LANES):
        cols = slice(c * LANES, (c + 1) * LANES)
        store_split(x0_ref, cols, conv(c * LANES))
        store_split(z_ref, cols, conv(2 * HY_DIM + c * LANES) * conv(HY_DIM + c * LANES))


def _hy_pre(proj3, cw, cb):
    b, seq, _ = proj3.shape
    out = jax.ShapeDtypeStruct((b, 2, seq // 2, HY_DIM), BF16)
    return pl.pallas_call(
        _hy_pre_kernel,
        grid=(b,),
        in_specs=[pl.BlockSpec((1, seq, 3 * HY_DIM), lambda i: (i, 0, 0)), _resident(cw.shape), _resident(cb.shape)],
        out_specs=[pl.BlockSpec((1, 2, seq // 2, HY_DIM), lambda i: (i, 0, 0, 0))] * 2,
        out_shape=[out, out],
        scratch_shapes=[pltpu.VMEM((seq, LANES), F32)],
        compiler_params=_params("parallel"),
        name="hy_pre",
    )(proj3, cw, cb)


def _hy_conv_kernel(z_ref, x0_ref, ce_ref, co_ref, se_ref, so_ref, cet_ref, cot_ref, set_ref, sot_ref,
                    kr_ref, ki_ref, b_ref, g_ref, o_ref, y_s):
    c = pl.program_id(1)
    half = z_ref.shape[2]
    ze, zo = z_ref[0, 0], z_ref[0, 1]
    ae, ao = _mm(ce_ref[...], ze), _mm(co_ref[...], zo)
    be, bo = _mm(se_ref[...], ze), _mm(so_ref[...], zo)
    u_lo, u_hi, v_lo, v_hi = ae + ao, ae - ao, be + bo, bo - be
    kr_lo, kr_hi, ki_lo, ki_hi = kr_ref[0], kr_ref[1], ki_ref[0], ki_ref[1]
    yr_lo, w_lo = u_lo * kr_lo + v_lo * ki_lo, v_lo * kr_lo - u_lo * ki_lo
    yr_hi, w_hi = u_hi * kr_hi + v_hi * ki_hi, v_hi * kr_hi - u_hi * ki_hi
    ye = _mm(cet_ref[...], (yr_lo + yr_hi).astype(BF16)) + _mm(set_ref[...], (w_lo - w_hi).astype(BF16))
    yo = _mm(cot_ref[...], (yr_lo - yr_hi).astype(BF16)) + _mm(sot_ref[...], (w_lo + w_hi).astype(BF16))
    bias = b_ref[...]
    ye = (ye + ze.astype(F32) * bias) * x0_ref[0, 0].astype(F32)
    yo = (yo + zo.astype(F32) * bias) * x0_ref[0, 1].astype(F32)
    per = HALF_CH // LANES
    for k in range(per):
        y_s[c * per + k, pl.ds(0, half, stride=2), :] = ye[:, k * LANES:(k + 1) * LANES]
        y_s[c * per + k, pl.ds(1, half, stride=2), :] = yo[:, k * LANES:(k + 1) * LANES]

    @pl.when(c == pl.num_programs(1) - 1)
    def _():
        slabs = [y_s[k] for k in range(HY_DIM // LANES)]
        ms = sum(jnp.sum(y * y, axis=-1, keepdims=True) for y in slabs) * (1.0 / HY_DIM)
        rs = lax.rsqrt(ms + NORM_EPS)
        for k, y in enumerate(slabs):
            cols = slice(k * LANES, (k + 1) * LANES)
            o_ref[0, :, cols] = (y * rs * g_ref[:, cols]).astype(BF16)


def _hy_conv(tables, z, x0, kr, ki, bias, g):
    b, _, half, _ = z.shape
    data = pl.BlockSpec((1, 2, half, HALF_CH), lambda i, c: (i, 0, 0, c))
    spec = pl.BlockSpec((2, half, HALF_CH), lambda i, c: (0, 0, c))
    return pl.pallas_call(
        _hy_conv_kernel,
        grid=(b, HY_DIM // HALF_CH),
        in_specs=[data, data] + [_resident(t.shape) for t in tables] + [spec, spec,
                  pl.BlockSpec((1, HALF_CH), lambda i, c: (0, c)), _resident((1, HY_DIM))],
        out_specs=pl.BlockSpec((1, 2 * half, HY_DIM), lambda i, c: (i, 0, 0)),
        out_shape=jax.ShapeDtypeStruct((b, 2 * half, HY_DIM), BF16),
        scratch_shapes=[pltpu.VMEM((HY_DIM // LANES, 2 * half, LANES), F32)],
        compiler_params=_params("parallel", "arbitrary"),
        name="hy_conv",
    )(z, x0, *tables, kr, ki, bias, g)


def _out_proj_kernel(x_ref, a_ref, b_ref, c_ref, w_ref, o_ref):
    acc = jnp.dot(a_ref[...], w_ref[0:POOL_DIM, :], preferred_element_type=F32)
    acc = acc + jnp.dot(b_ref[...], w_ref[POOL_DIM:POOL_DIM + MLA_DIM, :], preferred_element_type=F32)
    acc = acc + jnp.dot(c_ref[...], w_ref[POOL_DIM + MLA_DIM:, :], preferred_element_type=F32)
    o_ref[...] = x_ref[...] + acc


def _out_proj(x2, ma, mb, mc, w, tm):
    m, d = x2.shape
    row = lambda width: pl.BlockSpec((tm, width), lambda i: (i, 0))
    return pl.pallas_call(
        _out_proj_kernel,
        grid=(m // tm,),
        in_specs=[row(d), row(POOL_DIM), row(MLA_DIM), row(HY_DIM), _resident(w.shape)],
        out_specs=row(d),
        out_shape=jax.ShapeDtypeStruct((m, d), F32),
        compiler_params=_params("parallel"),
        name="out_proj",
    )(x2, ma, mb, mc, w)


HALO = 16


def _ffn_kernel(x_ref, xp_ref, xn_ref, g_ref, wg_ref, wv_ref, cw_ref, cb_ref, wd_ref, o_ref, h_s, up_s, *, blocks_per_seq):
    tm = x_ref.shape[0]
    f = pl.program_id(1)

    @pl.when(f == 0)
    def _():
        x = x_ref[...]
        g = g_ref[...]
        o_ref[...] = x
        h_s[HALO:HALO + tm, :] = _rms(x, g).astype(BF16)
        pos = pl.program_id(0) % blocks_per_seq
        pad = jnp.zeros((HALO - 8, x.shape[1]), F32)
        hp = jnp.where(pos == 0, 0.0, _rms(xp_ref[...], g))
        hn = jnp.where(pos == blocks_per_seq - 1, 0.0, _rms(xn_ref[...], g))
        h_s[0:HALO, :] = jnp.concatenate([pad, hp], axis=0).astype(BF16)
        h_s[HALO + tm:, :] = jnp.concatenate([hn, pad], axis=0).astype(BF16)

    up_s[...] = jnp.dot(h_s[...], wg_ref[...], preferred_element_type=F32)
    val = jnp.dot(h_s[HALO:HALO + tm, :], wv_ref[...], preferred_element_type=F32)
    cw = cw_ref[...]
    gate = (up_s[HALO - 1:HALO - 1 + tm, :] * cw[0:1] + up_s[HALO:HALO + tm, :] * cw[1:2]
            + up_s[HALO + 1:HALO + 1 + tm, :] * cw[2:3] + cb_ref[...])
    act = (gate * jax.nn.sigmoid(gate) * val).astype(BF16)
    o_ref[...] += jnp.dot(act, wd_ref[...], preferred_element_type=F32)


def _ffn(x2, seq, g, w_up, cw, cb, w_down, tm, tf):
    m, d = x2.shape
    nf = D_FF // tf
    r8 = tm // 8
    last8 = m // 8 - 1
    kern = functools.partial(_ffn_kernel, blocks_per_seq=seq // tm)
    return pl.pallas_call(
        kern,
        grid=(m // tm, nf),
        in_specs=[pl.BlockSpec((tm, d), lambda i, f: (i, 0)),
                  pl.BlockSpec((8, d), lambda i, f: (jnp.maximum(i * r8 - 1, 0), 0)),
                  pl.BlockSpec((8, d), lambda i, f: (jnp.minimum((i + 1) * r8, last8), 0)),
                  _resident((1, d)),
                  pl.BlockSpec((d, tf), lambda i, f: (0, f)),
                  pl.BlockSpec((d, tf), lambda i, f: (0, f + nf)),
                  pl.BlockSpec((3, tf), lambda i, f: (0, f)),
                  pl.BlockSpec((1, tf), lambda i, f: (0, f)),
                  pl.BlockSpec((tf, d), lambda i, f: (f, 0))],
        out_specs=pl.BlockSpec((tm, d), lambda i, f: (i, 0)),
        out_shape=jax.ShapeDtypeStruct((m, d), F32),
        scratch_shapes=[pltpu.VMEM((tm + 2 * HALO, d), BF16), pltpu.VMEM((tm + 2 * HALO, tf), F32)],
        compiler_params=_params("parallel", "arbitrary"),
        name="ffn",
    )(x2, x2, x2, g, w_up, w_up, cw, cb, w_down)


def _rope_tables(seq):
    freqs = ROPE_THETA ** (-jnp.arange(0, QK_ROPE, 2, dtype=F32) / QK_ROPE)
    ang = jnp.arange(seq, dtype=F32)[:, None] * freqs[None, :]
    cos, sin, zero = jnp.cos(ang), jnp.sin(ang), jnp.zeros((seq, ROPE_HALF), F32)
    return (jnp.concatenate([cos, zero, cos, zero], axis=1), jnp.concatenate([-sin, zero, sin, zero], axis=1))


def _dft_tables(seq):
    half = seq // 2
    f = jnp.arange(half, dtype=jnp.int32)[:, None]
    j = jnp.arange(half, dtype=jnp.int32)[None, :]
    out = []
    for tok in (2 * j, 2 * j + 1):
        ang = (((2 * f + 1) * tok) % (4 * seq)).astype(F32) * (math.pi / (2 * seq))
        out.append((jnp.cos(ang).astype(BF16), jnp.sin(ang).astype(BF16)))
    (ce, se), (co, so) = out
    return ce, co, se, so, ce.T, co.T, se.T, so.T


def _filter_features(seq):
    t = jnp.linspace(0.0, 1.0, seq, dtype=F32)[:, None]
    w = 2.0 * math.pi * jnp.arange(seq, dtype=F32)[:, None] / seq
    bands = jnp.linspace(1e-4, N_BANDS - 1, N_BANDS, dtype=F32)[None, :]
    z = jnp.concatenate([t, jnp.cos(bands * w), -jnp.sin(bands * w)], axis=-1)
    z = jnp.concatenate([z[0::2], z[1::2]], axis=0)
    return jnp.pad(z, ((0, 0), (0, LANES - N_POS_FEAT)))


def _decay_rates():
    max_decay = math.log(HY_DECAY_TARGET) / HY_FAST_DECAY
    min_decay = math.log(HY_DECAY_TARGET) / HY_SLOW_DECAY
    return jnp.abs(jnp.linspace(min_decay, max_decay, HY_DIM, dtype=F32))[None, :]


def _spread_rope(a):
    zero = jnp.zeros(a.shape[:-1] + (ROPE_HALF,), a.dtype)
    return jnp.concatenate([a[..., :ROPE_HALF], zero, a[..., ROPE_HALF:], zero], axis=-1)


def _pad_to(a, shape):
    return jnp.pad(a, [(0, t - s) for s, t in zip(a.shape, shape)])


def _layer_weights(p, l):
    row = lambda a: a[l][None, :].astype(F32)
    w_in = p["w_in"][l]
    o1, o2, o3 = POOL_DIM + Q_LORA, POOL_DIM + Q_LORA + KV_LORA, POOL_DIM + Q_LORA + KV_LORA + QK_ROPE
    w_in_p = jnp.concatenate([w_in[:, o3:], w_in[:, :o2], _spread_rope(w_in[:, o2:o3])], axis=1).astype(BF16)
    wq = p["mla_w_uq"][l].reshape(Q_LORA, MLA_HEADS, QK_HEAD)
    wq_p = jnp.concatenate([wq[..., :QK_NOPE], _spread_rope(wq[..., QK_NOPE:])], axis=-1)
    wkv = p["mla_w_ukv"][l].reshape(KV_LORA, MLA_HEADS, QK_NOPE + V_HEAD)
    qhg, khg = p["mla_q_head_norm_g"][l], p["mla_k_head_norm_g"][l]
    hid = (LANES, LANES)
    return dict(
        attn_g=row(p["attn_norm_g"]), w_in=w_in_p,
        pool_w=p["pool_w"][l].astype(BF16), pool_scale=row(p["pool_scale"]),
        qn=row(p["mla_q_norm_g"]), wq=wq_p.reshape(Q_LORA, MLA_HEADS * HEAD_PAD).astype(BF16),
        kvn=row(p["mla_kv_norm_g"]),
        wk=wkv[..., :QK_NOPE].reshape(KV_LORA, MLA_HEADS * QK_NOPE).astype(BF16),
        wv=wkv[..., QK_NOPE:].reshape(KV_LORA, MLA_DIM).T.astype(BF16),
        qhg=jnp.concatenate([qhg[:QK_NOPE], _spread_rope(qhg[QK_NOPE:])])[None, :],
        khn=khg[None, :QK_NOPE], khr=_spread_rope(khg[QK_NOPE:])[None, :],
        hy_cw=p["hy_conv_w"][l], hy_cb=row(p["hy_conv_b"]), hy_bias=row(p["hy_bias"]),
        f_w1=_pad_to(p["hy_filt_w1"][l], hid), f_b1=_pad_to(row(p["hy_filt_b1"]), (1, LANES)),
        f_fr=_pad_to(row(p["hy_filt_freq"]), (1, LANES)),
        f_w2=_pad_to(p["hy_filt_w2"][l], hid), f_b2=_pad_to(row(p["hy_filt_b2"]), (1, LANES)),
        f_w3=_pad_to(p["hy_filt_w3"][l], (LANES, 2 * HY_DIM)), f_b3=row(p["hy_filt_b3"]),
        g_pool=p["grp_norm_g"][l][None, :POOL_DIM], g_mla=p["grp_norm_g"][l][None, POOL_DIM:POOL_DIM + MLA_DIM],
        g_hy=p["grp_norm_g"][l][None, POOL_DIM + MLA_DIM:],
        w_out=p["w_out"][l].astype(BF16),
        ffn_g=row(p["ffn_norm_g"]), w_up=p["ffn_w_up"][l].astype(BF16), ffn_cw=p["ffn_conv_w"][l],
        ffn_cb=row(p["ffn_conv_b"]), w_down=p["ffn_w_down"][l].astype(BF16),
    )


def _tile(n, want):
    return min(n, want)


def _layer(x2, b, seq, w, kr, ki, rope, tables):
    cos_t, sin_t = rope
    m = b * seq
    proj2 = _in_proj(x2, w["attn_g"], w["w_in"], _tile(seq, 512))
    proj3 = proj2.reshape(b, seq, D_IN_P)
    m_pool = _pool(proj3, w["pool_w"], w["pool_scale"], w["g_pool"])
    q2, k2, vt = _mla_qkv(proj2, seq, cos_t, sin_t, w["qn"], w["wq"], w["kvn"], w["wk"], w["wv"],
                          w["qhg"], w["khn"], w["khr"], _tile(seq, 512))
    hp = MLA_HEADS * HEAD_PAD
    m_mla = _attention(q2.reshape(b, seq, hp), k2.reshape(b, seq, hp), vt,
                       w["g_mla"], _tile(seq, 512))
    z, x0 = _hy_pre(proj3, w["hy_cw"], w["hy_cb"])
    m_hy = _hy_conv(tables, z, x0, kr, ki, w["hy_bias"], w["g_hy"])
    x2 = _out_proj(x2, m_pool.reshape(m, POOL_DIM), m_mla.reshape(m, MLA_DIM), m_hy.reshape(m, HY_DIM),
                   w["w_out"], _tile(seq, 512))
    return _ffn(x2, seq, w["ffn_g"], w["w_up"], w["ffn_cw"], w["ffn_cb"], w["w_down"], _tile(seq, 1024), 512)


def kernel(x_prompt, x_sample, attn_norm_g, w_in, pool_w, pool_scale, mla_q_norm_g, mla_w_uq, mla_kv_norm_g, mla_w_ukv, mla_q_head_norm_g, mla_k_head_norm_g, hy_conv_w, hy_conv_b, hy_filt_w1, hy_filt_b1, hy_filt_freq, hy_filt_w2, hy_filt_b2, hy_filt_w3, hy_filt_b3, hy_bias, grp_norm_g, w_out, ffn_norm_g, ffn_w_up, ffn_conv_w, ffn_conv_b, ffn_w_down):
    p = dict(attn_norm_g=attn_norm_g, w_in=w_in, pool_w=pool_w, pool_scale=pool_scale, mla_q_norm_g=mla_q_norm_g,
             mla_w_uq=mla_w_uq, mla_kv_norm_g=mla_kv_norm_g, mla_w_ukv=mla_w_ukv,
             mla_q_head_norm_g=mla_q_head_norm_g, mla_k_head_norm_g=mla_k_head_norm_g, hy_conv_w=hy_conv_w,
             hy_conv_b=hy_conv_b, hy_filt_w1=hy_filt_w1, hy_filt_b1=hy_filt_b1, hy_filt_freq=hy_filt_freq,
             hy_filt_w2=hy_filt_w2, hy_filt_b2=hy_filt_b2, hy_filt_w3=hy_filt_w3, hy_filt_b3=hy_filt_b3,
             hy_bias=hy_bias, grp_norm_g=grp_norm_g, w_out=w_out, ffn_norm_g=ffn_norm_g, ffn_w_up=ffn_w_up,
             ffn_conv_w=ffn_conv_w, ffn_conv_b=ffn_conv_b, ffn_w_down=ffn_w_down)
    depth = attn_norm_g.shape[0]
    seq = x_prompt.shape[1]
    assert x_sample.shape[1] == seq and x_prompt.shape[2] == D_MODEL
    rope = _rope_tables(seq)
    tables = _dft_tables(seq)
    feat = _filter_features(seq)
    rates = _decay_rates()
    xs = [x.reshape(-1, D_MODEL) for x in (x_prompt, x_sample)]
    for l in range(depth):
        w = _layer_weights(p, l)
        hs, hd = _hy_filter(feat, w["f_w1"], w["f_b1"], w["f_fr"], w["f_w2"], w["f_b2"], w["f_w3"], w["f_b3"],
                            rates, _tile(seq, 512))
        half = (2, seq // 2, HY_DIM)
        kr, ki = _hy_spec(tables, hs.reshape(half), hd.reshape(half))
        xs = [_layer(x2, x.shape[0], seq, w, kr, ki, rope, tables) for x2, x in zip(xs, (x_prompt, x_sample))]
    return (xs[0].reshape(x_prompt.shape), xs[1].reshape(x_sample.shape))
```

```python
import functools
import math

import jax
import jax.numpy as jnp
from jax import lax
from jax.experimental import pallas as pl
from jax.experimental.pallas import tpu as pltpu

F32 = jnp.float32
BF16 = jnp.bfloat16

D_MODEL = 2048
DEPTH = 4
POOL_DIM = 512
POOL_WINDOWS = (2, 4, 8, 16)
POOL_GROUP = 128
MLA_HEADS = 8
QK_NOPE = 128
QK_ROPE = 64
QK_HEAD = QK_NOPE + QK_ROPE
V_HEAD = 128
MLA_DIM = MLA_HEADS * V_HEAD
Q_LORA = 512
KV_LORA = 256
ROPE_THETA = 10000.0
HY_DIM = 512
N_BANDS = 8
N_POS_FEAT = 1 + 2 * N_BANDS
FILT_HIDDEN = 64
HY_FAST_DECAY = 0.3
HY_SLOW_DECAY = 1.5
HY_DECAY_TARGET = 1e-2
D_FF = 5632
NORM_EPS = 1e-6

LANES = 128
HEAD_PAD = 2 * LANES
ROPE_HALF = QK_ROPE // 2
VT_ROWS = V_HEAD + 16
HALF_CH = 2 * LANES
D_IN_P = 3 * HY_DIM + POOL_DIM + Q_LORA + KV_LORA + LANES
COL_POOL = 3 * HY_DIM
COL_CQ = COL_POOL + POOL_DIM
COL_CKV = COL_CQ + Q_LORA
COL_ROPE = COL_CKV + KV_LORA
VMEM_LIMIT = 56 * 1024 * 1024


def _params(*sem):
    return pltpu.CompilerParams(dimension_semantics=sem, vmem_limit_bytes=VMEM_LIMIT)


def _resident(shape):
    nd = len(shape)
    return pl.BlockSpec(shape, lambda *_: (0,) * nd, pipeline_mode=pl.Buffered(1))


def _rms(x, g, n=None):
    n = x.shape[-1] if n is None else n
    ms = jnp.sum(x * x, axis=-1, keepdims=True) * (1.0 / n)
    return x * lax.rsqrt(ms + NORM_EPS) * g


def _shift_down(x, k, row):
    return jnp.where(row >= k, pltpu.roll(x, k, 0), 0.0)


def _shift_up(x, k, row):
    n = x.shape[0]
    return jnp.where(row < n - k, pltpu.roll(x, n - k, 0), 0.0)


def _in_proj_kernel(x_ref, g_ref, w_ref, o_ref):
    h = _rms(x_ref[...], g_ref[...]).astype(BF16)
    o_ref[...] = jnp.dot(h, w_ref[...], preferred_element_type=F32).astype(BF16)


def _in_proj(x2, g, w, tm):
    m, d = x2.shape
    n = w.shape[1]
    return pl.pallas_call(
        _in_proj_kernel,
        grid=(m // tm,),
        in_specs=[pl.BlockSpec((tm, d), lambda i: (i, 0)), _resident((1, d)), _resident((d, n))],
        out_specs=pl.BlockSpec((tm, n), lambda i: (i, 0)),
        out_shape=jax.ShapeDtypeStruct((m, n), BF16),
        compiler_params=_params("parallel"),
        name="in_proj",
    )(x2, g, w)


def _pool_kernel(u_ref, w_ref, sc_ref, g_ref, o_ref, y_s):
    seq = u_ref.shape[1]
    row = lax.broadcasted_iota(jnp.int32, (seq, 1), 0)
    for gi, win in enumerate(POOL_WINDOWS):
        half = win // 2
        cols = slice(gi * POOL_GROUP, (gi + 1) * POOL_GROUP)
        x = u_ref[0, :, cols].astype(F32)
        ahead = x
        behind = _shift_down(x, 1, row)
        span = 1
        while span < half:
            ahead = ahead + _shift_up(ahead, span, row)
            behind = behind + _shift_down(behind, span, row)
            span *= 2
        tot = ahead + behind
        cnt = (jnp.minimum(row + half, seq) - jnp.maximum(row - half, 0)).astype(F32)
        pooled = (tot / cnt - x).astype(BF16)
        y_s[:, cols] = jnp.dot(pooled, w_ref[gi], preferred_element_type=F32) * sc_ref[:, cols]
    o_ref[0] = _rms(y_s[...], g_ref[...]).astype(BF16)


def _pool(proj3, pool_w, pool_scale, g):
    b, seq, _ = proj3.shape
    return pl.pallas_call(
        _pool_kernel,
        grid=(b,),
        in_specs=[pl.BlockSpec((1, seq, POOL_DIM), lambda i: (i, 0, COL_POOL // POOL_DIM)),
                  _resident(pool_w.shape), _resident((1, POOL_DIM)), _resident((1, POOL_DIM))],
        out_specs=pl.BlockSpec((1, seq, POOL_DIM), lambda i: (i, 0, 0)),
        out_shape=jax.ShapeDtypeStruct((b, seq, POOL_DIM), BF16),
        scratch_shapes=[pltpu.VMEM((seq, POOL_DIM), F32)],
        compiler_params=_params("parallel"),
        name="pool",
    )(proj3, pool_w, pool_scale, g)


def _rope(r, cos_t, sin_t):
    return r * cos_t + pltpu.roll(r, LANES // 2, 1) * sin_t


def _mla_qkv_kernel(cq_ref, ckv_ref, kr_ref, cos_ref, sin_ref, qn_ref, wq_ref, kvn_ref, wk_ref, wv_ref,
                    qhg_ref, khn_ref, khr_ref, q_ref, k_ref, v_ref):
    cos_t = cos_ref[...]
    sin_t = sin_ref[...]
    scale = QK_HEAD ** -0.5 * math.log2(math.e)
    cq =_rms(cq_ref[...].astype(F32), qn_ref[...]).astype(BF16)
    qf = jnp.dot(cq, wq_ref[...], preferred_element_type=F32)
    for h in range(MLA_HEADS):
        qh = _rms(qf[:, h * HEAD_PAD:(h + 1) * HEAD_PAD], qhg_ref[...], QK_HEAD)
        q_ref[:, h * HEAD_PAD:h * HEAD_PAD + LANES] = (qh[:, :LANES] * scale).astype(BF16)
        q_ref[:, h * HEAD_PAD + LANES:(h + 1) * HEAD_PAD] = (_rope(qh[:, LANES:], cos_t, sin_t) * scale).astype(BF16)
    ckv = _rms(ckv_ref[...].astype(F32), kvn_ref[...]).astype(BF16)
    vt = lax.dot_general(wv_ref[...], ckv, (((1,), (1,)), ((), ())), preferred_element_type=F32)
    ones_rows = (lax.broadcasted_iota(jnp.int32, (VT_ROWS - V_HEAD, vt.shape[1]), 0) == 0).astype(BF16)
    for h in range(MLA_HEADS):
        v_ref[0, h * VT_ROWS:h * VT_ROWS + V_HEAD, :] = vt[h * V_HEAD:(h + 1) * V_HEAD, :].astype(BF16)
        v_ref[0, h * VT_ROWS + V_HEAD:(h + 1) * VT_ROWS, :] = ones_rows
    kn =jnp.dot(ckv, wk_ref[...], preferred_element_type=F32)
    kr = kr_ref[...].astype(F32)
    ss_r = jnp.sum(kr * kr, axis=-1, keepdims=True)
    kr_rot = _rope(kr * khr_ref[...], cos_t, sin_t)
    for h in range(MLA_HEADS):
        knh = kn[:, h * LANES:(h + 1) * LANES]
        ms = (jnp.sum(knh * knh, axis=-1, keepdims=True) + ss_r) * (1.0 / QK_HEAD)
        rs = lax.rsqrt(ms + NORM_EPS)
        k_ref[:, h * HEAD_PAD:h * HEAD_PAD + LANES] = (knh * rs * khn_ref[...]).astype(BF16)
        k_ref[:, h * HEAD_PAD + LANES:(h + 1) * HEAD_PAD] = (kr_rot * rs).astype(BF16)


def _mla_qkv(proj2, seq, cos_t, sin_t, qn, wq, kvn, wk, wv, qhg, khn, khr, tl):
    m = proj2.shape[0]
    nseq = seq // tl
    row = lambda width, col: pl.BlockSpec((tl, width), lambda i: (i, col // width))
    tab = pl.BlockSpec((tl, LANES), lambda i: (i % nseq, 0))
    hp = MLA_HEADS * HEAD_PAD
    return pl.pallas_call(
        _mla_qkv_kernel,
        grid=(m // tl,),
        in_specs=[row(Q_LORA, COL_CQ), row(KV_LORA, COL_CKV), row(LANES, COL_ROPE), tab, tab,
                  _resident(qn.shape), _resident(wq.shape), _resident(kvn.shape), _resident(wk.shape),
                  _resident(wv.shape), _resident(qhg.shape), _resident(khn.shape), _resident(khr.shape)],
        out_specs=[pl.BlockSpec((tl, hp), lambda i: (i, 0)), pl.BlockSpec((tl, hp), lambda i: (i, 0)),
                   pl.BlockSpec((1, MLA_HEADS * VT_ROWS, tl), lambda i: (i // nseq, 0, i % nseq))],
        out_shape=[jax.ShapeDtypeStruct((m, hp), BF16), jax.ShapeDtypeStruct((m, hp), BF16),
                   jax.ShapeDtypeStruct((m // seq, MLA_HEADS * VT_ROWS, seq), BF16)],
        compiler_params=_params("parallel"),
        name="mla_qkv",
    )(proj2, proj2, proj2, cos_t, sin_t, qn, wq, kvn, wk, wv, qhg, khn, khr)


def _attn_kernel(q_ref, k_ref, v_ref, g_ref, o_ref, y_s):
    def scores(h):
        q = q_ref[0, :, h * HEAD_PAD:(h + 1) * HEAD_PAD]
        k = k_ref[0, :, h * HEAD_PAD:(h + 1) * HEAD_PAD]
        return lax.dot_general(k, q, (((1,), (1,)), ((), ())), preferred_element_type=F32)

    st_next = scores(0)
    for h in range(MLA_HEADS):
        st = st_next
        if h + 1 < MLA_HEADS:
            st_next = scores(h + 1)
        pt = jnp.exp2((st - jnp.max(st, axis=0, keepdims=True)).astype(BF16))
        ot = jnp.dot(v_ref[0, h * VT_ROWS:(h + 1) * VT_ROWS, :], pt, preferred_element_type=F32)
        yt = ot[:V_HEAD, :] / ot[V_HEAD:V_HEAD + 1, :]
        y_s[:, h * V_HEAD:(h + 1) * V_HEAD] = yt.T
    o_ref[0] = _rms(y_s[...], g_ref[...]).astype(BF16)


def _attention(q3, k3, v3, g, tq):
    b, seq, hp = q3.shape
    return pl.pallas_call(
        _attn_kernel,
        grid=(b, seq // tq),
        in_specs=[pl.BlockSpec((1, tq, hp), lambda i, j: (i, j, 0)),
                  pl.BlockSpec((1, seq, hp), lambda i, j: (i, 0, 0)),
                  pl.BlockSpec((1, MLA_HEADS * VT_ROWS, seq), lambda i, j: (i, 0, 0)),
                  _resident((1, MLA_DIM))],
        out_specs=pl.BlockSpec((1, tq, MLA_DIM), lambda i, j: (i, j, 0)),
        out_shape=jax.ShapeDtypeStruct((b, seq, MLA_DIM), BF16),
        scratch_shapes=[pltpu.VMEM((tq, MLA_DIM), F32)],
        compiler_params=_params("parallel", "arbitrary"),
        name="attention",
    )(q3, k3, v3, g)


def _hy_filter_kernel(feat_ref, w1_ref, b1_ref, fr_ref, w2_ref, b2_ref, w3_ref, b3_ref, dl_ref, hs_ref, hd_ref):
    hi = lax.Precision.HIGHEST
    feat = feat_ref[...]
    fr = fr_ref[...]
    h = jnp.sin(fr * (jnp.dot(feat, w1_ref[...], precision=hi, preferred_element_type=F32) + b1_ref[...]))
    h = jnp.sin(fr * (jnp.dot(h, w2_ref[...], precision=hi, preferred_element_type=F32) + b2_ref[...]))
    h = jnp.dot(h, w3_ref[...], precision=hi, preferred_element_type=F32) + b3_ref[...]
    decay = jnp.exp(-feat[:, 0:1] * dl_ref[...])
    fwd = h[:, :HY_DIM] * decay
    row = lax.broadcasted_iota(jnp.int32, (feat.shape[0], 1), 0) + pl.program_id(0) * feat.shape[0]
    bwd = jnp.where(row > 0, h[:, HY_DIM:] * decay, 0.0)
    hs_ref[...] = (fwd + bwd).astype(BF16)
    hd_ref[...] = (bwd - fwd).astype(BF16)


def _hy_filter(feat, w1, b1, fr, w2, b2, w3, b3, dl, tl):
    seq = feat.shape[0]
    out = jax.ShapeDtypeStruct((seq, HY_DIM), BF16)
    return pl.pallas_call(
        _hy_filter_kernel,
        grid=(seq // tl,),
        in_specs=[pl.BlockSpec((tl, LANES), lambda i: (i, 0))] + [_resident(a.shape) for a in (w1, b1, fr, w2, b2, w3, b3, dl)],
        out_specs=[pl.BlockSpec((tl, HY_DIM), lambda i: (i, 0))] * 2,
        out_shape=[out, out],
        compiler_params=_params("parallel"),
        name="hy_filter",
    )(feat, w1, b1, fr, w2, b2, w3, b3, dl)


def _mm(a, b):
    return jnp.dot(a, b, preferred_element_type=F32)


def _hy_spec_kernel(ce_ref, co_ref, se_ref, so_ref, hs_ref, hd_ref, kr_ref, ki_ref):
    norm = 0.5 / ce_ref.shape[1]
    ae, ao = _mm(ce_ref[...], hs_ref[0]), _mm(co_ref[...], hs_ref[1])
    be, bo = _mm(se_ref[...], hd_ref[0]), _mm(so_ref[...], hd_ref[1])
    kr_ref[0] = (ae + ao) * norm
    kr_ref[1] = (ae - ao) * norm
    ki_ref[0] = (be + bo) * norm
    ki_ref[1] = (bo - be) * norm


def _hy_spec(tables, hs, hd):
    half = hs.shape[1]
    blk = pl.BlockSpec((2, half, HALF_CH), lambda c: (0, 0, c))
    out = jax.ShapeDtypeStruct((2, half, HY_DIM), F32)
    return pl.pallas_call(
        _hy_spec_kernel,
        grid=(HY_DIM // HALF_CH,),
        in_specs=[_resident(t.shape) for t in tables[:4]] + [blk, blk],
        out_specs=[blk, blk],
        out_shape=[out, out],
        compiler_params=_params("parallel"),
        name="hy_spec",
    )(*tables[:4], hs, hd)


def _hy_pre_kernel(u_ref, cw_ref, cb_ref, z_ref, x0_ref, t_s):
    seq = u_ref.shape[1]
    half = seq // 2
    row = lax.broadcasted_iota(jnp.int32, (seq, 1), 0)

    def conv(col):
        cols = slice(col, col + LANES)
        x = u_ref[0, :, cols].astype(F32)
        w = cw_ref[:, cols]
        return _shift_down(x, 1, row) * w[0:1] + x * w[1:2] + _shift_up(x, 1, row) * w[2:3] + cb_ref[:, cols]

    def store_split(dst_ref, cols, val):
        t_s[...] = val
        dst_ref[0, 0, :, cols] = t_s[pl.ds(0, half, stride=2), :].astype(BF16)
        dst_ref[0, 1, :, cols] = t_s[pl.ds(1, half, stride=2), :].astype(BF16)

    for c in range(HY_DIM // LANES):
        cols = slice(c * LANES, (c + 1) * LANES)
        store_split(x0_ref, cols, conv(c * LANES))
        store_split(z_ref, cols, conv(2 * HY_DIM + c * LANES) * conv(HY_DIM + c * LANES))


def _hy_pre(proj3, cw, cb):
    b, seq, _ = proj3.shape
    out = jax.ShapeDtypeStruct((b, 2, seq // 2, HY_DIM), BF16)
    return pl.pallas_call(
        _hy_pre_kernel,
        grid=(b,),
        in_specs=[pl.BlockSpec((1, seq, 3 * HY_DIM), lambda i: (i, 0, 0)), _resident(cw.shape), _resident(cb.shape)],
        out_specs=[pl.BlockSpec((1, 2, seq // 2, HY_DIM), lambda i: (i, 0, 0, 0))] * 2,
        out_shape=[out, out],
        scratch_shapes=[pltpu.VMEM((seq, LANES), F32)],
        compiler_params=_params("parallel"),
        name="hy_pre",
    )(proj3, cw, cb)


def _hy_conv_kernel(z_ref, x0_ref, ce_ref, co_ref, se_ref, so_ref, cet_ref, cot_ref, set_ref, sot_ref,
                    kr_ref, ki_ref, b_ref, g_ref, o_ref, y_s):
    c = pl.program_id(1)
    half = z_ref.shape[2]
    ze, zo = z_ref[0, 0], z_ref[0, 1]
    ae, ao = _mm(ce_ref[...], ze), _mm(co_ref[...], zo)
    be, bo = _mm(se_ref[...], ze), _mm(so_ref[...], zo)
    u_lo, u_hi, v_lo, v_hi = ae + ao, ae - ao, be + bo, bo - be
    kr_lo, kr_hi, ki_lo, ki_hi = kr_ref[0], kr_ref[1], ki_ref[0], ki_ref[1]
    yr_lo, w_lo = u_lo * kr_lo + v_lo * ki_lo, v_lo * kr_lo - u_lo * ki_lo
    yr_hi, w_hi = u_hi * kr_hi + v_hi * ki_hi, v_hi * kr_hi - u_hi * ki_hi
    ye = _mm(cet_ref[...], (yr_lo + yr_hi).astype(BF16)) + _mm(set_ref[...], (w_lo - w_hi).astype(BF16))
    yo = _mm(cot_ref[...], (yr_lo - yr_hi).astype(BF16)) + _mm(sot_ref[...], (w_lo + w_hi).astype(BF16))
    bias = b_ref[...]
    ye = (ye + ze.astype(F32) * bias) * x0_ref[0, 0].astype(F32)
    yo = (yo + zo.astype(F32) * bias) * x0_ref[0, 1].astype(F32)
    per = HALF_CH // LANES
    for k in range(per):
        y_s[c * per + k, pl.ds(0, half, stride=2), :] = ye[:, k * LANES:(k + 1) * LANES]
        y_s[c * per + k, pl.ds(1, half, stride=2), :] = yo[:, k * LANES:(k + 1) * LANES]

    @pl.when(c == pl.num_programs(1) - 1)
    def _():
        slabs = [y_s[k] for k in range(HY_DIM // LANES)]
        ms = sum(jnp.sum(y * y, axis=-1, keepdims=True) for y in slabs) * (1.0 / HY_DIM)
        rs = lax.rsqrt(ms + NORM_EPS)
        for k, y in enumerate(slabs):
            cols = slice(k * LANES, (k + 1) * LANES)
            o_ref[0, :, cols] = (y * rs * g_ref[:, cols]).astype(BF16)


def _hy_conv(tables, z, x0, kr, ki, bias, g):
    b, _, half, _ = z.shape
    data = pl.BlockSpec((1, 2, half, HALF_CH), lambda i, c: (i, 0, 0, c))
    spec = pl.BlockSpec((2, half, HALF_CH), lambda i, c: (0, 0, c))
    return pl.pallas_call(
        _hy_conv_kernel,
        grid=(b, HY_DIM // HALF_CH),
        in_specs=[data, data] + [_resident(t.shape) for t in tables] + [spec, spec,
                  pl.BlockSpec((1, HALF_CH), lambda i, c: (0, c)), _resident((1, HY_DIM))],
        out_specs=pl.BlockSpec((1, 2 * half, HY_DIM), lambda i, c: (i, 0, 0)),
        out_shape=jax.ShapeDtypeStruct((b, 2 * half, HY_DIM), BF16),
        scratch_shapes=[pltpu.VMEM((HY_DIM // LANES, 2 * half, LANES), F32)],
        compiler_params=_params("parallel", "arbitrary"),
        name="hy_conv",
    )(z, x0, *tables, kr, ki, bias, g)


def _out_proj_kernel(x_ref, a_ref, b_ref, c_ref, w_ref, o_ref):
    acc = jnp.dot(a_ref[...], w_ref[0:POOL_DIM, :], preferred_element_type=F32)
    acc = acc + jnp.dot(b_ref[...], w_ref[POOL_DIM:POOL_DIM + MLA_DIM, :], preferred_element_type=F32)
    acc = acc + jnp.dot(c_ref[...], w_ref[POOL_DIM + MLA_DIM:, :], preferred_element_type=F32)
    o_ref[...] = x_ref[...] + acc


def _out_proj(x2, ma, mb, mc, w, tm):
    m, d = x2.shape
    row = lambda width: pl.BlockSpec((tm, width), lambda i: (i, 0))
    return pl.pallas_call(
        _out_proj_kernel,
        grid=(m // tm,),
        in_specs=[row(d), row(POOL_DIM), row(MLA_DIM), row(HY_DIM), _resident(w.shape)],
        out_specs=row(d),
        out_shape=jax.ShapeDtypeStruct((m, d), F32),
        compiler_params=_params("parallel"),
        name="out_proj",
    )(x2, ma, mb, mc, w)


HALO = 16


def _ffn_kernel(x_ref, xp_ref, xn_ref, g_ref, wg_ref, wv_ref, cw_ref, cb_ref, wd_ref, o_ref, h_s, up_s, *, blocks_per_seq):
    tm = x_ref.shape[0]
    f = pl.program_id(1)

    @pl.when(f == 0)
    def _():
        x = x_ref[...]
        g = g_ref[...]
        o_ref[...] = x
        h_s[HALO:HALO + tm, :] = _rms(x, g).astype(BF16)
        pos = pl.program_id(0) % blocks_per_seq
        pad = jnp.zeros((HALO - 8, x.shape[1]), F32)
        hp = jnp.where(pos == 0, 0.0, _rms(xp_ref[...], g))
        hn = jnp.where(pos == blocks_per_seq - 1, 0.0, _rms(xn_ref[...], g))
        h_s[0:HALO, :] = jnp.concatenate([pad, hp], axis=0).astype(BF16)
        h_s[HALO + tm:, :] = jnp.concatenate([hn, pad], axis=0).astype(BF16)

    up_s[...] = jnp.dot(h_s[...], wg_ref[...], preferred_element_type=F32)
    val = jnp.dot(h_s[HALO:HALO + tm, :], wv_ref[...], preferred_element_type=F32)
    cw = cw_ref[...]
    gate = (up_s[HALO - 1:HALO - 1 + tm, :] * cw[0:1] + up_s[HALO:HALO + tm, :] * cw[1:2]
            + up_s[HALO + 1:HALO + 1 + tm, :] * cw[2:3] + cb_ref[...])
    act = (gate * jax.nn.sigmoid(gate) * val).astype(BF16)
    o_ref[...] += jnp.dot(act, wd_ref[...], preferred_element_type=F32)


def _ffn(x2, seq, g, w_up, cw, cb, w_down, tm, tf):
    m, d = x2.shape
    nf = D_FF // tf
    r8 = tm // 8
    last8 = m // 8 - 1
    kern = functools.partial(_ffn_kernel, blocks_per_seq=seq // tm)
    return pl.pallas_call(
        kern,
        grid=(m // tm, nf),
        in_specs=[pl.BlockSpec((tm, d), lambda i, f: (i, 0)),
                  pl.BlockSpec((8, d), lambda i, f: (jnp.maximum(i * r8 - 1, 0), 0)),
                  pl.BlockSpec((8, d), lambda i, f: (jnp.minimum((i + 1) * r8, last8), 0)),
                  _resident((1, d)),
                  pl.BlockSpec((d, tf), lambda i, f: (0, f)),
                  pl.BlockSpec((d, tf), lambda i, f: (0, f + nf)),
                  pl.BlockSpec((3, tf), lambda i, f: (0, f)),
                  pl.BlockSpec((1, tf), lambda i, f: (0, f)),
                  pl.BlockSpec((tf, d), lambda i, f: (f, 0))],
        out_specs=pl.BlockSpec((tm, d), lambda i, f: (i, 0)),
        out_shape=jax.ShapeDtypeStruct((m, d), F32),
        scratch_shapes=[pltpu.VMEM((tm + 2 * HALO, d), BF16), pltpu.VMEM((tm + 2 * HALO, tf), F32)],
        compiler_params=_params("parallel", "arbitrary"),
        name="ffn",
    )(x2, x2, x2, g, w_up, w_up, cw, cb, w_down)


def _rope_tables(seq):
    freqs = ROPE_THETA ** (-jnp.arange(0, QK_ROPE, 2, dtype=F32) / QK_ROPE)
    ang = jnp.arange(seq, dtype=F32)[:, None] * freqs[None, :]
    cos, sin, zero = jnp.cos(ang), jnp.sin(ang), jnp.zeros((seq, ROPE_HALF), F32)
    return (jnp.concatenate([cos, zero, cos, zero], axis=1), jnp.concatenate([-sin, zero, sin, zero], axis=1))


def _dft_tables(seq):
    half = seq // 2
    f = jnp.arange(half, dtype=jnp.int32)[:, None]
    j = jnp.arange(half, dtype=jnp.int32)[None, :]
    out = []
    for tok in (2 * j, 2 * j + 1):
        ang = (((2 * f + 1) * tok) % (4 * seq)).astype(F32) * (math.pi / (2 * seq))
        out.append((jnp.cos(ang).astype(BF16), jnp.sin(ang).astype(BF16)))
    (ce, se), (co, so) = out
    return ce, co, se, so, ce.T, co.T, se.T, so.T


def _filter_features(seq):
    t = jnp.linspace(0.0, 1.0, seq, dtype=F32)[:, None]
    w = 2.0 * math.pi * jnp.arange(seq, dtype=F32)[:, None] / seq
    bands = jnp.linspace(1e-4, N_BANDS - 1, N_BANDS, dtype=F32)[None, :]
    z = jnp.concatenate([t, jnp.cos(bands * w), -jnp.sin(bands * w)], axis=-1)
    z = jnp.concatenate([z[0::2], z[1::2]], axis=0)
    return jnp.pad(z, ((0, 0), (0, LANES - N_POS_FEAT)))


def _decay_rates():
    max_decay = math.log(HY_DECAY_TARGET) / HY_FAST_DECAY
    min_decay = math.log(HY_DECAY_TARGET) / HY_SLOW_DECAY
    return jnp.abs(jnp.linspace(min_decay, max_decay, HY_DIM, dtype=F32))[None, :]


def _spread_rope(a):
    zero = jnp.zeros(a.shape[:-1] + (ROPE_HALF,), a.dtype)
    return jnp.concatenate([a[..., :ROPE_HALF], zero, a[..., ROPE_HALF:], zero], axis=-1)


def _pad_to(a, shape):
    return jnp.pad(a, [(0, t - s) for s, t in zip(a.shape, shape)])


def _layer_weights(p, l):
    row = lambda a: a[l][None, :].astype(F32)
    w_in = p["w_in"][l]
    o1, o2, o3 = POOL_DIM + Q_LORA, POOL_DIM + Q_LORA + KV_LORA, POOL_DIM + Q_LORA + KV_LORA + QK_ROPE
    w_in_p = jnp.concatenate([w_in[:, o3:], w_in[:, :o2], _spread_rope(w_in[:, o2:o3])], axis=1).astype(BF16)
    wq = p["mla_w_uq"][l].reshape(Q_LORA, MLA_HEADS, QK_HEAD)
    wq_p = jnp.concatenate([wq[..., :QK_NOPE], _spread_rope(wq[..., QK_NOPE:])], axis=-1)
    wkv = p["mla_w_ukv"][l].reshape(KV_LORA, MLA_HEADS, QK_NOPE + V_HEAD)
    qhg, khg = p["mla_q_head_norm_g"][l], p["mla_k_head_norm_g"][l]
    hid = (LANES, LANES)
    return dict(
        attn_g=row(p["attn_norm_g"]), w_in=w_in_p,
        pool_w=p["pool_w"][l].astype(BF16), pool_scale=row(p["pool_scale"]),
        qn=row(p["mla_q_norm_g"]), wq=wq_p.reshape(Q_LORA, MLA_HEADS * HEAD_PAD).astype(BF16),
        kvn=row(p["mla_kv_norm_g"]),
        wk=wkv[..., :QK_NOPE].reshape(KV_LORA, MLA_HEADS * QK_NOPE).astype(BF16),
        wv=wkv[..., QK_NOPE:].reshape(KV_LORA, MLA_DIM).T.astype(BF16),
        qhg=jnp.concatenate([qhg[:QK_NOPE], _spread_rope(qhg[QK_NOPE:])])[None, :],
        khn=khg[None, :QK_NOPE], khr=_spread_rope(khg[QK_NOPE:])[None, :],
        hy_cw=p["hy_conv_w"][l], hy_cb=row(p["hy_conv_b"]), hy_bias=row(p["hy_bias"]),
        f_w1=_pad_to(p["hy_filt_w1"][l], hid), f_b1=_pad_to(row(p["hy_filt_b1"]), (1, LANES)),
        f_fr=_pad_to(row(p["hy_filt_freq"]), (1, LANES)),
        f_w2=_pad_to(p["hy_filt_w2"][l], hid), f_b2=_pad_to(row(p["hy_filt_b2"]), (1, LANES)),
        f_w3=_pad_to(p["hy_filt_w3"][l], (LANES, 2 * HY_DIM)), f_b3=row(p["hy_filt_b3"]),
        g_pool=p["grp_norm_g"][l][None, :POOL_DIM], g_mla=p["grp_norm_g"][l][None, POOL_DIM:POOL_DIM + MLA_DIM],
        g_hy=p["grp_norm_g"][l][None, POOL_DIM + MLA_DIM:],
        w_out=p["w_out"][l].astype(BF16),
        ffn_g=row(p["ffn_norm_g"]), w_up=p["ffn_w_up"][l].astype(BF16), ffn_cw=p["ffn_conv_w"][l],
        ffn_cb=row(p["ffn_conv_b"]), w_down=p["ffn_w_down"][l].astype(BF16),
    )


def _tile(n, want):
    return min(n, want)


def _layer(x2, b, seq, w, kr, ki, rope, tables):
    cos_t, sin_t = rope
    m = b * seq
    proj2 = _in_proj(x2, w["attn_g"], w["w_in"], _tile(seq, 512))
    proj3 = proj2.reshape(b, seq, D_IN_P)
    m_pool = _pool(proj3, w["pool_w"], w["pool_scale"], w["g_pool"])
    q2, k2, vt = _mla_qkv(proj2, seq, cos_t, sin_t, w["qn"], w["wq"], w["kvn"], w["wk"], w["wv"],
                          w["qhg"], w["khn"], w["khr"], _tile(seq, 512))
    hp = MLA_HEADS * HEAD_PAD
    m_mla = _attention(q2.reshape(b, seq, hp), k2.reshape(b, seq, hp), vt,
                       w["g_mla"], _tile(seq, 512))
    z, x0 = _hy_pre(proj3, w["hy_cw"], w["hy_cb"])
    m_hy = _hy_conv(tables, z, x0, kr, ki, w["hy_bias"], w["g_hy"])
    x2 = _out_proj(x2, m_pool.reshape(m, POOL_DIM), m_mla.reshape(m, MLA_DIM), m_hy.reshape(m, HY_DIM),
                   w["w_out"], _tile(seq, 512))
    return _ffn(x2, seq, w["ffn_g"], w["w_up"], w["ffn_cw"], w["ffn_cb"], w["w_down"], _tile(seq, 1024), 512)


def kernel(x_prompt, x_sample, attn_norm_g, w_in, pool_w, pool_scale, mla_q_norm_g, mla_w_uq, mla_kv_norm_g, mla_w_ukv, mla_q_head_norm_g, mla_k_head_norm_g, hy_conv_w, hy_conv_b, hy_filt_w1, hy_filt_b1, hy_filt_freq, hy_filt_w2, hy_filt_b2, hy_filt_w3, hy_filt_b3, hy_bias, grp_norm_g, w_out, ffn_norm_g, ffn_w_up, ffn_conv_w, ffn_conv_b, ffn_w_down):
    p = dict(attn_norm_g=attn_norm_g, w_in=w_in, pool_w=pool_w, pool_scale=pool_scale, mla_q_norm_g=mla_q_norm_g,
             mla_w_uq=mla_w_uq, mla_kv_norm_g=mla_kv_norm_g, mla_w_ukv=mla_w_ukv,
             mla_q_head_norm_g=mla_q_head_norm_g, mla_k_head_norm_g=mla_k_head_norm_g, hy_conv_w=hy_conv_w,
             hy_conv_b=hy_conv_b, hy_filt_w1=hy_filt_w1, hy_filt_b1=hy_filt_b1, hy_filt_freq=hy_filt_freq,
             hy_filt_w2=hy_filt_w2, hy_filt_b2=hy_filt_b2, hy_filt_w3=hy_filt_w3, hy_filt_b3=hy_filt_b3,
             hy_bias=hy_bias, grp_norm_g=grp_norm_g, w_out=w_out, ffn_norm_g=ffn_norm_g, ffn_w_up=ffn_w_up,
             ffn_conv_w=ffn_conv_w, ffn_conv_b=ffn_conv_b, ffn_w_down=ffn_w_down)
    depth = attn_norm_g.shape[0]
    seq = x_prompt.shape[1]
    assert x_sample.shape[1] == seq and x_prompt.shape[2] == D_MODEL
    rope = _rope_tables(seq)
    tables = _dft_tables(seq)
    feat = _filter_features(seq)
    rates = _decay_rates()
    xs = [x.reshape(-1, D_MODEL) for x in (x_prompt, x_sample)]
    for l in range(depth):
        w = _layer_weights(p, l)
        hs, hd = _hy_filter(feat, w["f_w1"], w["f_b1"], w["f_fr"], w["f_w2"], w["f_b2"], w["f_w3"], w["f_b3"],
                            rates, _tile(seq, 512))
        half = (2, seq // 2, HY_DIM)
        kr, ki = _hy_spec(tables, hs.reshape(half), hd.reshape(half))
        xs = [_layer(x2, x.shape[0], seq, w, kr, ki, rope, tables) for x2, x in zip(xs, (x_prompt, x_sample))]
    return (xs[0].reshape(x_prompt.shape), xs[1].reshape(x_sample.shape))
```

```python
import functools
import math

import jax
import jax.numpy as jnp
from jax import lax
from jax.experimental import pallas as pl
from jax.experimental.pallas import tpu as pltpu

F32 = jnp.float32
BF16 = jnp.bfloat16

D_MODEL = 2048
DEPTH = 4
POOL_DIM = 512
POOL_WINDOWS = (2, 4, 8, 16)
POOL_GROUP = 128
MLA_HEADS = 8
QK_NOPE = 128
QK_ROPE = 64
QK_HEAD = QK_NOPE + QK_ROPE
V_HEAD = 128
MLA_DIM = MLA_HEADS * V_HEAD
Q_LORA = 512
KV_LORA = 256
ROPE_THETA = 10000.0
HY_DIM = 512
N_BANDS = 8
N_POS_FEAT = 1 + 2 * N_BANDS
FILT_HIDDEN = 64
HY_FAST_DECAY = 0.3
HY_SLOW_DECAY = 1.5
HY_DECAY_TARGET = 1e-2
D_FF = 5632
NORM_EPS = 1e-6

LANES = 128
HEAD_PAD = 2 * LANES
ROPE_HALF = QK_ROPE // 2
VT_ROWS = V_HEAD + 16
HALF_CH = 2 * LANES
COL_POOL = 3 * HY_DIM
COL_CQ = COL_POOL + POOL_DIM
VMEM_LIMIT = 56 * 1024 * 1024


def _params(*sem):
    return pltpu.CompilerParams(dimension_semantics=sem, vmem_limit_bytes=VMEM_LIMIT)


def _resident(shape):
    nd = len(shape)
    return pl.BlockSpec(shape, lambda *_: (0,) * nd, pipeline_mode=pl.Buffered(1))


def _mm(a, b):
    return jnp.dot(a, b, preferred_element_type=F32)


def _rms(x, g, n=None):
    n = x.shape[-1] if n is None else n
    ms = jnp.sum(x * x, axis=-1, keepdims=True) * (1.0 / n)
    return x * lax.rsqrt(ms + NORM_EPS) * g


def _shift_down(x, k, row):
    return jnp.where(row >= k, pltpu.roll(x, k, 0), 0.0)


def _shift_up(x, k, row):
    n = x.shape[0]
    return jnp.where(row < n - k, pltpu.roll(x, n - k, 0), 0.0)


def _rope(r, cos_t, sin_t):
    return r * cos_t + pltpu.roll(r, LANES // 2, 1) * sin_t


def _in_proj_kernel(x_ref, g_ref, w_ref, cos_ref, sin_ref, qn_ref, wq_ref, kvn_ref, wk_ref, wv_ref,
                    qhg_ref, khn_ref, khr_ref, o_ref, q_ref, k_ref, v_ref):
    h = _rms(x_ref[...], g_ref[...]).astype(BF16)
    lat = _mm(h, w_ref[:, COL_CQ:])
    cq = _rms(lat[:, :Q_LORA], qn_ref[...]).astype(BF16)
    ckv = _rms(lat[:, Q_LORA:Q_LORA + KV_LORA], kvn_ref[...]).astype(BF16)
    kr = lat[:, Q_LORA + KV_LORA:]
    qf = _mm(cq, wq_ref[...])
    kn = _mm(ckv, wk_ref[...])
    vt = lax.dot_general(wv_ref[...], ckv, (((1,), (1,)), ((), ())), preferred_element_type=F32)
    o_ref[...] = _mm(h, w_ref[:, :COL_CQ]).astype(BF16)

    cos_t = cos_ref[...]
    sin_t = sin_ref[...]
    scale = QK_HEAD ** -0.5 * math.log2(math.e)
    qg = qhg_ref[...] * scale
    qg_n, qg_r = qg[:, :LANES], qg[:, LANES:]
    rope_a = qg_r * cos_t
    rope_b = pltpu.roll(qg_r, LANES // 2, 1) * sin_t
    for hd in range(MLA_HEADS):
        qh = qf[:, hd * HEAD_PAD:(hd + 1) * HEAD_PAD]
        rs = lax.rsqrt(jnp.sum(qh * qh, axis=-1, keepdims=True) * (1.0 / QK_HEAD) + NORM_EPS)
        qr = qh[:, LANES:]
        q_ref[:, hd * HEAD_PAD:hd * HEAD_PAD + LANES] = (qh[:, :LANES] * rs * qg_n).astype(BF16)
        q_ref[:, hd * HEAD_PAD + LANES:(hd + 1) * HEAD_PAD] = (
            (qr * rope_a + pltpu.roll(qr, LANES // 2, 1) * rope_b) * rs).astype(BF16)
    ones_rows = (lax.broadcasted_iota(jnp.int32, (VT_ROWS - V_HEAD, vt.shape[1]), 0) == 0).astype(BF16)
    for hd in range(MLA_HEADS):
        v_ref[0, hd * VT_ROWS:hd * VT_ROWS + V_HEAD, :] = vt[hd * V_HEAD:(hd + 1) * V_HEAD, :].astype(BF16)
        v_ref[0, hd * VT_ROWS + V_HEAD:(hd + 1) * VT_ROWS, :] = ones_rows
    ss_r = jnp.sum(kr * kr, axis=-1, keepdims=True)
    kr_rot = _rope(kr * khr_ref[...], cos_t, sin_t)
    for hd in range(MLA_HEADS):
        knh = kn[:, hd * LANES:(hd + 1) * LANES]
        ms = (jnp.sum(knh * knh, axis=-1, keepdims=True) + ss_r) * (1.0 / QK_HEAD)
        rs = lax.rsqrt(ms + NORM_EPS)
        k_ref[:, hd * HEAD_PAD:hd * HEAD_PAD + LANES] = (knh * rs * khn_ref[...]).astype(BF16)
        k_ref[:, hd * HEAD_PAD + LANES:(hd + 1) * HEAD_PAD] = (kr_rot * rs).astype(BF16)


def _in_proj(x2, seq, g, w, cos_t, sin_t, qn, wq, kvn, wk, wv, qhg, khn, khr, tm):
    m, d = x2.shape
    nseq = seq // tm
    hp = MLA_HEADS * HEAD_PAD
    row = lambda width: pl.BlockSpec((tm, width), lambda i: (i, 0))
    tab = pl.BlockSpec((tm, LANES), lambda i: (i % nseq, 0))
    return pl.pallas_call(
        _in_proj_kernel,
        grid=(m // tm,),
        in_specs=[row(d), _resident((1, d)), _resident(w.shape), tab, tab] +
                 [_resident(a.shape) for a in (qn, wq, kvn, wk, wv, qhg, khn, khr)],
        out_specs=[row(COL_CQ), row(hp), row(hp),
                   pl.BlockSpec((1, MLA_HEADS * VT_ROWS, tm), lambda i: (i // nseq, 0, i % nseq))],
        out_shape=[jax.ShapeDtypeStruct((m, COL_CQ), BF16), jax.ShapeDtypeStruct((m, hp), BF16),
                   jax.ShapeDtypeStruct((m, hp), BF16),
                   jax.ShapeDtypeStruct((m // seq, MLA_HEADS * VT_ROWS, seq), BF16)],
        compiler_params=_params("parallel"),
        name="in_proj",
    )(x2, g, w, cos_t, sin_t, qn, wq, kvn, wk, wv, qhg, khn, khr)


def _pool_kernel(u_ref, w_ref, sc_ref, g_ref, o_ref, y_s):
    seq = u_ref.shape[1]
    row = lax.broadcasted_iota(jnp.int32, (seq, 1), 0)
    for gi, win in enumerate(POOL_WINDOWS):
        half = win // 2
        cols = slice(gi * POOL_GROUP, (gi + 1) * POOL_GROUP)
        x = u_ref[0, :, cols].astype(F32)
        ahead = x
        behind = _shift_down(x, 1, row)
        span = 1
        while span < half:
            ahead = ahead + _shift_up(ahead, span, row)
            behind = behind + _shift_down(behind, span, row)
            span *= 2
        tot = ahead + behind
        cnt = (jnp.minimum(row + half, seq) - jnp.maximum(row - half, 0)).astype(F32)
        pooled = (tot / cnt - x).astype(BF16)
        y_s[:, cols] = jnp.dot(pooled, w_ref[gi], preferred_element_type=F32) * sc_ref[:, cols]
    o_ref[0] = _rms(y_s[...], g_ref[...]).astype(BF16)


def _pool(proj3, pool_w, pool_scale, g):
    b, seq, _ = proj3.shape
    return pl.pallas_call(
        _pool_kernel,
        grid=(b,),
        in_specs=[pl.BlockSpec((1, seq, POOL_DIM), lambda i: (i, 0, COL_POOL // POOL_DIM)),
                  _resident(pool_w.shape), _resident((1, POOL_DIM)), _resident((1, POOL_DIM))],
        out_specs=pl.BlockSpec((1, seq, POOL_DIM), lambda i: (i, 0, 0)),
        out_shape=jax.ShapeDtypeStruct((b, seq, POOL_DIM), BF16),
        scratch_shapes=[pltpu.VMEM((seq, POOL_DIM), F32)],
        compiler_params=_params("parallel"),
        name="pool",
    )(proj3, pool_w, pool_scale, g)


def _attn_kernel(q_ref, k_ref, v_ref, g_ref, o_ref, y_s):
    def scores(h):
        q = q_ref[0, :, h * HEAD_PAD:(h + 1) * HEAD_PAD]
        k = k_ref[0, :, h * HEAD_PAD:(h + 1) * HEAD_PAD]
        return lax.dot_general(k, q, (((1,), (1,)), ((), ())), preferred_element_type=F32)

    st_next = scores(0)
    for h in range(MLA_HEADS):
        st = st_next
        if h + 1 < MLA_HEADS:
            st_next = scores(h + 1)
        pt = jnp.exp2(st - jnp.max(st, axis=0, keepdims=True)).astype(BF16)
        ot = jnp.dot(v_ref[0, h * VT_ROWS:(h + 1) * VT_ROWS, :], pt, preferred_element_type=F32)
        yt = ot[:V_HEAD, :] / ot[V_HEAD:V_HEAD + 1, :]
        y_s[:, h * V_HEAD:(h + 1) * V_HEAD] = yt.T
    o_ref[0] = _rms(y_s[...], g_ref[...]).astype(BF16)


def _attention(q3, k3, v3, g, tq):
    b, seq, hp = q3.shape
    return pl.pallas_call(
        _attn_kernel,
        grid=(b, seq // tq),
        in_specs=[pl.BlockSpec((1, tq, hp), lambda i, j: (i, j, 0)),
                  pl.BlockSpec((1, seq, hp), lambda i, j: (i, 0, 0)),
                  pl.BlockSpec((1, MLA_HEADS * VT_ROWS, seq), lambda i, j: (i, 0, 0)),
                  _resident((1, MLA_DIM))],
        out_specs=pl.BlockSpec((1, tq, MLA_DIM), lambda i, j: (i, j, 0)),
        out_shape=jax.ShapeDtypeStruct((b, seq, MLA_DIM), BF16),
        scratch_shapes=[pltpu.VMEM((tq, MLA_DIM), F32)],
        compiler_params=_params("parallel", "arbitrary"),
        name="attention",
    )(q3, k3, v3, g)


def _hy_filter_kernel(feat_ref, w1_ref, b1_ref, fr_ref, w2_ref, b2_ref, w3_ref, b3_ref, dl_ref, hs_ref, hd_ref):
    hi = lax.Precision.HIGHEST
    feat = feat_ref[...]
    fr = fr_ref[...]
    h = jnp.sin(fr * (jnp.dot(feat, w1_ref[...], precision=hi, preferred_element_type=F32) + b1_ref[...]))
    h = jnp.sin(fr * (jnp.dot(h, w2_ref[...], precision=hi, preferred_element_type=F32) + b2_ref[...]))
    h = jnp.dot(h, w3_ref[...], precision=hi, preferred_element_type=F32) + b3_ref[...]
    decay = jnp.exp(-feat[:, 0:1] * dl_ref[...])
    fwd = h[:, :HY_DIM] * decay
    row = lax.broadcasted_iota(jnp.int32, (feat.shape[0], 1), 0) + pl.program_id(0) * feat.shape[0]
    bwd = jnp.where(row > 0, h[:, HY_DIM:] * decay, 0.0)
    hs_ref[...] = (fwd + bwd).astype(BF16)
    hd_ref[...] = (bwd - fwd).astype(BF16)


def _hy_filter(feat, w1, b1, fr, w2, b2, w3, b3, dl, tl):
    seq = feat.shape[0]
    out = jax.ShapeDtypeStruct((seq, HY_DIM), BF16)
    return pl.pallas_call(
        _hy_filter_kernel,
        grid=(seq // tl,),
        in_specs=[pl.BlockSpec((tl, LANES), lambda i: (i, 0))] + [_resident(a.shape) for a in (w1, b1, fr, w2, b2, w3, b3, dl)],
        out_specs=[pl.BlockSpec((tl, HY_DIM), lambda i: (i, 0))] * 2,
        out_shape=[out, out],
        compiler_params=_params("parallel"),
        name="hy_filter",
    )(feat, w1, b1, fr, w2, b2, w3, b3, dl)


def _hy_spec_kernel(ce_ref, co_ref, se_ref, so_ref, hs_ref, hd_ref, kr_ref, ki_ref):
    norm = 0.5 / ce_ref.shape[1]
    ae, ao = _mm(ce_ref[...], hs_ref[0]), _mm(co_ref[...], hs_ref[1])
    be, bo = _mm(se_ref[...], hd_ref[0]), _mm(so_ref[...], hd_ref[1])
    kr_ref[0] = (ae + ao) * norm
    kr_ref[1] = (ae - ao) * norm
    ki_ref[0] = (be + bo) * norm
    ki_ref[1] = (bo - be) * norm


def _hy_spec(tables, hs, hd):
    half = hs.shape[1]
    blk = pl.BlockSpec((2, half, HALF_CH), lambda c: (0, 0, c))
    out = jax.ShapeDtypeStruct((2, half, HY_DIM), F32)
    return pl.pallas_call(
        _hy_spec_kernel,
        grid=(HY_DIM // HALF_CH,),
        in_specs=[_resident(t.shape) for t in tables[:4]] + [blk, blk],
        out_specs=[blk, blk],
        out_shape=[out, out],
        compiler_params=_params("parallel"),
        name="hy_spec",
    )(*tables[:4], hs, hd)


def _hy_pre_kernel(u_ref, cw_ref, cb_ref, z_ref, x0_ref, t_s):
    seq = u_ref.shape[1]
    half = seq // 2
    row = lax.broadcasted_iota(jnp.int32, (seq, 1), 0)

    def conv(col):
        cols = slice(col, col + LANES)
        x = u_ref[0, :, cols].astype(F32)
        w = cw_ref[:, cols]
        return _shift_down(x, 1, row) * w[0:1] + x * w[1:2] + _shift_up(x, 1, row) * w[2:3] + cb_ref[:, cols]

    def store_split(dst_ref, cols, val):
        t_s[...] = val
        dst_ref[0, 0, :, cols] = t_s[pl.ds(0, half, stride=2), :].astype(BF16)
        dst_ref[0, 1, :, cols] = t_s[pl.ds(1, half, stride=2), :].astype(BF16)

    for c in range(HY_DIM // LANES):
        cols = slice(c * LANES, (c + 1) * LANES)
        store_split(x0_ref, cols, conv(c * LANES))
        store_split(z_ref, cols, conv(2 * HY_DIM + c * LANES) * conv(HY_DIM + c * LANES))


def _hy_pre(proj3, cw, cb):
    b, seq, _ = proj3.shape
    out = jax.ShapeDtypeStruct((b, 2, seq // 2, HY_DIM), BF16)
    return pl.pallas_call(
        _hy_pre_kernel,
        grid=(b,),
        in_specs=[pl.BlockSpec((1, seq, 3 * HY_DIM), lambda i: (i, 0, 0)), _resident(cw.shape), _resident(cb.shape)],
        out_specs=[pl.BlockSpec((1, 2, seq // 2, HY_DIM), lambda i: (i, 0, 0, 0))] * 2,
        out_shape=[out, out],
        scratch_shapes=[pltpu.VMEM((seq, LANES), F32)],
        compiler_params=_params("parallel"),
        name="hy_pre",
    )(proj3, cw, cb)


def _hy_conv_kernel(z_ref, x0_ref, ce_ref, co_ref, se_ref, so_ref, cet_ref, cot_ref, set_ref, sot_ref,
                    kr_ref, ki_ref, b_ref, g_ref, o_ref, y_s):
    c = pl.program_id(1)
    half = z_ref.shape[2]
    ze, zo = z_ref[0, 0], z_ref[0, 1]
    ae, ao = _mm(ce_ref[...], ze), _mm(co_ref[...], zo)
    be, bo = _mm(se_ref[...], ze), _mm(so_ref[...], zo)
    u_lo, u_hi, v_lo, v_hi = ae + ao, ae - ao, be + bo, bo - be
    kr_lo, kr_hi, ki_lo, ki_hi = kr_ref[0], kr_ref[1], ki_ref[0], ki_ref[1]
    yr_lo, w_lo = u_lo * kr_lo + v_lo * ki_lo, v_lo * kr_lo - u_lo * ki_lo
    yr_hi, w_hi = u_hi * kr_hi + v_hi * ki_hi, v_hi * kr_hi - u_hi * ki_hi
    ye = _mm(cet_ref[...], (yr_lo + yr_hi).astype(BF16)) + _mm(set_ref[...], (w_lo - w_hi).astype(BF16))
    yo = _mm(cot_ref[...], (yr_lo - yr_hi).astype(BF16)) + _mm(sot_ref[...], (w_lo + w_hi).astype(BF16))
    bias = b_ref[...]
    ye = (ye + ze.astype(F32) * bias) * x0_ref[0, 0].astype(F32)
    yo = (yo + zo.astype(F32) * bias) * x0_ref[0, 1].astype(F32)
    per = HALF_CH // LANES
    for k in range(per):
        y_s[c * per + k, pl.ds(0, half, stride=2), :] = ye[:, k * LANES:(k + 1) * LANES]
        y_s[c * per + k, pl.ds(1, half, stride=2), :] = yo[:, k * LANES:(k + 1) * LANES]

    @pl.when(c == pl.num_programs(1) - 1)
    def _():
        slabs = [y_s[k] for k in range(HY_DIM // LANES)]
        ms = sum(jnp.sum(y * y, axis=-1, keepdims=True) for y in slabs) * (1.0 / HY_DIM)
        rs = lax.rsqrt(ms + NORM_EPS)
        for k, y in enumerate(slabs):
            cols = slice(k * LANES, (k + 1) * LANES)
            o_ref[0, :, cols] = (y * rs * g_ref[:, cols]).astype(BF16)


def _hy_conv(tables, z, x0, kr, ki, bias, g):
    b, _, half, _ = z.shape
    data = pl.BlockSpec((1, 2, half, HALF_CH), lambda i, c: (i, 0, 0, c))
    spec = pl.BlockSpec((2, half, HALF_CH), lambda i, c: (0, 0, c))
    return pl.pallas_call(
        _hy_conv_kernel,
        grid=(b, HY_DIM // HALF_CH),
        in_specs=[data, data] + [_resident(t.shape) for t in tables] + [spec, spec,
                  pl.BlockSpec((1, HALF_CH), lambda i, c: (0, c)), _resident((1, HY_DIM))],
        out_specs=pl.BlockSpec((1, 2 * half, HY_DIM), lambda i, c: (i, 0, 0)),
        out_shape=jax.ShapeDtypeStruct((b, 2 * half, HY_DIM), BF16),
        scratch_shapes=[pltpu.VMEM((HY_DIM // LANES, 2 * half, LANES), F32)],
        compiler_params=_params("parallel", "arbitrary"),
        name="hy_conv",
    )(z, x0, *tables, kr, ki, bias, g)


def _out_proj_kernel(x_ref, a_ref, b_ref, c_ref, w_ref, o_ref):
    acc = jnp.dot(a_ref[...], w_ref[0:POOL_DIM, :], preferred_element_type=F32)
    acc = acc + jnp.dot(b_ref[...], w_ref[POOL_DIM:POOL_DIM + MLA_DIM, :], preferred_element_type=F32)
    acc = acc + jnp.dot(c_ref[...], w_ref[POOL_DIM + MLA_DIM:, :], preferred_element_type=F32)
    o_ref[...] = x_ref[...] + acc


def _out_proj(x2, ma, mb, mc, w_all, layer, tm):
    m, d = x2.shape
    row = lambda width: pl.BlockSpec((tm, width), lambda i: (i, 0))
    w_spec = pl.BlockSpec((None,) + w_all.shape[1:], lambda i: (layer, 0, 0), pipeline_mode=pl.Buffered(1))
    return pl.pallas_call(
        _out_proj_kernel,
        grid=(m // tm,),
        in_specs=[row(d), row(POOL_DIM), row(MLA_DIM), row(HY_DIM), w_spec],
        out_specs=row(d),
        out_shape=jax.ShapeDtypeStruct((m, d), F32),
        compiler_params=_params("parallel"),
        name="out_proj",
    )(x2, ma, mb, mc, w_all)


HALO = 16


def _ffn_kernel(x_ref, xp_ref, xn_ref, g_ref, wg_ref, wv_ref, cw_ref, cb_ref, wd_ref, o_ref, h_s, up_s, *, blocks_per_seq):
    tm = x_ref.shape[0]
    f = pl.program_id(1)

    @pl.when(f == 0)
    def _():
        x = x_ref[...]
        g = g_ref[...]
        o_ref[...] = x
        h_s[HALO:HALO + tm, :] = _rms(x, g).astype(BF16)
        pos = pl.program_id(0) % blocks_per_seq
        pad = jnp.zeros((HALO - 8, x.shape[1]), F32)
        hp = jnp.where(pos == 0, 0.0, _rms(xp_ref[...], g))
        hn = jnp.where(pos == blocks_per_seq - 1, 0.0, _rms(xn_ref[...], g))
        h_s[0:HALO, :] = jnp.concatenate([pad, hp], axis=0).astype(BF16)
        h_s[HALO + tm:, :] = jnp.concatenate([hn, pad], axis=0).astype(BF16)

    up_s[...] = jnp.dot(h_s[...], wg_ref[...], preferred_element_type=F32)
    val = jnp.dot(h_s[HALO:HALO + tm, :], wv_ref[...], preferred_element_type=F32)
    cw = cw_ref[...]
    gate = (up_s[HALO - 1:HALO - 1 + tm, :] * cw[0:1] + up_s[HALO:HALO + tm, :] * cw[1:2]
            + up_s[HALO + 1:HALO + 1 + tm, :] * cw[2:3] + cb_ref[...])
    act = (gate * jax.nn.sigmoid(gate) * val).astype(BF16)
    o_ref[...] += jnp.dot(act, wd_ref[...], preferred_element_type=F32)


def _ffn(x2, seq, g, w_up, cw, cb, w_down, layer, tm, tf):
    m, d = x2.shape
    nf = D_FF // tf
    r8 = tm // 8
    last8 = m // 8 - 1
    kern = functools.partial(_ffn_kernel, blocks_per_seq=seq // tm)
    return pl.pallas_call(
        kern,
        grid=(m // tm, nf),
        in_specs=[pl.BlockSpec((tm, d), lambda i, f: (i, 0)),
                  pl.BlockSpec((8, d), lambda i, f: (jnp.maximum(i * r8 - 1, 0), 0)),
                  pl.BlockSpec((8, d), lambda i, f: (jnp.minimum((i + 1) * r8, last8), 0)),
                  _resident((1, d)),
                  pl.BlockSpec((None, d, tf), lambda i, f: (layer, 0, f)),
                  pl.BlockSpec((None, d, tf), lambda i, f: (layer, 0, f + nf)),
                  pl.BlockSpec((3, tf), lambda i, f: (0, f)),
                  pl.BlockSpec((1, tf), lambda i, f: (0, f)),
                  pl.BlockSpec((None, tf, d), lambda i, f: (layer, f, 0))],
        out_specs=pl.BlockSpec((tm, d), lambda i, f: (i, 0)),
        out_shape=jax.ShapeDtypeStruct((m, d), F32),
        scratch_shapes=[pltpu.VMEM((tm + 2 * HALO, d), BF16), pltpu.VMEM((tm + 2 * HALO, tf), F32)],
        compiler_params=_params("parallel", "arbitrary"),
        name="ffn",
    )(x2, x2, x2, g, w_up, w_up, cw, cb, w_down)


def _rope_tables(seq):
    freqs = ROPE_THETA ** (-jnp.arange(0, QK_ROPE, 2, dtype=F32) / QK_ROPE)
    ang = jnp.arange(seq, dtype=F32)[:, None] * freqs[None, :]
    cos, sin, zero = jnp.cos(ang), jnp.sin(ang), jnp.zeros((seq, ROPE_HALF), F32)
    return (jnp.concatenate([cos, zero, cos, zero], axis=1), jnp.concatenate([-sin, zero, sin, zero], axis=1))


def _dft_tables(seq):
    half = seq // 2
    f = jnp.arange(half, dtype=jnp.int32)[:, None]
    j = jnp.arange(half, dtype=jnp.int32)[None, :]
    out = []
    for tok in (2 * j, 2 * j + 1):
        ang = (((2 * f + 1) * tok) % (4 * seq)).astype(F32) * (math.pi / (2 * seq))
        out.append((jnp.cos(ang).astype(BF16), jnp.sin(ang).astype(BF16)))
    (ce, se), (co, so) = out
    return ce, co, se, so, ce.T, co.T, se.T, so.T


def _filter_features(seq):
    t = jnp.linspace(0.0, 1.0, seq, dtype=F32)[:, None]
    w = 2.0 * math.pi * jnp.arange(seq, dtype=F32)[:, None] / seq
    bands = jnp.linspace(1e-4, N_BANDS - 1, N_BANDS, dtype=F32)[None, :]
    z = jnp.concatenate([t, jnp.cos(bands * w), -jnp.sin(bands * w)], axis=-1)
    z = jnp.concatenate([z[0::2], z[1::2]], axis=0)
    return jnp.pad(z, ((0, 0), (0, LANES - N_POS_FEAT)))


def _decay_rates():
    max_decay = math.log(HY_DECAY_TARGET) / HY_FAST_DECAY
    min_decay = math.log(HY_DECAY_TARGET) / HY_SLOW_DECAY
    return jnp.abs(jnp.linspace(min_decay, max_decay, HY_DIM, dtype=F32))[None, :]


def _spread_rope(a):
    zero = jnp.zeros(a.shape[:-1] + (ROPE_HALF,), a.dtype)
    return jnp.concatenate([a[..., :ROPE_HALF], zero, a[..., ROPE_HALF:], zero], axis=-1)


def _pad_to(a, shape):
    return jnp.pad(a, [(0, t - s) for s, t in zip(a.shape, shape)])


def _layer_weights(p, l):
    row = lambda a: a[l][None, :].astype(F32)
    w_in = p["w_in"][l]
    o1, o2, o3 = POOL_DIM + Q_LORA, POOL_DIM + Q_LORA + KV_LORA, POOL_DIM + Q_LORA + KV_LORA + QK_ROPE
    w_in_p = jnp.concatenate([w_in[:, o3:], w_in[:, :o2], _spread_rope(w_in[:, o2:o3])], axis=1).astype(BF16)
    wq = p["mla_w_uq"][l].reshape(Q_LORA, MLA_HEADS, QK_HEAD)
    wq_p = jnp.concatenate([wq[..., :QK_NOPE], _spread_rope(wq[..., QK_NOPE:])], axis=-1)
    wkv = p["mla_w_ukv"][l].reshape(KV_LORA, MLA_HEADS, QK_NOPE + V_HEAD)
    qhg, khg = p["mla_q_head_norm_g"][l], p["mla_k_head_norm_g"][l]
    hid = (LANES, LANES)
    return dict(
        attn_g=row(p["attn_norm_g"]), w_in=w_in_p,
        pool_w=p["pool_w"][l].astype(BF16), pool_scale=row(p["pool_scale"]),
        qn=row(p["mla_q_norm_g"]), wq=wq_p.reshape(Q_LORA, MLA_HEADS * HEAD_PAD).astype(BF16),
        kvn=row(p["mla_kv_norm_g"]),
        wk=wkv[..., :QK_NOPE].reshape(KV_LORA, MLA_HEADS * QK_NOPE).astype(BF16),
        wv=wkv[..., QK_NOPE:].reshape(KV_LORA, MLA_DIM).T.astype(BF16),
        qhg=jnp.concatenate([qhg[:QK_NOPE], _spread_rope(qhg[QK_NOPE:])])[None, :],
        khn=khg[None, :QK_NOPE], khr=_spread_rope(khg[QK_NOPE:])[None, :],
        hy_cw=p["hy_conv_w"][l], hy_cb=row(p["hy_conv_b"]), hy_bias=row(p["hy_bias"]),
        f_w1=_pad_to(p["hy_filt_w1"][l], hid), f_b1=_pad_to(row(p["hy_filt_b1"]), (1, LANES)),
        f_fr=_pad_to(row(p["hy_filt_freq"]), (1, LANES)),
        f_w2=_pad_to(p["hy_filt_w2"][l], hid), f_b2=_pad_to(row(p["hy_filt_b2"]), (1, LANES)),
        f_w3=_pad_to(p["hy_filt_w3"][l], (LANES, 2 * HY_DIM)), f_b3=row(p["hy_filt_b3"]),
        g_pool=p["grp_norm_g"][l][None, :POOL_DIM], g_mla=p["grp_norm_g"][l][None, POOL_DIM:POOL_DIM + MLA_DIM],
        g_hy=p["grp_norm_g"][l][None, POOL_DIM + MLA_DIM:],
        ffn_g=row(p["ffn_norm_g"]), ffn_cw=p["ffn_conv_w"][l], ffn_cb=row(p["ffn_conv_b"]),
    )


def _tile(n, want):
    return min(n, want)


def _layer(x2, b, seq, layer, w, stacks, kr, ki, rope, tables):
    cos_t, sin_t = rope
    m = b * seq
    proj2, q2, k2, vt = _in_proj(x2, seq, w["attn_g"], w["w_in"], cos_t, sin_t, w["qn"], w["wq"], w["kvn"],
                                 w["wk"], w["wv"], w["qhg"], w["khn"], w["khr"], _tile(seq, 512))
    proj3 = proj2.reshape(b, seq, COL_CQ)
    m_pool = _pool(proj3, w["pool_w"], w["pool_scale"], w["g_pool"])
    hp = MLA_HEADS * HEAD_PAD
    m_mla = _attention(q2.reshape(b, seq, hp), k2.reshape(b, seq, hp), vt,
                       w["g_mla"], _tile(seq, 512))
    z, x0 = _hy_pre(proj3, w["hy_cw"], w["hy_cb"])
    m_hy = _hy_conv(tables, z, x0, kr, ki, w["hy_bias"], w["g_hy"])
    x2 = _out_proj(x2, m_pool.reshape(m, POOL_DIM), m_mla.reshape(m, MLA_DIM), m_hy.reshape(m, HY_DIM),
                   stacks["w_out"], layer, _tile(seq, 512))
    return _ffn(x2, seq, w["ffn_g"], stacks["w_up"], w["ffn_cw"], w["ffn_cb"], stacks["w_down"], layer,
                _tile(seq, 1024), 512)


def kernel(x_prompt, x_sample, attn_norm_g, w_in, pool_w, pool_scale, mla_q_norm_g, mla_w_uq, mla_kv_norm_g, mla_w_ukv, mla_q_head_norm_g, mla_k_head_norm_g, hy_conv_w, hy_conv_b, hy_filt_w1, hy_filt_b1, hy_filt_freq, hy_filt_w2, hy_filt_b2, hy_filt_w3, hy_filt_b3, hy_bias, grp_norm_g, w_out, ffn_norm_g, ffn_w_up, ffn_conv_w, ffn_conv_b, ffn_w_down):
    p = dict(attn_norm_g=attn_norm_g, w_in=w_in, pool_w=pool_w, pool_scale=pool_scale, mla_q_norm_g=mla_q_norm_g,
             mla_w_uq=mla_w_uq, mla_kv_norm_g=mla_kv_norm_g, mla_w_ukv=mla_w_ukv,
             mla_q_head_norm_g=mla_q_head_norm_g, mla_k_head_norm_g=mla_k_head_norm_g, hy_conv_w=hy_conv_w,
             hy_conv_b=hy_conv_b, hy_filt_w1=hy_filt_w1, hy_filt_b1=hy_filt_b1, hy_filt_freq=hy_filt_freq,
             hy_filt_w2=hy_filt_w2, hy_filt_b2=hy_filt_b2, hy_filt_w3=hy_filt_w3, hy_filt_b3=hy_filt_b3,
             hy_bias=hy_bias, grp_norm_g=grp_norm_g, w_out=w_out, ffn_norm_g=ffn_norm_g, ffn_w_up=ffn_w_up,
             ffn_conv_w=ffn_conv_w, ffn_conv_b=ffn_conv_b, ffn_w_down=ffn_w_down)
    depth = attn_norm_g.shape[0]
    seq = x_prompt.shape[1]
    assert x_sample.shape[1] == seq and x_prompt.shape[2] == D_MODEL
    rope = _rope_tables(seq)
    tables = _dft_tables(seq)
    feat = _filter_features(seq)
    rates = _decay_rates()
    stacks = dict(w_out=w_out.astype(BF16), w_up=ffn_w_up.astype(BF16), w_down=ffn_w_down.astype(BF16))
    xs = [x.reshape(-1, D_MODEL) for x in (x_prompt, x_sample)]
    for l in range(depth):
        w = _layer_weights(p, l)
        hs, hd = _hy_filter(feat, w["f_w1"], w["f_b1"], w["f_fr"], w["f_w2"], w["f_b2"], w["f_w3"], w["f_b3"],
                            rates, _tile(seq, 512))
        half = (2, seq // 2, HY_DIM)
        kr, ki = _hy_spec(tables, hs.reshape(half), hd.reshape(half))
        xs = [_layer(x2, x.shape[0], seq, l, w, stacks, kr, ki, rope, tables)
              for x2, x in zip(xs, (x_prompt, x_sample))]
    return (xs[0].reshape(x_prompt.shape), xs[1].reshape(x_sample.shape))
```

```python
import functools
import math

import jax
import jax.numpy as jnp
from jax import lax
from jax.experimental import pallas as pl
from jax.experimental.pallas import tpu as pltpu

F32 = jnp.float32
BF16 = jnp.bfloat16

D_MODEL = 2048
DEPTH = 4
POOL_DIM = 512
POOL_WINDOWS = (2, 4, 8, 16)
POOL_GROUP = 128
MLA_HEADS = 8
QK_NOPE = 128
QK_ROPE = 64
QK_HEAD = QK_NOPE + QK_ROPE
V_HEAD = 128
MLA_DIM = MLA_HEADS * V_HEAD
Q_LORA = 512
KV_LORA = 256
ROPE_THETA = 10000.0
HY_DIM = 512
N_BANDS = 8
N_POS_FEAT = 1 + 2 * N_BANDS
FILT_HIDDEN = 64
HY_FAST_DECAY = 0.3
HY_SLOW_DECAY = 1.5
HY_DECAY_TARGET = 1e-2
D_FF = 5632
NORM_EPS = 1e-6

LANES = 128
HEAD_PAD = 2 * LANES
ROPE_HALF = QK_ROPE // 2
VT_ROWS = V_HEAD + 16
HALF_CH = 2 * LANES
COL_POOL = 3 * HY_DIM
COL_CQ = COL_POOL + POOL_DIM
HALO = 16
VMEM_LIMIT = 56 * 1024 * 1024


def _params(*sem):
    return pltpu.CompilerParams(dimension_semantics=sem, vmem_limit_bytes=VMEM_LIMIT)


def _resident(shape):
    nd = len(shape)
    return pl.BlockSpec(shape, lambda *_: (0,) * nd, pipeline_mode=pl.Buffered(1))


def _mm(a, b):
    return jnp.dot(a, b, preferred_element_type=F32)


def _rms(x, g, n=None):
    n = x.shape[-1] if n is None else n
    ms = jnp.sum(x * x, axis=-1, keepdims=True) * (1.0 / n)
    return x * lax.rsqrt(ms + NORM_EPS) * g


def _rope(r, cos_t, sin_t):
    return r * cos_t + pltpu.roll(r, LANES // 2, 1) * sin_t


def _in_proj_kernel(x_ref, xp_ref, xn_ref, g_ref, w_ref, cos_ref, sin_ref, qn_ref, wq_ref, kvn_ref, wk_ref, wv_ref,
                    qhg_ref, khn_ref, khr_ref, pw_ref, ps_ref, pg_ref, cw_ref, cb_ref,
                    q_ref, k_ref, v_ref, mp_ref, z_ref, x0_ref, h_s, t_s, y_s, *, blocks_per_seq, seq):
    tm = x_ref.shape[0]
    g = g_ref[...]
    pos = pl.program_id(0) % blocks_per_seq
    h = _rms(x_ref[...], g).astype(BF16)
    h_s[HALO:HALO + tm, :] = h
    pad = jnp.zeros((HALO - 8, x_ref.shape[1]), F32)
    hp = jnp.where(pos == 0, 0.0, _rms(xp_ref[...], g))
    hn = jnp.where(pos == blocks_per_seq - 1, 0.0, _rms(xn_ref[...], g))
    h_s[0:HALO, :] = jnp.concatenate([pad, hp], axis=0).astype(BF16)
    h_s[HALO + tm:, :] = jnp.concatenate([hn, pad], axis=0).astype(BF16)
    lat = _mm(h, w_ref[:, COL_CQ:])
    cq = _rms(lat[:, :Q_LORA], qn_ref[...]).astype(BF16)
    ckv = _rms(lat[:, Q_LORA:Q_LORA + KV_LORA], kvn_ref[...]).astype(BF16)
    kr = lat[:, Q_LORA + KV_LORA:]
    qf = _mm(cq, wq_ref[...])
    kn = _mm(ckv, wk_ref[...])
    vt = lax.dot_general(wv_ref[...], ckv, (((1,), (1,)), ((), ())), preferred_element_type=F32)
    pool_u = _mm(h_s[...], w_ref[:, COL_POOL:COL_CQ])
    hy_u = [[_mm(h_s[...], w_ref[:, sec * HY_DIM + cc * HALF_CH:sec * HY_DIM + (cc + 1) * HALF_CH])
             for sec in range(3)] for cc in range(HY_DIM // HALF_CH)]

    cos_t = cos_ref[...]
    sin_t = sin_ref[...]
    scale = QK_HEAD ** -0.5 * math.log2(math.e)
    qg = qhg_ref[...] * scale
    qg_n, qg_r = qg[:, :LANES], qg[:, LANES:]
    rope_a = qg_r * cos_t
    rope_b = pltpu.roll(qg_r, LANES // 2, 1) * sin_t
    for hd in range(MLA_HEADS):
        qh = qf[:, hd * HEAD_PAD:(hd + 1) * HEAD_PAD]
        rs = lax.rsqrt(jnp.sum(qh * qh, axis=-1, keepdims=True) * (1.0 / QK_HEAD) + NORM_EPS)
        qr = qh[:, LANES:]
        q_ref[:, hd * HEAD_PAD:hd * HEAD_PAD + LANES] = (qh[:, :LANES] * rs * qg_n).astype(BF16)
        q_ref[:, hd * HEAD_PAD + LANES:(hd + 1) * HEAD_PAD] = (
            (qr * rope_a + pltpu.roll(qr, LANES // 2, 1) * rope_b) * rs).astype(BF16)
    ones_rows = (lax.broadcasted_iota(jnp.int32, (VT_ROWS - V_HEAD, vt.shape[1]), 0) == 0).astype(BF16)
    for hd in range(MLA_HEADS):
        v_ref[0, hd * VT_ROWS:hd * VT_ROWS + V_HEAD, :] = vt[hd * V_HEAD:(hd + 1) * V_HEAD, :].astype(BF16)
        v_ref[0, hd * VT_ROWS + V_HEAD:(hd + 1) * VT_ROWS, :] = ones_rows
    ss_r = jnp.sum(kr * kr, axis=-1, keepdims=True)
    kr_rot = _rope(kr * khr_ref[...], cos_t, sin_t)
    for hd in range(MLA_HEADS):
        knh = kn[:, hd * LANES:(hd + 1) * LANES]
        ms = (jnp.sum(knh * knh, axis=-1, keepdims=True) + ss_r) * (1.0 / QK_HEAD)
        rs = lax.rsqrt(ms + NORM_EPS)
        k_ref[:, hd * HEAD_PAD:hd * HEAD_PAD + LANES] = (knh * rs * khn_ref[...]).astype(BF16)
        k_ref[:, hd * HEAD_PAD + LANES:(hd + 1) * HEAD_PAD] = (kr_rot * rs).astype(BF16)

    ext = tm + 2 * HALO
    main = slice(HALO, HALO + tm)
    down = lambda a, n: pltpu.roll(a, n, 0)
    up = lambda a, n: pltpu.roll(a, ext - n, 0)

    tok = lax.broadcasted_iota(jnp.int32, (tm, 1), 0) + pos * tm
    for gi, win in enumerate(POOL_WINDOWS):
        half = win // 2
        cols = slice(gi * POOL_GROUP, (gi + 1) * POOL_GROUP)
        xg = pool_u[:, cols]
        ahead = xg
        behind = down(xg, 1)
        span = 1
        while span < half:
            ahead = ahead + up(ahead, span)
            behind = behind + down(behind, span)
            span *= 2
        tot = (ahead + behind)[main]
        cnt = (jnp.minimum(tok + half, seq) - jnp.maximum(tok - half, 0)).astype(F32)
        pooled = (tot / cnt - xg[main]).astype(BF16)
        y_s[:, cols] = _mm(pooled, pw_ref[gi]) * ps_ref[:, cols]
    mp_ref[...] = _rms(y_s[...], pg_ref[...]).astype(BF16)

    def conv3(xc, col):
        cols = slice(col, col + LANES)
        w = cw_ref[:, cols]
        return (down(xc, 1) * w[0:1] + xc * w[1:2] + up(xc, 1) * w[2:3])[main] + cb_ref[:, cols]

    def store_split(dst_ref, slab, cols, val):
        t_s[slab] = val
        dst_ref[0, 0, :, cols] = t_s[slab, pl.ds(0, tm // 2, stride=2), :].astype(BF16)
        dst_ref[0, 1, :, cols] = t_s[slab, pl.ds(1, tm // 2, stride=2), :].astype(BF16)

    for c in range(HY_DIM // LANES):
        cols = slice(c * LANES, (c + 1) * LANES)
        x0u, x1u, vu = (u[:, (c % 2) * LANES:(c % 2 + 1) * LANES] for u in hy_u[c // 2])
        store_split(x0_ref, 2 * c, cols, conv3(x0u, c * LANES))
        store_split(z_ref, 2 * c + 1, cols, conv3(vu, 2 * HY_DIM + c * LANES) * conv3(x1u, HY_DIM + c * LANES))


def _in_proj(x2, seq, g, w, cos_t, sin_t, qn, wq, kvn, wk, wv, qhg, khn, khr, pool_w, pool_scale, g_pool,
             hy_cw, hy_cb, tm):
    m, d = x2.shape
    nseq = seq // tm
    hp = MLA_HEADS * HEAD_PAD
    r8 = tm // 8
    last8 = m // 8 - 1
    row = lambda width: pl.BlockSpec((tm, width), lambda i: (i, 0))
    tab = pl.BlockSpec((tm, LANES), lambda i: (i % nseq, 0))
    split = pl.BlockSpec((1, 2, tm // 2, HY_DIM), lambda i: (i // nseq, 0, i % nseq, 0))
    split_shape = jax.ShapeDtypeStruct((m // seq, 2, seq // 2, HY_DIM), BF16)
    consts = (qn, wq, kvn, wk, wv, qhg, khn, khr, pool_w, pool_scale, g_pool, hy_cw, hy_cb)
    return pl.pallas_call(
        functools.partial(_in_proj_kernel, blocks_per_seq=nseq, seq=seq),
        grid=(m // tm,),
        in_specs=[row(d),
                  pl.BlockSpec((8, d), lambda i: (jnp.maximum(i * r8 - 1, 0), 0)),
                  pl.BlockSpec((8, d), lambda i: (jnp.minimum((i + 1) * r8, last8), 0)),
                  _resident((1, d)), _resident(w.shape), tab, tab] + [_resident(a.shape) for a in consts],
        out_specs=[row(hp), row(hp),
                   pl.BlockSpec((1, MLA_HEADS * VT_ROWS, tm), lambda i: (i // nseq, 0, i % nseq)),
                   row(POOL_DIM), split, split],
        out_shape=[jax.ShapeDtypeStruct((m, hp), BF16), jax.ShapeDtypeStruct((m, hp), BF16),
                   jax.ShapeDtypeStruct((m // seq, MLA_HEADS * VT_ROWS, seq), BF16),
                   jax.ShapeDtypeStruct((m, POOL_DIM), BF16), split_shape, split_shape],
        scratch_shapes=[pltpu.VMEM((tm + 2 * HALO, d), BF16), pltpu.VMEM((2 * HY_DIM // LANES, tm, LANES), F32),
                        pltpu.VMEM((tm, POOL_DIM), F32)],
        compiler_params=_params("parallel"),
        name="in_proj",
    )(x2, x2, x2, g, w, cos_t, sin_t, *consts)


def _attn_kernel(q_ref, k_ref, v_ref, g_ref, o_ref, y_s):
    def scores(h):
        q = q_ref[0, :, h * HEAD_PAD:(h + 1) * HEAD_PAD]
        k = k_ref[0, :, h * HEAD_PAD:(h + 1) * HEAD_PAD]
        return lax.dot_general(k, q, (((1,), (1,)), ((), ())), preferred_element_type=F32)

    st_next = scores(0)
    for h in range(MLA_HEADS):
        st = st_next
        if h + 1 < MLA_HEADS:
            st_next = scores(h + 1)
        pt = jnp.exp2(st - jnp.max(st, axis=0, keepdims=True)).astype(BF16)
        ot = jnp.dot(v_ref[0, h * VT_ROWS:(h + 1) * VT_ROWS, :], pt, preferred_element_type=F32)
        yt = ot[:V_HEAD, :] / ot[V_HEAD:V_HEAD + 1, :]
        y_s[:, h * V_HEAD:(h + 1) * V_HEAD] = yt.T
    o_ref[0] = _rms(y_s[...], g_ref[...]).astype(BF16)


def _attention(q3, k3, v3, g, tq):
    b, seq, hp = q3.shape
    return pl.pallas_call(
        _attn_kernel,
        grid=(b, seq // tq),
        in_specs=[pl.BlockSpec((1, tq, hp), lambda i, j: (i, j, 0)),
                  pl.BlockSpec((1, seq, hp), lambda i, j: (i, 0, 0)),
                  pl.BlockSpec((1, MLA_HEADS * VT_ROWS, seq), lambda i, j: (i, 0, 0)),
                  _resident((1, MLA_DIM))],
        out_specs=pl.BlockSpec((1, tq, MLA_DIM), lambda i, j: (i, j, 0)),
        out_shape=jax.ShapeDtypeStruct((b, seq, MLA_DIM), BF16),
        scratch_shapes=[pltpu.VMEM((tq, MLA_DIM), F32)],
        compiler_params=_params("parallel", "arbitrary"),
        name="attention",
    )(q3, k3, v3, g)


def _hy_filter_kernel(feat_ref, w1_ref, b1_ref, fr_ref, w2_ref, b2_ref, w3_ref, b3_ref, dl_ref, hs_ref, hd_ref):
    hi = lax.Precision.HIGHEST
    feat = feat_ref[...]
    fr = fr_ref[...]
    h = jnp.sin(fr * (jnp.dot(feat, w1_ref[...], precision=hi, preferred_element_type=F32) + b1_ref[...]))
    h = jnp.sin(fr * (jnp.dot(h, w2_ref[...], precision=hi, preferred_element_type=F32) + b2_ref[...]))
    h = jnp.dot(h, w3_ref[...], precision=hi, preferred_element_type=F32) + b3_ref[...]
    decay = jnp.exp(-feat[:, 0:1] * dl_ref[...])
    fwd = h[:, :HY_DIM] * decay
    row = lax.broadcasted_iota(jnp.int32, (feat.shape[0], 1), 0) + pl.program_id(0) * feat.shape[0]
    bwd = jnp.where(row > 0, h[:, HY_DIM:] * decay, 0.0)
    hs_ref[...] = (fwd + bwd).astype(BF16)
    hd_ref[...] = (bwd - fwd).astype(BF16)


def _hy_filter(feat, w1, b1, fr, w2, b2, w3, b3, dl, tl):
    seq = feat.shape[0]
    out = jax.ShapeDtypeStruct((seq, HY_DIM), BF16)
    return pl.pallas_call(
        _hy_filter_kernel,
        grid=(seq // tl,),
        in_specs=[pl.BlockSpec((tl, LANES), lambda i: (i, 0))] + [_resident(a.shape) for a in (w1, b1, fr, w2, b2, w3, b3, dl)],
        out_specs=[pl.BlockSpec((tl, HY_DIM), lambda i: (i, 0))] * 2,
        out_shape=[out, out],
        compiler_params=_params("parallel"),
        name="hy_filter",
    )(feat, w1, b1, fr, w2, b2, w3, b3, dl)


def _hy_spec_kernel(ce_ref, co_ref, se_ref, so_ref, hs_ref, hd_ref, kr_ref, ki_ref):
    norm = 0.5 / ce_ref.shape[1]
    ae, ao = _mm(ce_ref[...], hs_ref[0]), _mm(co_ref[...], hs_ref[1])
    be, bo = _mm(se_ref[...], hd_ref[0]), _mm(so_ref[...], hd_ref[1])
    kr_ref[0] = (ae + ao) * norm
    kr_ref[1] = (ae - ao) * norm
    ki_ref[0] = (be + bo) * norm
    ki_ref[1] = (bo - be) * norm


def _hy_spec(tables, hs, hd):
    half = hs.shape[1]
    blk = pl.BlockSpec((2, half, HALF_CH), lambda c: (0, 0, c))
    out = jax.ShapeDtypeStruct((2, half, HY_DIM), F32)
    return pl.pallas_call(
        _hy_spec_kernel,
        grid=(HY_DIM // HALF_CH,),
        in_specs=[_resident(t.shape) for t in tables[:4]] + [blk, blk],
        out_specs=[blk, blk],
        out_shape=[out, out],
        compiler_params=_params("parallel"),
        name="hy_spec",
    )(*tables[:4], hs, hd)


def _hy_conv_kernel(z_ref, x0_ref, ce_ref, co_ref, se_ref, so_ref, cet_ref, cot_ref, set_ref, sot_ref,
                    kr_ref, ki_ref, b_ref, g_ref, o_ref, y_s):
    c = pl.program_id(1)
    half = z_ref.shape[2]
    ze, zo = z_ref[0, 0], z_ref[0, 1]
    ae, ao = _mm(ce_ref[...], ze), _mm(co_ref[...], zo)
    be, bo = _mm(se_ref[...], ze), _mm(so_ref[...], zo)
    u_lo, u_hi, v_lo, v_hi = ae + ao, ae - ao, be + bo, bo - be
    kr_lo, kr_hi, ki_lo, ki_hi = kr_ref[0], kr_ref[1], ki_ref[0], ki_ref[1]
    yr_lo, w_lo = u_lo * kr_lo + v_lo * ki_lo, v_lo * kr_lo - u_lo * ki_lo
    yr_hi, w_hi = u_hi * kr_hi + v_hi * ki_hi, v_hi * kr_hi - u_hi * ki_hi
    ye = _mm(cet_ref[...], (yr_lo + yr_hi).astype(BF16)) + _mm(set_ref[...], (w_lo - w_hi).astype(BF16))
    yo = _mm(cot_ref[...], (yr_lo - yr_hi).astype(BF16)) + _mm(sot_ref[...], (w_lo + w_hi).astype(BF16))
    bias = b_ref[...]
    ye = (ye + ze.astype(F32) * bias) * x0_ref[0, 0].astype(F32)
    yo = (yo + zo.astype(F32) * bias) * x0_ref[0, 1].astype(F32)
    per = HALF_CH // LANES
    for k in range(per):
        y_s[c * per + k, pl.ds(0, half, stride=2), :] = ye[:, k * LANES:(k + 1) * LANES]
        y_s[c * per + k, pl.ds(1, half, stride=2), :] = yo[:, k * LANES:(k + 1) * LANES]

    @pl.when(c == pl.num_programs(1) - 1)
    def _():
        slabs = [y_s[k] for k in range(HY_DIM // LANES)]
        ms = sum(jnp.sum(y * y, axis=-1, keepdims=True) for y in slabs) * (1.0 / HY_DIM)
        rs = lax.rsqrt(ms + NORM_EPS)
        for k, y in enumerate(slabs):
            cols = slice(k * LANES, (k + 1) * LANES)
            o_ref[0, :, cols] = (y * rs * g_ref[:, cols]).astype(BF16)


def _hy_conv(tables, z, x0, kr, ki, bias, g):
    b, _, half, _ = z.shape
    data = pl.BlockSpec((1, 2, half, HALF_CH), lambda i, c: (i, 0, 0, c))
    spec = pl.BlockSpec((2, half, HALF_CH), lambda i, c: (0, 0, c))
    return pl.pallas_call(
        _hy_conv_kernel,
        grid=(b, HY_DIM // HALF_CH),
        in_specs=[data, data] + [_resident(t.shape) for t in tables] + [spec, spec,
                  pl.BlockSpec((1, HALF_CH), lambda i, c: (0, c)), _resident((1, HY_DIM))],
        out_specs=pl.BlockSpec((1, 2 * half, HY_DIM), lambda i, c: (i, 0, 0)),
        out_shape=jax.ShapeDtypeStruct((b, 2 * half, HY_DIM), BF16),
        scratch_shapes=[pltpu.VMEM((HY_DIM // LANES, 2 * half, LANES), F32)],
        compiler_params=_params("parallel", "arbitrary"),
        name="hy_conv",
    )(z, x0, *tables, kr, ki, bias, g)


def _out_proj_kernel(x_ref, a_ref, b_ref, c_ref, w_ref, o_ref):
    acc = jnp.dot(a_ref[...], w_ref[0:POOL_DIM, :], preferred_element_type=F32)
    acc = acc + jnp.dot(b_ref[...], w_ref[POOL_DIM:POOL_DIM + MLA_DIM, :], preferred_element_type=F32)
    acc = acc + jnp.dot(c_ref[...], w_ref[POOL_DIM + MLA_DIM:, :], preferred_element_type=F32)
    o_ref[...] = x_ref[...] + acc


def _out_proj(x2, ma, mb, mc, w_all, layer, tm):
    m, d = x2.shape
    row = lambda width: pl.BlockSpec((tm, width), lambda i: (i, 0))
    w_spec = pl.BlockSpec((None,) + w_all.shape[1:], lambda i: (layer, 0, 0), pipeline_mode=pl.Buffered(1))
    return pl.pallas_call(
        _out_proj_kernel,
        grid=(m // tm,),
        in_specs=[row(d), row(POOL_DIM), row(MLA_DIM), row(HY_DIM), w_spec],
        out_specs=row(d),
        out_shape=jax.ShapeDtypeStruct((m, d), F32),
        compiler_params=_params("parallel"),
        name="out_proj",
    )(x2, ma, mb, mc, w_all)


def _ffn_kernel(x_ref, xp_ref, xn_ref, g_ref, wg_ref, wv_ref, cw_ref, cb_ref, wd_ref, o_ref, h_s, up_s, *, blocks_per_seq):
    tm = x_ref.shape[0]
    f = pl.program_id(1)

    @pl.when(f == 0)
    def _():
        x = x_ref[...]
        g = g_ref[...]
        o_ref[...] = x
        h_s[HALO:HALO + tm, :] = _rms(x, g).astype(BF16)
        pos = pl.program_id(0) % blocks_per_seq
        pad = jnp.zeros((HALO - 8, x.shape[1]), F32)
        hp = jnp.where(pos == 0, 0.0, _rms(xp_ref[...], g))
        hn = jnp.where(pos == blocks_per_seq - 1, 0.0, _rms(xn_ref[...], g))
        h_s[0:HALO, :] = jnp.concatenate([pad, hp], axis=0).astype(BF16)
        h_s[HALO + tm:, :] = jnp.concatenate([hn, pad], axis=0).astype(BF16)

    up_s[...] = jnp.dot(h_s[...], wg_ref[...], preferred_element_type=F32)
    val = jnp.dot(h_s[HALO:HALO + tm, :], wv_ref[...], preferred_element_type=F32)
    cw = cw_ref[...]
    gate = (up_s[HALO - 1:HALO - 1 + tm, :] * cw[0:1] + up_s[HALO:HALO + tm, :] * cw[1:2]
            + up_s[HALO + 1:HALO + 1 + tm, :] * cw[2:3] + cb_ref[...])
    act = (gate * jax.nn.sigmoid(gate) * val).astype(BF16)
    o_ref[...] += jnp.dot(act, wd_ref[...], preferred_element_type=F32)


def _ffn(x2, seq, g, w_up, cw, cb, w_down, layer, tm, tf):
    m, d = x2.shape
    nf = D_FF // tf
    r8 = tm // 8
    last8 = m // 8 - 1
    kern = functools.partial(_ffn_kernel, blocks_per_seq=seq // tm)
    return pl.pallas_call(
        kern,
        grid=(m // tm, nf),
        in_specs=[pl.BlockSpec((tm, d), lambda i, f: (i, 0)),
                  pl.BlockSpec((8, d), lambda i, f: (jnp.maximum(i * r8 - 1, 0), 0)),
                  pl.BlockSpec((8, d), lambda i, f: (jnp.minimum((i + 1) * r8, last8), 0)),
                  _resident((1, d)),
                  pl.BlockSpec((None, d, tf), lambda i, f: (layer, 0, f)),
                  pl.BlockSpec((None, d, tf), lambda i, f: (layer, 0, f + nf)),
                  pl.BlockSpec((3, tf), lambda i, f: (0, f)),
                  pl.BlockSpec((1, tf), lambda i, f: (0, f)),
                  pl.BlockSpec((None, tf, d), lambda i, f: (layer, f, 0))],
        out_specs=pl.BlockSpec((tm, d), lambda i, f: (i, 0)),
        out_shape=jax.ShapeDtypeStruct((m, d), F32),
        scratch_shapes=[pltpu.VMEM((tm + 2 * HALO, d), BF16), pltpu.VMEM((tm + 2 * HALO, tf), F32)],
        compiler_params=_params("parallel", "arbitrary"),
        name="ffn",
    )(x2, x2, x2, g, w_up, w_up, cw, cb, w_down)


def _rope_tables(seq):
    freqs = ROPE_THETA ** (-jnp.arange(0, QK_ROPE, 2, dtype=F32) / QK_ROPE)
    ang = jnp.arange(seq, dtype=F32)[:, None] * freqs[None, :]
    cos, sin, zero = jnp.cos(ang), jnp.sin(ang), jnp.zeros((seq, ROPE_HALF), F32)
    return (jnp.concatenate([cos, zero, cos, zero], axis=1), jnp.concatenate([-sin, zero, sin, zero], axis=1))


def _dft_tables(seq):
    half = seq // 2
    f = jnp.arange(half, dtype=jnp.int32)[:, None]
    j = jnp.arange(half, dtype=jnp.int32)[None, :]
    out = []
    for tok in (2 * j, 2 * j + 1):
        ang = (((2 * f + 1) * tok) % (4 * seq)).astype(F32) * (math.pi / (2 * seq))
        out.append((jnp.cos(ang).astype(BF16), jnp.sin(ang).astype(BF16)))
    (ce, se), (co, so) = out
    return ce, co, se, so, ce.T, co.T, se.T, so.T


def _filter_features(seq):
    t = jnp.linspace(0.0, 1.0, seq, dtype=F32)[:, None]
    w = 2.0 * math.pi * jnp.arange(seq, dtype=F32)[:, None] / seq
    bands = jnp.linspace(1e-4, N_BANDS - 1, N_BANDS, dtype=F32)[None, :]
    z = jnp.concatenate([t, jnp.cos(bands * w), -jnp.sin(bands * w)], axis=-1)
    z = jnp.concatenate([z[0::2], z[1::2]], axis=0)
    return jnp.pad(z, ((0, 0), (0, LANES - N_POS_FEAT)))


def _decay_rates():
    max_decay = math.log(HY_DECAY_TARGET) / HY_FAST_DECAY
    min_decay = math.log(HY_DECAY_TARGET) / HY_SLOW_DECAY
    return jnp.abs(jnp.linspace(min_decay, max_decay, HY_DIM, dtype=F32))[None, :]


def _spread_rope(a):
    zero = jnp.zeros(a.shape[:-1] + (ROPE_HALF,), a.dtype)
    return jnp.concatenate([a[..., :ROPE_HALF], zero, a[..., ROPE_HALF:], zero], axis=-1)


def _pad_to(a, shape):
    return jnp.pad(a, [(0, t - s) for s, t in zip(a.shape, shape)])


def _layer_weights(p, l):
    row = lambda a: a[l][None, :].astype(F32)
    w_in = p["w_in"][l]
    o1, o2, o3 = POOL_DIM + Q_LORA, POOL_DIM + Q_LORA + KV_LORA, POOL_DIM + Q_LORA + KV_LORA + QK_ROPE
    w_in_p = jnp.concatenate([w_in[:, o3:], w_in[:, :o2], _spread_rope(w_in[:, o2:o3])], axis=1).astype(BF16)
    wq = p["mla_w_uq"][l].reshape(Q_LORA, MLA_HEADS, QK_HEAD)
    wq_p = jnp.concatenate([wq[..., :QK_NOPE], _spread_rope(wq[..., QK_NOPE:])], axis=-1)
    wkv = p["mla_w_ukv"][l].reshape(KV_LORA, MLA_HEADS, QK_NOPE + V_HEAD)
    qhg, khg = p["mla_q_head_norm_g"][l], p["mla_k_head_norm_g"][l]
    hid = (LANES, LANES)
    return dict(
        attn_g=row(p["attn_norm_g"]), w_in=w_in_p,
        pool_w=p["pool_w"][l].astype(BF16), pool_scale=row(p["pool_scale"]),
        qn=row(p["mla_q_norm_g"]), wq=wq_p.reshape(Q_LORA, MLA_HEADS * HEAD_PAD).astype(BF16),
        kvn=row(p["mla_kv_norm_g"]),
        wk=wkv[..., :QK_NOPE].reshape(KV_LORA, MLA_HEADS * QK_NOPE).astype(BF16),
        wv=wkv[..., QK_NOPE:].reshape(KV_LORA, MLA_DIM).T.astype(BF16),
        qhg=jnp.concatenate([qhg[:QK_NOPE], _spread_rope(qhg[QK_NOPE:])])[None, :],
        khn=khg[None, :QK_NOPE], khr=_spread_rope(khg[QK_NOPE:])[None, :],
        hy_cw=p["hy_conv_w"][l], hy_cb=row(p["hy_conv_b"]), hy_bias=row(p["hy_bias"]),
        f_w1=_pad_to(p["hy_filt_w1"][l], hid), f_b1=_pad_to(row(p["hy_filt_b1"]), (1, LANES)),
        f_fr=_pad_to(row(p["hy_filt_freq"]), (1, LANES)),
        f_w2=_pad_to(p["hy_filt_w2"][l], hid), f_b2=_pad_to(row(p["hy_filt_b2"]), (1, LANES)),
        f_w3=_pad_to(p["hy_filt_w3"][l], (LANES, 2 * HY_DIM)), f_b3=row(p["hy_filt_b3"]),
        g_pool=p["grp_norm_g"][l][None, :POOL_DIM], g_mla=p["grp_norm_g"][l][None, POOL_DIM:POOL_DIM + MLA_DIM],
        g_hy=p["grp_norm_g"][l][None, POOL_DIM + MLA_DIM:],
        ffn_g=row(p["ffn_norm_g"]), ffn_cw=p["ffn_conv_w"][l], ffn_cb=row(p["ffn_conv_b"]),
    )


def _tile(n, want):
    return min(n, want)


def _layer(x2, b, seq, layer, w, stacks, kr, ki, rope, tables):
    cos_t, sin_t = rope
    m = b * seq
    q2, k2, vt, m_pool, z, x0 = _in_proj(
        x2, seq, w["attn_g"], w["w_in"], cos_t, sin_t, w["qn"], w["wq"], w["kvn"], w["wk"], w["wv"], w["qhg"],
        w["khn"], w["khr"], w["pool_w"], w["pool_scale"], w["g_pool"], w["hy_cw"], w["hy_cb"], _tile(seq, 512))
    hp = MLA_HEADS * HEAD_PAD
    m_mla = _attention(q2.reshape(b, seq, hp), k2.reshape(b, seq, hp), vt,
                       w["g_mla"], _tile(seq, 512))
    m_hy = _hy_conv(tables, z, x0, kr, ki, w["hy_bias"], w["g_hy"])
    x2 = _out_proj(x2, m_pool.reshape(m, POOL_DIM), m_mla.reshape(m, MLA_DIM), m_hy.reshape(m, HY_DIM),
                   stacks["w_out"], layer, _tile(seq, 512))
    return _ffn(x2, seq, w["ffn_g"], stacks["w_up"], w["ffn_cw"], w["ffn_cb"], stacks["w_down"], layer,
                _tile(seq, 1024), 512)


def kernel(x_prompt, x_sample, attn_norm_g, w_in, pool_w, pool_scale, mla_q_norm_g, mla_w_uq, mla_kv_norm_g, mla_w_ukv, mla_q_head_norm_g, mla_k_head_norm_g, hy_conv_w, hy_conv_b, hy_filt_w1, hy_filt_b1, hy_filt_freq, hy_filt_w2, hy_filt_b2, hy_filt_w3, hy_filt_b3, hy_bias, grp_norm_g, w_out, ffn_norm_g, ffn_w_up, ffn_conv_w, ffn_conv_b, ffn_w_down):
    p = dict(attn_norm_g=attn_norm_g, w_in=w_in, pool_w=pool_w, pool_scale=pool_scale, mla_q_norm_g=mla_q_norm_g,
             mla_w_uq=mla_w_uq, mla_kv_norm_g=mla_kv_norm_g, mla_w_ukv=mla_w_ukv,
             mla_q_head_norm_g=mla_q_head_norm_g, mla_k_head_norm_g=mla_k_head_norm_g, hy_conv_w=hy_conv_w,
             hy_conv_b=hy_conv_b, hy_filt_w1=hy_filt_w1, hy_filt_b1=hy_filt_b1, hy_filt_freq=hy_filt_freq,
             hy_filt_w2=hy_filt_w2, hy_filt_b2=hy_filt_b2, hy_filt_w3=hy_filt_w3, hy_filt_b3=hy_filt_b3,
             hy_bias=hy_bias, grp_norm_g=grp_norm_g, w_out=w_out, ffn_norm_g=ffn_norm_g, ffn_w_up=ffn_w_up,
             ffn_conv_w=ffn_conv_w, ffn_conv_b=ffn_conv_b, ffn_w_down=ffn_w_down)
    depth = attn_norm_g.shape[0]
    seq = x_prompt.shape[1]
    assert x_sample.shape[1] == seq and x_prompt.shape[2] == D_MODEL
    rope = _rope_tables(seq)
    tables = _dft_tables(seq)
    feat = _filter_features(seq)
    rates = _decay_rates()
    stacks = dict(w_out=w_out.astype(BF16), w_up=ffn_w_up.astype(BF16), w_down=ffn_w_down.astype(BF16))
    xs = [x.reshape(-1, D_MODEL) for x in (x_prompt, x_sample)]
    for l in range(depth):
        w = _layer_weights(p, l)
        hs, hd = _hy_filter(feat, w["f_w1"], w["f_b1"], w["f_fr"], w["f_w2"], w["f_b2"], w["f_w3"], w["f_b3"],
                            rates, _tile(seq, 512))
        half = (2, seq // 2, HY_DIM)
        kr, ki = _hy_spec(tables, hs.reshape(half), hd.reshape(half))
        xs = [_layer(x2, x.shape[0], seq, l, w, stacks, kr, ki, rope, tables)
              for x2, x in zip(xs, (x_prompt, x_sample))]
    return (xs[0].reshape(x_prompt.shape), xs[1].reshape(x_sample.shape))
```

```python
import functools
import math

import jax
import jax.numpy as jnp
from jax import lax
from jax.experimental import pallas as pl
from jax.experimental.pallas import tpu as pltpu

F32 = jnp.float32
BF16 = jnp.bfloat16

D_MODEL = 2048
DEPTH = 4
POOL_DIM = 512
POOL_WINDOWS = (2, 4, 8, 16)
POOL_GROUP = 128
MLA_HEADS = 8
QK_NOPE = 128
QK_ROPE = 64
QK_HEAD = QK_NOPE + QK_ROPE
V_HEAD = 128
MLA_DIM = MLA_HEADS * V_HEAD
Q_LORA = 512
KV_LORA = 256
ROPE_THETA = 10000.0
HY_DIM = 512
N_BANDS = 8
N_POS_FEAT = 1 + 2 * N_BANDS
FILT_HIDDEN = 64
HY_FAST_DECAY = 0.3
HY_SLOW_DECAY = 1.5
HY_DECAY_TARGET = 1e-2
D_FF = 5632
NORM_EPS = 1e-6

LANES = 128
HEAD_PAD = 2 * LANES
ROPE_HALF = QK_ROPE // 2
VT_ROWS = V_HEAD + 16
HALF_CH = 2 * LANES
COL_POOL = 3 * HY_DIM
COL_CQ = COL_POOL + POOL_DIM
HALO = 16
VMEM_LIMIT = 56 * 1024 * 1024


def _params(*sem):
    return pltpu.CompilerParams(dimension_semantics=sem, vmem_limit_bytes=VMEM_LIMIT)


def _resident(shape):
    nd = len(shape)
    return pl.BlockSpec(shape, lambda *_: (0,) * nd, pipeline_mode=pl.Buffered(1))


def _mm(a, b):
    return jnp.dot(a, b, preferred_element_type=F32)


def _rms(x, g, n=None):
    n = x.shape[-1] if n is None else n
    ms = jnp.sum(x * x, axis=-1, keepdims=True) * (1.0 / n)
    return x * lax.rsqrt(ms + NORM_EPS) * g


def _rope(r, cos_t, sin_t):
    return r * cos_t + pltpu.roll(r, LANES // 2, 1) * sin_t


def _in_proj_kernel(x_ref, xp_ref, xn_ref, g_ref, w_ref, cos_ref, sin_ref, qn_ref, wq_ref, kvn_ref, wk_ref, wv_ref,
                    qhg_ref, khn_ref, khr_ref, pw_ref, ps_ref, pg_ref, cw_ref, cb_ref,
                    q_ref, k_ref, v_ref, mp_ref, z_ref, x0_ref, h_s, t_s, y_s, *, blocks_per_seq, seq):
    tm = x_ref.shape[0]
    g = g_ref[...]
    pos = pl.program_id(0) % blocks_per_seq
    h = _rms(x_ref[...], g).astype(BF16)
    h_s[HALO:HALO + tm, :] = h
    pad = jnp.zeros((HALO - 8, x_ref.shape[1]), F32)
    hp = jnp.where(pos == 0, 0.0, _rms(xp_ref[...], g))
    hn = jnp.where(pos == blocks_per_seq - 1, 0.0, _rms(xn_ref[...], g))
    h_s[0:HALO, :] = jnp.concatenate([pad, hp], axis=0).astype(BF16)
    h_s[HALO + tm:, :] = jnp.concatenate([hn, pad], axis=0).astype(BF16)
    lat = _mm(h, w_ref[:, COL_CQ:])
    cq = _rms(lat[:, :Q_LORA], qn_ref[...]).astype(BF16)
    ckv = _rms(lat[:, Q_LORA:Q_LORA + KV_LORA], kvn_ref[...]).astype(BF16)
    kr = lat[:, Q_LORA + KV_LORA:]
    qf = _mm(cq, wq_ref[...])
    kn = _mm(ckv, wk_ref[...])
    vt = lax.dot_general(wv_ref[...], ckv, (((1,), (1,)), ((), ())), preferred_element_type=F32)
    pool_u = _mm(h_s[...], w_ref[:, COL_POOL:COL_CQ])
    hy_u = [[_mm(h_s[...], w_ref[:, sec * HY_DIM + cc * HALF_CH:sec * HY_DIM + (cc + 1) * HALF_CH])
             for sec in range(3)] for cc in range(HY_DIM // HALF_CH)]

    cos_t = cos_ref[...]
    sin_t = sin_ref[...]
    scale = QK_HEAD ** -0.5 * math.log2(math.e)
    qg = qhg_ref[...] * scale
    qg_n, qg_r = qg[:, :LANES], qg[:, LANES:]
    rope_a = qg_r * cos_t
    rope_b = pltpu.roll(qg_r, LANES // 2, 1) * sin_t
    for hd in range(MLA_HEADS):
        qh = qf[:, hd * HEAD_PAD:(hd + 1) * HEAD_PAD]
        rs = lax.rsqrt(jnp.sum(qh * qh, axis=-1, keepdims=True) * (1.0 / QK_HEAD) + NORM_EPS)
        qr = qh[:, LANES:]
        q_ref[:, hd * HEAD_PAD:hd * HEAD_PAD + LANES] = (qh[:, :LANES] * rs * qg_n).astype(BF16)
        q_ref[:, hd * HEAD_PAD + LANES:(hd + 1) * HEAD_PAD] = (
            (qr * rope_a + pltpu.roll(qr, LANES // 2, 1) * rope_b) * rs).astype(BF16)
    ones_rows = (lax.broadcasted_iota(jnp.int32, (VT_ROWS - V_HEAD, vt.shape[1]), 0) == 0).astype(BF16)
    for hd in range(MLA_HEADS):
        v_ref[0, hd * VT_ROWS:hd * VT_ROWS + V_HEAD, :] = vt[hd * V_HEAD:(hd + 1) * V_HEAD, :].astype(BF16)
        v_ref[0, hd * VT_ROWS + V_HEAD:(hd + 1) * VT_ROWS, :] = ones_rows
    ss_r = jnp.sum(kr * kr, axis=-1, keepdims=True)
    kr_rot = _rope(kr * khr_ref[...], cos_t, sin_t)
    for hd in range(MLA_HEADS):
        knh = kn[:, hd * LANES:(hd + 1) * LANES]
        ms = (jnp.sum(knh * knh, axis=-1, keepdims=True) + ss_r) * (1.0 / QK_HEAD)
        rs = lax.rsqrt(ms + NORM_EPS)
        k_ref[:, hd * HEAD_PAD:hd * HEAD_PAD + LANES] = (knh * rs * khn_ref[...]).astype(BF16)
        k_ref[:, hd * HEAD_PAD + LANES:(hd + 1) * HEAD_PAD] = (kr_rot * rs).astype(BF16)

    ext = tm + 2 * HALO
    main = slice(HALO, HALO + tm)
    down = lambda a, n: pltpu.roll(a, n, 0)
    up = lambda a, n: pltpu.roll(a, ext - n, 0)

    tok = lax.broadcasted_iota(jnp.int32, (tm, 1), 0) + pos * tm
    for gi, win in enumerate(POOL_WINDOWS):
        half = win // 2
        cols = slice(gi * POOL_GROUP, (gi + 1) * POOL_GROUP)
        xg = pool_u[:, cols]
        ahead = xg
        behind = down(xg, 1)
        span = 1
        while span < half:
            ahead = ahead + up(ahead, span)
            behind = behind + down(behind, span)
            span *= 2
        tot = (ahead + behind)[main]
        cnt = (jnp.minimum(tok + half, seq) - jnp.maximum(tok - half, 0)).astype(F32)
        pooled = (tot / cnt - xg[main]).astype(BF16)
        y_s[:, cols] = _mm(pooled, pw_ref[gi]) * ps_ref[:, cols]
    mp_ref[...] = _rms(y_s[...], pg_ref[...]).astype(BF16)

    def conv3(xc, col):
        cols = slice(col, col + LANES)
        w = cw_ref[:, cols]
        return (down(xc, 1) * w[0:1] + xc * w[1:2] + up(xc, 1) * w[2:3])[main] + cb_ref[:, cols]

    def store_split(dst_ref, slab, cols, val):
        t_s[slab] = val
        dst_ref[0, 0, :, cols] = t_s[slab, pl.ds(0, tm // 2, stride=2), :].astype(BF16)
        dst_ref[0, 1, :, cols] = t_s[slab, pl.ds(1, tm // 2, stride=2), :].astype(BF16)

    for c in range(HY_DIM // LANES):
        cols = slice(c * LANES, (c + 1) * LANES)
        x0u, x1u, vu = (u[:, (c % 2) * LANES:(c % 2 + 1) * LANES] for u in hy_u[c // 2])
        store_split(x0_ref, 2 * c, cols, conv3(x0u, c * LANES))
        store_split(z_ref, 2 * c + 1, cols, conv3(vu, 2 * HY_DIM + c * LANES) * conv3(x1u, HY_DIM + c * LANES))


def _in_proj(x2, seq, g, w, cos_t, sin_t, qn, wq, kvn, wk, wv, qhg, khn, khr, pool_w, pool_scale, g_pool,
             hy_cw, hy_cb, tm):
    m, d = x2.shape
    nseq = seq // tm
    hp = MLA_HEADS * HEAD_PAD
    r8 = tm // 8
    last8 = m // 8 - 1
    row = lambda width: pl.BlockSpec((tm, width), lambda i: (i, 0))
    tab = pl.BlockSpec((tm, LANES), lambda i: (i % nseq, 0))
    split = pl.BlockSpec((1, 2, tm // 2, HY_DIM), lambda i: (i // nseq, 0, i % nseq, 0))
    split_shape = jax.ShapeDtypeStruct((m // seq, 2, seq // 2, HY_DIM), BF16)
    consts = (qn, wq, kvn, wk, wv, qhg, khn, khr, pool_w, pool_scale, g_pool, hy_cw, hy_cb)
    return pl.pallas_call(
        functools.partial(_in_proj_kernel, blocks_per_seq=nseq, seq=seq),
        grid=(m // tm,),
        in_specs=[row(d),
                  pl.BlockSpec((8, d), lambda i: (jnp.maximum(i * r8 - 1, 0), 0)),
                  pl.BlockSpec((8, d), lambda i: (jnp.minimum((i + 1) * r8, last8), 0)),
                  _resident((1, d)), _resident(w.shape), tab, tab] + [_resident(a.shape) for a in consts],
        out_specs=[row(hp), row(hp),
                   pl.BlockSpec((1, MLA_HEADS * VT_ROWS, tm), lambda i: (i // nseq, 0, i % nseq)),
                   row(POOL_DIM), split, split],
        out_shape=[jax.ShapeDtypeStruct((m, hp), BF16), jax.ShapeDtypeStruct((m, hp), BF16),
                   jax.ShapeDtypeStruct((m // seq, MLA_HEADS * VT_ROWS, seq), BF16),
                   jax.ShapeDtypeStruct((m, POOL_DIM), BF16), split_shape, split_shape],
        scratch_shapes=[pltpu.VMEM((tm + 2 * HALO, d), BF16), pltpu.VMEM((2 * HY_DIM // LANES, tm, LANES), F32),
                        pltpu.VMEM((tm, POOL_DIM), F32)],
        compiler_params=_params("parallel"),
        name="in_proj",
    )(x2, x2, x2, g, w, cos_t, sin_t, *consts)


def _attn_kernel(q_ref, k_ref, v_ref, g_ref, o_ref, y_s):
    def scores(h):
        q = q_ref[0, :, h * HEAD_PAD:(h + 1) * HEAD_PAD]
        k = k_ref[0, :, h * HEAD_PAD:(h + 1) * HEAD_PAD]
        return lax.dot_general(k, q, (((1,), (1,)), ((), ())), preferred_element_type=F32)

    st_next = scores(0)
    for h in range(MLA_HEADS):
        st = st_next
        if h + 1 < MLA_HEADS:
            st_next = scores(h + 1)
        pt = jnp.exp2(st - jnp.max(st, axis=0, keepdims=True)).astype(BF16)
        ot = jnp.dot(v_ref[0, h * VT_ROWS:(h + 1) * VT_ROWS, :], pt, preferred_element_type=F32)
        yt = ot[:V_HEAD, :] / ot[V_HEAD:V_HEAD + 1, :]
        y_s[:, h * V_HEAD:(h + 1) * V_HEAD] = yt.T
    o_ref[0] = _rms(y_s[...], g_ref[...]).astype(BF16)


def _attention(q3, k3, v3, g, tq):
    b, seq, hp = q3.shape
    return pl.pallas_call(
        _attn_kernel,
        grid=(b, seq // tq),
        in_specs=[pl.BlockSpec((1, tq, hp), lambda i, j: (i, j, 0)),
                  pl.BlockSpec((1, seq, hp), lambda i, j: (i, 0, 0)),
                  pl.BlockSpec((1, MLA_HEADS * VT_ROWS, seq), lambda i, j: (i, 0, 0)),
                  _resident((1, MLA_DIM))],
        out_specs=pl.BlockSpec((1, tq, MLA_DIM), lambda i, j: (i, j, 0)),
        out_shape=jax.ShapeDtypeStruct((b, seq, MLA_DIM), BF16),
        scratch_shapes=[pltpu.VMEM((tq, MLA_DIM), F32)],
        compiler_params=_params("parallel", "arbitrary"),
        name="attention",
    )(q3, k3, v3, g)


def _attn_out_kernel(q_ref, k_ref, v_ref, g_ref, x_ref, a_ref, c_ref, w_ref, o_ref, y_s, ym_s):
    @pl.when(pl.program_id(0) == 0)
    def _():
        ym_s[...] = jnp.zeros_like(ym_s)

    chunk = D_MODEL // MLA_HEADS

    def project(h):
        cols = slice(h * chunk, (h + 1) * chunk)
        acc = _mm(a_ref[...], w_ref[0:POOL_DIM, cols])
        acc = acc + _mm(ym_s[...], w_ref[POOL_DIM:POOL_DIM + MLA_DIM, cols])
        acc = acc + _mm(c_ref[...], w_ref[POOL_DIM + MLA_DIM:, cols])
        o_ref[:, cols] = x_ref[:, cols] + acc

    def scores(h):
        q = q_ref[0, :, h * HEAD_PAD:(h + 1) * HEAD_PAD]
        k = k_ref[0, :, h * HEAD_PAD:(h + 1) * HEAD_PAD]
        return lax.dot_general(k, q, (((1,), (1,)), ((), ())), preferred_element_type=F32)

    st_next = scores(0)
    for h in range(MLA_HEADS):
        st = st_next
        if h + 1 < MLA_HEADS:
            st_next = scores(h + 1)
        project(h)
        pt = jnp.exp2(st - jnp.max(st, axis=0, keepdims=True)).astype(BF16)
        ot = jnp.dot(v_ref[0, h * VT_ROWS:(h + 1) * VT_ROWS, :], pt, preferred_element_type=F32)
        yt = ot[:V_HEAD, :] / ot[V_HEAD:V_HEAD + 1, :]
        y_s[:, h * V_HEAD:(h + 1) * V_HEAD] = yt.T
    ym_s[...] = _rms(y_s[...], g_ref[...]).astype(BF16)


def _attention_out(q3, k3, v3, g, x2, ma, mc, w_all, layer, tq):
    b, seq, hp = q3.shape
    m, d = x2.shape
    nq = seq // tq
    n = b * nq
    cur = lambda s: jnp.minimum(s, n - 1)
    prev = lambda s: jnp.maximum(s - 1, 0)
    once = pl.Buffered(1)
    lag = lambda width: pl.BlockSpec((tq, width), lambda s: (prev(s), 0))
    return pl.pallas_call(
        _attn_out_kernel,
        grid=(n + 1,),
        in_specs=[pl.BlockSpec((1, tq, hp), lambda s: (cur(s) // nq, cur(s) % nq, 0)),
                  pl.BlockSpec((1, seq, hp), lambda s: (cur(s) // nq, 0, 0), pipeline_mode=once),
                  pl.BlockSpec((1, MLA_HEADS * VT_ROWS, seq), lambda s: (cur(s) // nq, 0, 0), pipeline_mode=once),
                  _resident((1, MLA_DIM)), lag(d), lag(POOL_DIM), lag(HY_DIM),
                  pl.BlockSpec((None,) + w_all.shape[1:], lambda s: (layer, 0, 0), pipeline_mode=once)],
        out_specs=lag(d),
        out_shape=jax.ShapeDtypeStruct((m, d), F32),
        scratch_shapes=[pltpu.VMEM((tq, MLA_DIM), F32), pltpu.VMEM((tq, MLA_DIM), BF16)],
        compiler_params=_params("arbitrary"),
        name="attn_out",
    )(q3, k3, v3, g, x2, ma, mc, w_all)


def _hy_filter_kernel(feat_ref, w1_ref, b1_ref, fr_ref, w2_ref, b2_ref, w3_ref, b3_ref, dl_ref, hs_ref, hd_ref):
    hi = lax.Precision.HIGHEST
    feat = feat_ref[...]
    fr = fr_ref[...]
    h = jnp.sin(fr * (jnp.dot(feat, w1_ref[...], precision=hi, preferred_element_type=F32) + b1_ref[...]))
    h = jnp.sin(fr * (jnp.dot(h, w2_ref[...], precision=hi, preferred_element_type=F32) + b2_ref[...]))
    h = jnp.dot(h, w3_ref[...], precision=hi, preferred_element_type=F32) + b3_ref[...]
    decay = jnp.exp(-feat[:, 0:1] * dl_ref[...])
    fwd = h[:, :HY_DIM] * decay
    row = lax.broadcasted_iota(jnp.int32, (feat.shape[0], 1), 0) + pl.program_id(0) * feat.shape[0]
    bwd = jnp.where(row > 0, h[:, HY_DIM:] * decay, 0.0)
    hs_ref[...] = (fwd + bwd).astype(BF16)
    hd_ref[...] = (bwd - fwd).astype(BF16)


def _hy_filter(feat, w1, b1, fr, w2, b2, w3, b3, dl, tl):
    seq = feat.shape[0]
    out = jax.ShapeDtypeStruct((seq, HY_DIM), BF16)
    return pl.pallas_call(
        _hy_filter_kernel,
        grid=(seq // tl,),
        in_specs=[pl.BlockSpec((tl, LANES), lambda i: (i, 0))] + [_resident(a.shape) for a in (w1, b1, fr, w2, b2, w3, b3, dl)],
        out_specs=[pl.BlockSpec((tl, HY_DIM), lambda i: (i, 0))] * 2,
        out_shape=[out, out],
        compiler_params=_params("parallel"),
        name="hy_filter",
    )(feat, w1, b1, fr, w2, b2, w3, b3, dl)


def _hy_spec_kernel(ce_ref, co_ref, se_ref, so_ref, hs_ref, hd_ref, kr_ref, ki_ref):
    norm = 0.5 / ce_ref.shape[1]
    ae, ao = _mm(ce_ref[...], hs_ref[0]), _mm(co_ref[...], hs_ref[1])
    be, bo = _mm(se_ref[...], hd_ref[0]), _mm(so_ref[...], hd_ref[1])
    kr_ref[0] = (ae + ao) * norm
    kr_ref[1] = (ae - ao) * norm
    ki_ref[0] = (be + bo) * norm
    ki_ref[1] = (bo - be) * norm


def _hy_spec(tables, hs, hd):
    half = hs.shape[1]
    blk = pl.BlockSpec((2, half, HALF_CH), lambda c: (0, 0, c))
    out = jax.ShapeDtypeStruct((2, half, HY_DIM), F32)
    return pl.pallas_call(
        _hy_spec_kernel,
        grid=(HY_DIM // HALF_CH,),
        in_specs=[_resident(t.shape) for t in tables[:4]] + [blk, blk],
        out_specs=[blk, blk],
        out_shape=[out, out],
        compiler_params=_params("parallel"),
        name="hy_spec",
    )(*tables[:4], hs, hd)


def _hy_conv_kernel(z_ref, x0_ref, ce_ref, co_ref, se_ref, so_ref, cet_ref, cot_ref, set_ref, sot_ref,
                    kr_ref, ki_ref, b_ref, g_ref, o_ref, y_s):
    c = pl.program_id(1)
    half = z_ref.shape[2]
    ze, zo = z_ref[0, 0], z_ref[0, 1]
    ae, ao = _mm(ce_ref[...], ze), _mm(co_ref[...], zo)
    be, bo = _mm(se_ref[...], ze), _mm(so_ref[...], zo)
    u_lo, u_hi, v_lo, v_hi = ae + ao, ae - ao, be + bo, bo - be
    kr_lo, kr_hi, ki_lo, ki_hi = kr_ref[0], kr_ref[1], ki_ref[0], ki_ref[1]
    yr_lo, w_lo = u_lo * kr_lo + v_lo * ki_lo, v_lo * kr_lo - u_lo * ki_lo
    yr_hi, w_hi = u_hi * kr_hi + v_hi * ki_hi, v_hi * kr_hi - u_hi * ki_hi
    ye = _mm(cet_ref[...], (yr_lo + yr_hi).astype(BF16)) + _mm(set_ref[...], (w_lo - w_hi).astype(BF16))
    yo = _mm(cot_ref[...], (yr_lo - yr_hi).astype(BF16)) + _mm(sot_ref[...], (w_lo + w_hi).astype(BF16))
    bias = b_ref[...]
    ye = (ye + ze.astype(F32) * bias) * x0_ref[0, 0].astype(F32)
    yo = (yo + zo.astype(F32) * bias) * x0_ref[0, 1].astype(F32)
    per = HALF_CH // LANES
    for k in range(per):
        y_s[c * per + k, pl.ds(0, half, stride=2), :] = ye[:, k * LANES:(k + 1) * LANES]
        y_s[c * per + k, pl.ds(1, half, stride=2), :] = yo[:, k * LANES:(k + 1) * LANES]

    @pl.when(c == pl.num_programs(1) - 1)
    def _():
        slabs = [y_s[k] for k in range(HY_DIM // LANES)]
        ms = sum(jnp.sum(y * y, axis=-1, keepdims=True) for y in slabs) * (1.0 / HY_DIM)
        rs = lax.rsqrt(ms + NORM_EPS)
        for k, y in enumerate(slabs):
            cols = slice(k * LANES, (k + 1) * LANES)
            o_ref[0, :, cols] = (y * rs * g_ref[:, cols]).astype(BF16)


def _hy_conv(tables, z, x0, kr, ki, bias, g):
    b, _, half, _ = z.shape
    data = pl.BlockSpec((1, 2, half, HALF_CH), lambda i, c: (i, 0, 0, c))
    spec = pl.BlockSpec((2, half, HALF_CH), lambda i, c: (0, 0, c))
    return pl.pallas_call(
        _hy_conv_kernel,
        grid=(b, HY_DIM // HALF_CH),
        in_specs=[data, data] + [_resident(t.shape) for t in tables] + [spec, spec,
                  pl.BlockSpec((1, HALF_CH), lambda i, c: (0, c)), _resident((1, HY_DIM))],
        out_specs=pl.BlockSpec((1, 2 * half, HY_DIM), lambda i, c: (i, 0, 0)),
        out_shape=jax.ShapeDtypeStruct((b, 2 * half, HY_DIM), BF16),
        scratch_shapes=[pltpu.VMEM((HY_DIM // LANES, 2 * half, LANES), F32)],
        compiler_params=_params("parallel", "arbitrary"),
        name="hy_conv",
    )(z, x0, *tables, kr, ki, bias, g)


def _out_proj_kernel(x_ref, a_ref, b_ref, c_ref, w_ref, o_ref):
    acc = jnp.dot(a_ref[...], w_ref[0:POOL_DIM, :], preferred_element_type=F32)
    acc = acc + jnp.dot(b_ref[...], w_ref[POOL_DIM:POOL_DIM + MLA_DIM, :], preferred_element_type=F32)
    acc = acc + jnp.dot(c_ref[...], w_ref[POOL_DIM + MLA_DIM:, :], preferred_element_type=F32)
    o_ref[...] = x_ref[...] + acc


def _out_proj(x2, ma, mb, mc, w_all, layer, tm):
    m, d = x2.shape
    row = lambda width: pl.BlockSpec((tm, width), lambda i: (i, 0))
    w_spec = pl.BlockSpec((None,) + w_all.shape[1:], lambda i: (layer, 0, 0), pipeline_mode=pl.Buffered(1))
    return pl.pallas_call(
        _out_proj_kernel,
        grid=(m // tm,),
        in_specs=[row(d), row(POOL_DIM), row(MLA_DIM), row(HY_DIM), w_spec],
        out_specs=row(d),
        out_shape=jax.ShapeDtypeStruct((m, d), F32),
        compiler_params=_params("parallel"),
        name="out_proj",
    )(x2, ma, mb, mc, w_all)


def _ffn_kernel(x_ref, xp_ref, xn_ref, g_ref, wg_ref, wv_ref, cw_ref, cb_ref, wd_ref, o_ref, h_s, up_s, *, blocks_per_seq):
    tm = x_ref.shape[0]
    f = pl.program_id(1)

    @pl.when(f == 0)
    def _():
        x = x_ref[...]
        g = g_ref[...]
        o_ref[...] = x
        h_s[HALO:HALO + tm, :] = _rms(x, g).astype(BF16)
        pos = pl.program_id(0) % blocks_per_seq
        pad = jnp.zeros((HALO - 8, x.shape[1]), F32)
        hp = jnp.where(pos == 0, 0.0, _rms(xp_ref[...], g))
        hn = jnp.where(pos == blocks_per_seq - 1, 0.0, _rms(xn_ref[...], g))
        h_s[0:HALO, :] = jnp.concatenate([pad, hp], axis=0).astype(BF16)
        h_s[HALO + tm:, :] = jnp.concatenate([hn, pad], axis=0).astype(BF16)

    up_s[...] = jnp.dot(h_s[...], wg_ref[...], preferred_element_type=F32)
    val = jnp.dot(h_s[HALO:HALO + tm, :], wv_ref[...], preferred_element_type=F32)
    cw = cw_ref[...]
    gate = (up_s[HALO - 1:HALO - 1 + tm, :] * cw[0:1] + up_s[HALO:HALO + tm, :] * cw[1:2]
            + up_s[HALO + 1:HALO + 1 + tm, :] * cw[2:3] + cb_ref[...])
    act = (gate * jax.nn.sigmoid(gate) * val).astype(BF16)
    o_ref[...] += jnp.dot(act, wd_ref[...], preferred_element_type=F32)


def _ffn(x2, seq, g, w_up, cw, cb, w_down, layer, tm, tf):
    m, d = x2.shape
    nf = D_FF // tf
    r8 = tm // 8
    last8 = m // 8 - 1
    kern = functools.partial(_ffn_kernel, blocks_per_seq=seq // tm)
    return pl.pallas_call(
        kern,
        grid=(m // tm, nf),
        in_specs=[pl.BlockSpec((tm, d), lambda i, f: (i, 0)),
                  pl.BlockSpec((8, d), lambda i, f: (jnp.maximum(i * r8 - 1, 0), 0)),
                  pl.BlockSpec((8, d), lambda i, f: (jnp.minimum((i + 1) * r8, last8), 0)),
                  _resident((1, d)),
                  pl.BlockSpec((None, d, tf), lambda i, f: (layer, 0, f)),
                  pl.BlockSpec((None, d, tf), lambda i, f: (layer, 0, f + nf)),
                  pl.BlockSpec((3, tf), lambda i, f: (0, f)),
                  pl.BlockSpec((1, tf), lambda i, f: (0, f)),
                  pl.BlockSpec((None, tf, d), lambda i, f: (layer, f, 0))],
        out_specs=pl.BlockSpec((tm, d), lambda i, f: (i, 0)),
        out_shape=jax.ShapeDtypeStruct((m, d), F32),
        scratch_shapes=[pltpu.VMEM((tm + 2 * HALO, d), BF16), pltpu.VMEM((tm + 2 * HALO, tf), F32)],
        compiler_params=_params("parallel", "arbitrary"),
        name="ffn",
    )(x2, x2, x2, g, w_up, w_up, cw, cb, w_down)


def _rope_tables(seq):
    freqs = ROPE_THETA ** (-jnp.arange(0, QK_ROPE, 2, dtype=F32) / QK_ROPE)
    ang = jnp.arange(seq, dtype=F32)[:, None] * freqs[None, :]
    cos, sin, zero = jnp.cos(ang), jnp.sin(ang), jnp.zeros((seq, ROPE_HALF), F32)
    return (jnp.concatenate([cos, zero, cos, zero], axis=1), jnp.concatenate([-sin, zero, sin, zero], axis=1))


def _dft_tables(seq):
    half = seq // 2
    f = jnp.arange(half, dtype=jnp.int32)[:, None]
    j = jnp.arange(half, dtype=jnp.int32)[None, :]
    out = []
    for tok in (2 * j, 2 * j + 1):
        ang = (((2 * f + 1) * tok) % (4 * seq)).astype(F32) * (math.pi / (2 * seq))
        out.append((jnp.cos(ang).astype(BF16), jnp.sin(ang).astype(BF16)))
    (ce, se), (co, so) = out
    return ce, co, se, so, ce.T, co.T, se.T, so.T


def _filter_features(seq):
    t = jnp.linspace(0.0, 1.0, seq, dtype=F32)[:, None]
    w = 2.0 * math.pi * jnp.arange(seq, dtype=F32)[:, None] / seq
    bands = jnp.linspace(1e-4, N_BANDS - 1, N_BANDS, dtype=F32)[None, :]
    z = jnp.concatenate([t, jnp.cos(bands * w), -jnp.sin(bands * w)], axis=-1)
    z = jnp.concatenate([z[0::2], z[1::2]], axis=0)
    return jnp.pad(z, ((0, 0), (0, LANES - N_POS_FEAT)))


def _decay_rates():
    max_decay = math.log(HY_DECAY_TARGET) / HY_FAST_DECAY
    min_decay = math.log(HY_DECAY_TARGET) / HY_SLOW_DECAY
    return jnp.abs(jnp.linspace(min_decay, max_decay, HY_DIM, dtype=F32))[None, :]


def _spread_rope(a):
    zero = jnp.zeros(a.shape[:-1] + (ROPE_HALF,), a.dtype)
    return jnp.concatenate([a[..., :ROPE_HALF], zero, a[..., ROPE_HALF:], zero], axis=-1)


def _pad_to(a, shape):
    return jnp.pad(a, [(0, t - s) for s, t in zip(a.shape, shape)])


def _layer_weights(p, l):
    row = lambda a: a[l][None, :].astype(F32)
    w_in = p["w_in"][l]
    o1, o2, o3 = POOL_DIM + Q_LORA, POOL_DIM + Q_LORA + KV_LORA, POOL_DIM + Q_LORA + KV_LORA + QK_ROPE
    w_in_p = jnp.concatenate([w_in[:, o3:], w_in[:, :o2], _spread_rope(w_in[:, o2:o3])], axis=1).astype(BF16)
    wq = p["mla_w_uq"][l].reshape(Q_LORA, MLA_HEADS, QK_HEAD)
    wq_p = jnp.concatenate([wq[..., :QK_NOPE], _spread_rope(wq[..., QK_NOPE:])], axis=-1)
    wkv = p["mla_w_ukv"][l].reshape(KV_LORA, MLA_HEADS, QK_NOPE + V_HEAD)
    qhg, khg = p["mla_q_head_norm_g"][l], p["mla_k_head_norm_g"][l]
    hid = (LANES, LANES)
    return dict(
        attn_g=row(p["attn_norm_g"]), w_in=w_in_p,
        pool_w=p["pool_w"][l].astype(BF16), pool_scale=row(p["pool_scale"]),
        qn=row(p["mla_q_norm_g"]), wq=wq_p.reshape(Q_LORA, MLA_HEADS * HEAD_PAD).astype(BF16),
        kvn=row(p["mla_kv_norm_g"]),
        wk=wkv[..., :QK_NOPE].reshape(KV_LORA, MLA_HEADS * QK_NOPE).astype(BF16),
        wv=wkv[..., QK_NOPE:].reshape(KV_LORA, MLA_DIM).T.astype(BF16),
        qhg=jnp.concatenate([qhg[:QK_NOPE], _spread_rope(qhg[QK_NOPE:])])[None, :],
        khn=khg[None, :QK_NOPE], khr=_spread_rope(khg[QK_NOPE:])[None, :],
        hy_cw=p["hy_conv_w"][l], hy_cb=row(p["hy_conv_b"]), hy_bias=row(p["hy_bias"]),
        f_w1=_pad_to(p["hy_filt_w1"][l], hid), f_b1=_pad_to(row(p["hy_filt_b1"]), (1, LANES)),
        f_fr=_pad_to(row(p["hy_filt_freq"]), (1, LANES)),
        f_w2=_pad_to(p["hy_filt_w2"][l], hid), f_b2=_pad_to(row(p["hy_filt_b2"]), (1, LANES)),
        f_w3=_pad_to(p["hy_filt_w3"][l], (LANES, 2 * HY_DIM)), f_b3=row(p["hy_filt_b3"]),
        g_pool=p["grp_norm_g"][l][None, :POOL_DIM], g_mla=p["grp_norm_g"][l][None, POOL_DIM:POOL_DIM + MLA_DIM],
        g_hy=p["grp_norm_g"][l][None, POOL_DIM + MLA_DIM:],
        ffn_g=row(p["ffn_norm_g"]), ffn_cw=p["ffn_conv_w"][l], ffn_cb=row(p["ffn_conv_b"]),
    )


def _tile(n, want):
    return min(n, want)


def _layer(x2, b, seq, layer, w, stacks, kr, ki, rope, tables):
    cos_t, sin_t = rope
    m = b * seq
    q2, k2, vt, m_pool, z, x0 = _in_proj(
        x2, seq, w["attn_g"], w["w_in"], cos_t, sin_t, w["qn"], w["wq"], w["kvn"], w["wk"], w["wv"], w["qhg"],
        w["khn"], w["khr"], w["pool_w"], w["pool_scale"], w["g_pool"], w["hy_cw"], w["hy_cb"], _tile(seq, 512))
    hp = MLA_HEADS * HEAD_PAD
    m_hy = _hy_conv(tables, z, x0, kr, ki, w["hy_bias"], w["g_hy"])
    x2 = _attention_out(q2.reshape(b, seq, hp), k2.reshape(b, seq, hp), vt, w["g_mla"], x2, m_pool,
                        m_hy.reshape(m, HY_DIM), stacks["w_out"], layer, _tile(seq, 512))
    return _ffn(x2, seq, w["ffn_g"], stacks["w_up"], w["ffn_cw"], w["ffn_cb"], stacks["w_down"], layer,
                _tile(seq, 1024), 512)


def kernel(x_prompt, x_sample, attn_norm_g, w_in, pool_w, pool_scale, mla_q_norm_g, mla_w_uq, mla_kv_norm_g, mla_w_ukv, mla_q_head_norm_g, mla_k_head_norm_g, hy_conv_w, hy_conv_b, hy_filt_w1, hy_filt_b1, hy_filt_freq, hy_filt_w2, hy_filt_b2, hy_filt_w3, hy_filt_b3, hy_bias, grp_norm_g, w_out, ffn_norm_g, ffn_w_up, ffn_conv_w, ffn_conv_b, ffn_w_down):
    p = dict(attn_norm_g=attn_norm_g, w_in=w_in, pool_w=pool_w, pool_scale=pool_scale, mla_q_norm_g=mla_q_norm_g,
             mla_w_uq=mla_w_uq, mla_kv_norm_g=mla_kv_norm_g, mla_w_ukv=mla_w_ukv,
             mla_q_head_norm_g=mla_q_head_norm_g, mla_k_head_norm_g=mla_k_head_norm_g, hy_conv_w=hy_conv_w,
             hy_conv_b=hy_conv_b, hy_filt_w1=hy_filt_w1, hy_filt_b1=hy_filt_b1, hy_filt_freq=hy_filt_freq,
             hy_filt_w2=hy_filt_w2, hy_filt_b2=hy_filt_b2, hy_filt_w3=hy_filt_w3, hy_filt_b3=hy_filt_b3,
             hy_bias=hy_bias, grp_norm_g=grp_norm_g, w_out=w_out, ffn_norm_g=ffn_norm_g, ffn_w_up=ffn_w_up,
             ffn_conv_w=ffn_conv_w, ffn_conv_b=ffn_conv_b, ffn_w_down=ffn_w_down)
    depth = attn_norm_g.shape[0]
    seq = x_prompt.shape[1]
    assert x_sample.shape[1] == seq and x_prompt.shape[2] == D_MODEL
    rope = _rope_tables(seq)
    tables = _dft_tables(seq)
    feat = _filter_features(seq)
    rates = _decay_rates()
    stacks = dict(w_out=w_out.astype(BF16), w_up=ffn_w_up.astype(BF16), w_down=ffn_w_down.astype(BF16))
    xs = [x.reshape(-1, D_MODEL) for x in (x_prompt, x_sample)]
    for l in range(depth):
        w = _layer_weights(p, l)
        hs, hd = _hy_filter(feat, w["f_w1"], w["f_b1"], w["f_fr"], w["f_w2"], w["f_b2"], w["f_w3"], w["f_b3"],
                            rates, _tile(seq, 512))
        half = (2, seq // 2, HY_DIM)
        kr, ki = _hy_spec(tables, hs.reshape(half), hd.reshape(half))
        xs = [_layer(x2, x.shape[0], seq, l, w, stacks, kr, ki, rope, tables)
              for x2, x in zip(xs, (x_prompt, x_sample))]
    return (xs[0].reshape(x_prompt.shape), xs[1].reshape(x_sample.shape))
```

```python
import functools
import math

import jax
import jax.numpy as jnp
from jax import lax
from jax.experimental import pallas as pl
from jax.experimental.pallas import tpu as pltpu

F32 = jnp.float32
BF16 = jnp.bfloat16

D_MODEL = 2048
DEPTH = 4
POOL_DIM = 512
POOL_WINDOWS = (2, 4, 8, 16)
POOL_GROUP = 128
MLA_HEADS = 8
QK_NOPE = 128
QK_ROPE = 64
QK_HEAD = QK_NOPE + QK_ROPE
V_HEAD = 128
MLA_DIM = MLA_HEADS * V_HEAD
Q_LORA = 512
KV_LORA = 256
ROPE_THETA = 10000.0
HY_DIM = 512
N_BANDS = 8
N_POS_FEAT = 1 + 2 * N_BANDS
FILT_HIDDEN = 64
HY_FAST_DECAY = 0.3
HY_SLOW_DECAY = 1.5
HY_DECAY_TARGET = 1e-2
D_FF = 5632
NORM_EPS = 1e-6

LANES = 128
HEAD_PAD = 2 * LANES
ROPE_HALF = QK_ROPE // 2
HALF_CH = 2 * LANES
COL_POOL = 3 * HY_DIM
COL_CQ = COL_POOL + POOL_DIM
HALO = 16
VMEM_LIMIT = 56 * 1024 * 1024


def _params(*sem):
    return pltpu.CompilerParams(dimension_semantics=sem, vmem_limit_bytes=VMEM_LIMIT)


def _resident(shape):
    nd = len(shape)
    return pl.BlockSpec(shape, lambda *_: (0,) * nd, pipeline_mode=pl.Buffered(1))


def _mm(a, b):
    return jnp.dot(a, b, preferred_element_type=F32)


def _rms(x, g, n=None):
    n = x.shape[-1] if n is None else n
    ms = jnp.sum(x * x, axis=-1, keepdims=True) * (1.0 / n)
    return x * lax.rsqrt(ms + NORM_EPS) * g


def _rope(r, cos_t, sin_t):
    return r * cos_t + pltpu.roll(r, LANES // 2, 1) * sin_t


def _in_proj_kernel(x_ref, xp_ref, xn_ref, g_ref, w_ref, cos_ref, sin_ref, qn_ref, wq_ref, kvn_ref, wk_ref, wv_ref,
                    qhg_ref, khn_ref, khr_ref, pw_ref, ps_ref, pg_ref, cw_ref, cb_ref,
                    q_ref, k_ref, v_ref, mp_ref, z_ref, x0_ref, h_s, t_s, y_s, *, blocks_per_seq, seq):
    tm = x_ref.shape[0]
    g = g_ref[...]
    pos = pl.program_id(0) % blocks_per_seq
    h = _rms(x_ref[...], g).astype(BF16)
    h_s[HALO:HALO + tm, :] = h
    pad = jnp.zeros((HALO - 8, x_ref.shape[1]), F32)
    hp = jnp.where(pos == 0, 0.0, _rms(xp_ref[...], g))
    hn = jnp.where(pos == blocks_per_seq - 1, 0.0, _rms(xn_ref[...], g))
    h_s[0:HALO, :] = jnp.concatenate([pad, hp], axis=0).astype(BF16)
    h_s[HALO + tm:, :] = jnp.concatenate([hn, pad], axis=0).astype(BF16)
    lat = _mm(h, w_ref[:, COL_CQ:])
    cq = _rms(lat[:, :Q_LORA], qn_ref[...]).astype(BF16)
    ckv = _rms(lat[:, Q_LORA:Q_LORA + KV_LORA], kvn_ref[...]).astype(BF16)
    kr = lat[:, Q_LORA + KV_LORA:]
    qf = _mm(cq, wq_ref[...])
    kn = _mm(ckv, wk_ref[...])
    vv = _mm(ckv, wv_ref[...])
    pool_u = _mm(h_s[...], w_ref[:, COL_POOL:COL_CQ])
    hy_u = [[_mm(h_s[...], w_ref[:, sec * HY_DIM + cc * HALF_CH:sec * HY_DIM + (cc + 1) * HALF_CH])
             for sec in range(3)] for cc in range(HY_DIM // HALF_CH)]

    cos_t = cos_ref[...]
    sin_t = sin_ref[...]
    scale = QK_HEAD ** -0.5 * math.log2(math.e)
    qg = qhg_ref[...] * scale
    qg_n, qg_r = qg[:, :LANES], qg[:, LANES:]
    rope_a = qg_r * cos_t
    rope_b = pltpu.roll(qg_r, LANES // 2, 1) * sin_t
    for hd in range(MLA_HEADS):
        qh = qf[:, hd * HEAD_PAD:(hd + 1) * HEAD_PAD]
        rs = lax.rsqrt(jnp.sum(qh * qh, axis=-1, keepdims=True) * (1.0 / QK_HEAD) + NORM_EPS)
        qr = qh[:, LANES:]
        q_ref[:, hd * HEAD_PAD:hd * HEAD_PAD + LANES] = (qh[:, :LANES] * rs * qg_n).astype(BF16)
        q_ref[:, hd * HEAD_PAD + LANES:(hd + 1) * HEAD_PAD] = (
            (qr * rope_a + pltpu.roll(qr, LANES // 2, 1) * rope_b) * rs).astype(BF16)
    ones_col = (lax.broadcasted_iota(jnp.int32, (tm, LANES), 1) == 0).astype(BF16)
    for hd in range(MLA_HEADS):
        v_ref[:, hd * HEAD_PAD:hd * HEAD_PAD + V_HEAD] = vv[:, hd * V_HEAD:(hd + 1) * V_HEAD].astype(BF16)
        v_ref[:, hd * HEAD_PAD + V_HEAD:(hd + 1) * HEAD_PAD] = ones_col
    ss_r = jnp.sum(kr * kr, axis=-1, keepdims=True)
    kr_rot = _rope(kr * khr_ref[...], cos_t, sin_t)
    for hd in range(MLA_HEADS):
        knh = kn[:, hd * LANES:(hd + 1) * LANES]
        ms = (jnp.sum(knh * knh, axis=-1, keepdims=True) + ss_r) * (1.0 / QK_HEAD)
        rs = lax.rsqrt(ms + NORM_EPS)
        k_ref[:, hd * HEAD_PAD:hd * HEAD_PAD + LANES] = (knh * rs * khn_ref[...]).astype(BF16)
        k_ref[:, hd * HEAD_PAD + LANES:(hd + 1) * HEAD_PAD] = (kr_rot * rs).astype(BF16)

    ext = tm + 2 * HALO
    main = slice(HALO, HALO + tm)
    down = lambda a, n: pltpu.roll(a, n, 0)
    up = lambda a, n: pltpu.roll(a, ext - n, 0)

    tok = lax.broadcasted_iota(jnp.int32, (tm, 1), 0) + pos * tm
    for gi, win in enumerate(POOL_WINDOWS):
        half = win // 2
        cols = slice(gi * POOL_GROUP, (gi + 1) * POOL_GROUP)
        xg = pool_u[:, cols]
        ahead = xg
        behind = down(xg, 1)
        span = 1
        while span < half:
            ahead = ahead + up(ahead, span)
            behind = behind + down(behind, span)
            span *= 2
        tot = (ahead + behind)[main]
        cnt = (jnp.minimum(tok + half, seq) - jnp.maximum(tok - half, 0)).astype(F32)
        pooled = (tot / cnt - xg[main]).astype(BF16)
        y_s[:, cols] = _mm(pooled, pw_ref[gi]) * ps_ref[:, cols]
    mp_ref[...] = _rms(y_s[...], pg_ref[...]).astype(BF16)

    def conv3(xc, col):
        cols = slice(col, col + LANES)
        w = cw_ref[:, cols]
        return (down(xc, 1) * w[0:1] + xc * w[1:2] + up(xc, 1) * w[2:3])[main] + cb_ref[:, cols]

    def store_split(dst_ref, slab, cols, val):
        t_s[slab] = val
        dst_ref[0, 0, :, cols] = t_s[slab, pl.ds(0, tm // 2, stride=2), :].astype(BF16)
        dst_ref[0, 1, :, cols] = t_s[slab, pl.ds(1, tm // 2, stride=2), :].astype(BF16)

    for c in range(HY_DIM // LANES):
        cols = slice(c * LANES, (c + 1) * LANES)
        x0u, x1u, vu = (u[:, (c % 2) * LANES:(c % 2 + 1) * LANES] for u in hy_u[c // 2])
        store_split(x0_ref, 2 * c, cols, conv3(x0u, c * LANES))
        store_split(z_ref, 2 * c + 1, cols, conv3(vu, 2 * HY_DIM + c * LANES) * conv3(x1u, HY_DIM + c * LANES))


def _in_proj(x2, seq, g, w, cos_t, sin_t, qn, wq, kvn, wk, wv, qhg, khn, khr, pool_w, pool_scale, g_pool,
             hy_cw, hy_cb, tm):
    m, d = x2.shape
    nseq = seq // tm
    hp = MLA_HEADS * HEAD_PAD
    r8 = tm // 8
    last8 = m // 8 - 1
    row = lambda width: pl.BlockSpec((tm, width), lambda i: (i, 0))
    tab = pl.BlockSpec((tm, LANES), lambda i: (i % nseq, 0))
    split = pl.BlockSpec((1, 2, tm // 2, HY_DIM), lambda i: (i // nseq, 0, i % nseq, 0))
    split_shape = jax.ShapeDtypeStruct((m // seq, 2, seq // 2, HY_DIM), BF16)
    consts = (qn, wq, kvn, wk, wv, qhg, khn, khr, pool_w, pool_scale, g_pool, hy_cw, hy_cb)
    return pl.pallas_call(
        functools.partial(_in_proj_kernel, blocks_per_seq=nseq, seq=seq),
        grid=(m // tm,),
        in_specs=[row(d),
                  pl.BlockSpec((8, d), lambda i: (jnp.maximum(i * r8 - 1, 0), 0)),
                  pl.BlockSpec((8, d), lambda i: (jnp.minimum((i + 1) * r8, last8), 0)),
                  _resident((1, d)), _resident(w.shape), tab, tab] + [_resident(a.shape) for a in consts],
        out_specs=[row(hp), row(hp), row(hp), row(POOL_DIM), split, split],
        out_shape=[jax.ShapeDtypeStruct((m, hp), BF16)] * 3 +
                  [jax.ShapeDtypeStruct((m, POOL_DIM), BF16), split_shape, split_shape],
        scratch_shapes=[pltpu.VMEM((tm + 2 * HALO, d), BF16), pltpu.VMEM((2 * HY_DIM // LANES, tm, LANES), F32),
                        pltpu.VMEM((tm, POOL_DIM), F32)],
        compiler_params=_params("parallel"),
        name="in_proj",
    )(x2, x2, x2, g, w, cos_t, sin_t, *consts)


def _attn_kernel(q_ref, k_ref, v_ref, g_ref, o_ref, y_s):
    def scores(h):
        q = q_ref[0, :, h * HEAD_PAD:(h + 1) * HEAD_PAD]
        k = k_ref[0, :, h * HEAD_PAD:(h + 1) * HEAD_PAD]
        return lax.dot_general(q, k, (((1,), (1,)), ((), ())), preferred_element_type=F32)

    s_next = scores(0)
    for h in range(MLA_HEADS):
        s = s_next
        if h + 1 < MLA_HEADS:
            s_next = scores(h + 1)
        p = jnp.exp2(s - jnp.max(s, axis=-1, keepdims=True)).astype(BF16)
        o = _mm(p, v_ref[0, :, h * HEAD_PAD:(h + 1) * HEAD_PAD])
        y_s[:, h * V_HEAD:(h + 1) * V_HEAD] = o[:, :V_HEAD] / o[:, V_HEAD:V_HEAD + 1]
    o_ref[0] = _rms(y_s[...], g_ref[...]).astype(BF16)


def _attention(q3, k3, v3, g, tq):
    b, seq, hp = q3.shape
    return pl.pallas_call(
        _attn_kernel,
        grid=(b, seq // tq),
        in_specs=[pl.BlockSpec((1, tq, hp), lambda i, j: (i, j, 0)),
                  pl.BlockSpec((1, seq, hp), lambda i, j: (i, 0, 0)),
                  pl.BlockSpec((1, seq, hp), lambda i, j: (i, 0, 0)),
                  _resident((1, MLA_DIM))],
        out_specs=pl.BlockSpec((1, tq, MLA_DIM), lambda i, j: (i, j, 0)),
        out_shape=jax.ShapeDtypeStruct((b, seq, MLA_DIM), BF16),
        scratch_shapes=[pltpu.VMEM((tq, MLA_DIM), F32)],
        compiler_params=_params("parallel", "arbitrary"),
        name="attention",
    )(q3, k3, v3, g)


def _hy_filter_kernel(feat_ref, w1_ref, b1_ref, fr_ref, w2_ref, b2_ref, w3_ref, b3_ref, dl_ref, hs_ref, hd_ref):
    hi = lax.Precision.HIGHEST
    feat = feat_ref[...]
    fr = fr_ref[...]
    h = jnp.sin(fr * (jnp.dot(feat, w1_ref[...], precision=hi, preferred_element_type=F32) + b1_ref[...]))
    h = jnp.sin(fr * (jnp.dot(h, w2_ref[...], precision=hi, preferred_element_type=F32) + b2_ref[...]))
    h = jnp.dot(h, w3_ref[...], precision=hi, preferred_element_type=F32) + b3_ref[...]
    decay = jnp.exp(-feat[:, 0:1] * dl_ref[...])
    fwd = h[:, :HY_DIM] * decay
    row = lax.broadcasted_iota(jnp.int32, (feat.shape[0], 1), 0) + pl.program_id(0) * feat.shape[0]
    bwd = jnp.where(row > 0, h[:, HY_DIM:] * decay, 0.0)
    hs_ref[...] = (fwd + bwd).astype(BF16)
    hd_ref[...] = (bwd - fwd).astype(BF16)


def _hy_filter(feat, w1, b1, fr, w2, b2, w3, b3, dl, tl):
    seq = feat.shape[0]
    out = jax.ShapeDtypeStruct((seq, HY_DIM), BF16)
    return pl.pallas_call(
        _hy_filter_kernel,
        grid=(seq // tl,),
        in_specs=[pl.BlockSpec((tl, LANES), lambda i: (i, 0))] + [_resident(a.shape) for a in (w1, b1, fr, w2, b2, w3, b3, dl)],
        out_specs=[pl.BlockSpec((tl, HY_DIM), lambda i: (i, 0))] * 2,
        out_shape=[out, out],
        compiler_params=_params("parallel"),
        name="hy_filter",
    )(feat, w1, b1, fr, w2, b2, w3, b3, dl)


def _hy_spec_kernel(ce_ref, co_ref, se_ref, so_ref, hs_ref, hd_ref, kr_ref, ki_ref):
    norm = 0.5 / ce_ref.shape[1]
    ae, ao = _mm(ce_ref[...], hs_ref[0]), _mm(co_ref[...], hs_ref[1])
    be, bo = _mm(se_ref[...], hd_ref[0]), _mm(so_ref[...], hd_ref[1])
    kr_ref[0] = (ae + ao) * norm
    kr_ref[1] = (ae - ao) * norm
    ki_ref[0] = (be + bo) * norm
    ki_ref[1] = (bo - be) * norm


def _hy_spec(tables, hs, hd):
    half = hs.shape[1]
    blk = pl.BlockSpec((2, half, HALF_CH), lambda c: (0, 0, c))
    out = jax.ShapeDtypeStruct((2, half, HY_DIM), F32)
    return pl.pallas_call(
        _hy_spec_kernel,
        grid=(HY_DIM // HALF_CH,),
        in_specs=[_resident(t.shape) for t in tables[:4]] + [blk, blk],
        out_specs=[blk, blk],
        out_shape=[out, out],
        compiler_params=_params("parallel"),
        name="hy_spec",
    )(*tables[:4], hs, hd)


def _hy_conv_kernel(z_ref, x0_ref, ce_ref, co_ref, se_ref, so_ref, cet_ref, cot_ref, set_ref, sot_ref,
                    kr_ref, ki_ref, b_ref, g_ref, o_ref, y_s):
    c = pl.program_id(1)
    half = z_ref.shape[2]
    ze, zo = z_ref[0, 0], z_ref[0, 1]
    ae, ao = _mm(ce_ref[...], ze), _mm(co_ref[...], zo)
    be, bo = _mm(se_ref[...], ze), _mm(so_ref[...], zo)
    u_lo, u_hi, v_lo, v_hi = ae + ao, ae - ao, be + bo, bo - be
    kr_lo, kr_hi, ki_lo, ki_hi = kr_ref[0], kr_ref[1], ki_ref[0], ki_ref[1]
    yr_lo, w_lo = u_lo * kr_lo + v_lo * ki_lo, v_lo * kr_lo - u_lo * ki_lo
    yr_hi, w_hi = u_hi * kr_hi + v_hi * ki_hi, v_hi * kr_hi - u_hi * ki_hi
    ye = _mm(cet_ref[...], (yr_lo + yr_hi).astype(BF16)) + _mm(set_ref[...], (w_lo - w_hi).astype(BF16))
    yo = _mm(cot_ref[...], (yr_lo - yr_hi).astype(BF16)) + _mm(sot_ref[...], (w_lo + w_hi).astype(BF16))
    bias = b_ref[...]
    ye = (ye + ze.astype(F32) * bias) * x0_ref[0, 0].astype(F32)
    yo = (yo + zo.astype(F32) * bias) * x0_ref[0, 1].astype(F32)
    per = HALF_CH // LANES
    for k in range(per):
        y_s[c * per + k, pl.ds(0, half, stride=2), :] = ye[:, k * LANES:(k + 1) * LANES]
        y_s[c * per + k, pl.ds(1, half, stride=2), :] = yo[:, k * LANES:(k + 1) * LANES]

    @pl.when(c == pl.num_programs(1) - 1)
    def _():
        slabs = [y_s[k] for k in range(HY_DIM // LANES)]
        ms = sum(jnp.sum(y * y, axis=-1, keepdims=True) for y in slabs) * (1.0 / HY_DIM)
        rs = lax.rsqrt(ms + NORM_EPS)
        for k, y in enumerate(slabs):
            cols = slice(k * LANES, (k + 1) * LANES)
            o_ref[0, :, cols] = (y * rs * g_ref[:, cols]).astype(BF16)


def _hy_conv(tables, z, x0, kr, ki, bias, g):
    b, _, half, _ = z.shape
    data = pl.BlockSpec((1, 2, half, HALF_CH), lambda i, c: (i, 0, 0, c))
    spec = pl.BlockSpec((2, half, HALF_CH), lambda i, c: (0, 0, c))
    return pl.pallas_call(
        _hy_conv_kernel,
        grid=(b, HY_DIM // HALF_CH),
        in_specs=[data, data] + [_resident(t.shape) for t in tables] + [spec, spec,
                  pl.BlockSpec((1, HALF_CH), lambda i, c: (0, c)), _resident((1, HY_DIM))],
        out_specs=pl.BlockSpec((1, 2 * half, HY_DIM), lambda i, c: (i, 0, 0)),
        out_shape=jax.ShapeDtypeStruct((b, 2 * half, HY_DIM), BF16),
        scratch_shapes=[pltpu.VMEM((HY_DIM // LANES, 2 * half, LANES), F32)],
        compiler_params=_params("parallel", "arbitrary"),
        name="hy_conv",
    )(z, x0, *tables, kr, ki, bias, g)


def _out_proj_kernel(x_ref, a_ref, b_ref, c_ref, w_ref, o_ref):
    acc = jnp.dot(a_ref[...], w_ref[0:POOL_DIM, :], preferred_element_type=F32)
    acc = acc + jnp.dot(b_ref[...], w_ref[POOL_DIM:POOL_DIM + MLA_DIM, :], preferred_element_type=F32)
    acc = acc + jnp.dot(c_ref[...], w_ref[POOL_DIM + MLA_DIM:, :], preferred_element_type=F32)
    o_ref[...] = x_ref[...] + acc


def _out_proj(x2, ma, mb, mc, w_all, layer, tm):
    m, d = x2.shape
    row = lambda width: pl.BlockSpec((tm, width), lambda i: (i, 0))
    w_spec = pl.BlockSpec((None,) + w_all.shape[1:], lambda i: (layer, 0, 0), pipeline_mode=pl.Buffered(1))
    return pl.pallas_call(
        _out_proj_kernel,
        grid=(m // tm,),
        in_specs=[row(d), row(POOL_DIM), row(MLA_DIM), row(HY_DIM), w_spec],
        out_specs=row(d),
        out_shape=jax.ShapeDtypeStruct((m, d), F32),
        compiler_params=_params("parallel"),
        name="out_proj",
    )(x2, ma, mb, mc, w_all)


def _ffn_kernel(x_ref, xp_ref, xn_ref, g_ref, wg_ref, wv_ref, cw_ref, cb_ref, wd_ref, o_ref, h_s, up_s, *, blocks_per_seq):
    tm = x_ref.shape[0]
    f = pl.program_id(1)

    @pl.when(f == 0)
    def _():
        x = x_ref[...]
        g = g_ref[...]
        o_ref[...] = x
        h_s[HALO:HALO + tm, :] = _rms(x, g).astype(BF16)
        pos = pl.program_id(0) % blocks_per_seq
        pad = jnp.zeros((HALO - 8, x.shape[1]), F32)
        hp = jnp.where(pos == 0, 0.0, _rms(xp_ref[...], g))
        hn = jnp.where(pos == blocks_per_seq - 1, 0.0, _rms(xn_ref[...], g))
        h_s[0:HALO, :] = jnp.concatenate([pad, hp], axis=0).astype(BF16)
        h_s[HALO + tm:, :] = jnp.concatenate([hn, pad], axis=0).astype(BF16)

    up_s[...] = jnp.dot(h_s[...], wg_ref[...], preferred_element_type=F32)
    val = jnp.dot(h_s[HALO:HALO + tm, :], wv_ref[...], preferred_element_type=F32)
    cw = cw_ref[...]
    gate = (up_s[HALO - 1:HALO - 1 + tm, :] * cw[0:1] + up_s[HALO:HALO + tm, :] * cw[1:2]
            + up_s[HALO + 1:HALO + 1 + tm, :] * cw[2:3] + cb_ref[...])
    act = (gate * jax.nn.sigmoid(gate) * val).astype(BF16)
    o_ref[...] += jnp.dot(act, wd_ref[...], preferred_element_type=F32)


def _ffn(x2, seq, g, w_up, cw, cb, w_down, layer, tm, tf):
    m, d = x2.shape
    nf = D_FF // tf
    r8 = tm // 8
    last8 = m // 8 - 1
    kern = functools.partial(_ffn_kernel, blocks_per_seq=seq // tm)
    return pl.pallas_call(
        kern,
        grid=(m // tm, nf),
        in_specs=[pl.BlockSpec((tm, d), lambda i, f: (i, 0)),
                  pl.BlockSpec((8, d), lambda i, f: (jnp.maximum(i * r8 - 1, 0), 0)),
                  pl.BlockSpec((8, d), lambda i, f: (jnp.minimum((i + 1) * r8, last8), 0)),
                  _resident((1, d)),
                  pl.BlockSpec((None, d, tf), lambda i, f: (layer, 0, f)),
                  pl.BlockSpec((None, d, tf), lambda i, f: (layer, 0, f + nf)),
                  pl.BlockSpec((3, tf), lambda i, f: (0, f)),
                  pl.BlockSpec((1, tf), lambda i, f: (0, f)),
                  pl.BlockSpec((None, tf, d), lambda i, f: (layer, f, 0))],
        out_specs=pl.BlockSpec((tm, d), lambda i, f: (i, 0)),
        out_shape=jax.ShapeDtypeStruct((m, d), F32),
        scratch_shapes=[pltpu.VMEM((tm + 2 * HALO, d), BF16), pltpu.VMEM((tm + 2 * HALO, tf), F32)],
        compiler_params=_params("parallel", "arbitrary"),
        name="ffn",
    )(x2, x2, x2, g, w_up, w_up, cw, cb, w_down)


def _rope_tables(seq):
    freqs = ROPE_THETA ** (-jnp.arange(0, QK_ROPE, 2, dtype=F32) / QK_ROPE)
    ang = jnp.arange(seq, dtype=F32)[:, None] * freqs[None, :]
    cos, sin, zero = jnp.cos(ang), jnp.sin(ang), jnp.zeros((seq, ROPE_HALF), F32)
    return (jnp.concatenate([cos, zero, cos, zero], axis=1), jnp.concatenate([-sin, zero, sin, zero], axis=1))


def _dft_tables(seq):
    half = seq // 2
    f = jnp.arange(half, dtype=jnp.int32)[:, None]
    j = jnp.arange(half, dtype=jnp.int32)[None, :]
    out = []
    for tok in (2 * j, 2 * j + 1):
        ang = (((2 * f + 1) * tok) % (4 * seq)).astype(F32) * (math.pi / (2 * seq))
        out.append((jnp.cos(ang).astype(BF16), jnp.sin(ang).astype(BF16)))
    (ce, se), (co, so) = out
    return ce, co, se, so, ce.T, co.T, se.T, so.T


def _filter_features(seq):
    t = jnp.linspace(0.0, 1.0, seq, dtype=F32)[:, None]
    w = 2.0 * math.pi * jnp.arange(seq, dtype=F32)[:, None] / seq
    bands = jnp.linspace(1e-4, N_BANDS - 1, N_BANDS, dtype=F32)[None, :]
    z = jnp.concatenate([t, jnp.cos(bands * w), -jnp.sin(bands * w)], axis=-1)
    z = jnp.concatenate([z[0::2], z[1::2]], axis=0)
    return jnp.pad(z, ((0, 0), (0, LANES - N_POS_FEAT)))


def _decay_rates():
    max_decay = math.log(HY_DECAY_TARGET) / HY_FAST_DECAY
    min_decay = math.log(HY_DECAY_TARGET) / HY_SLOW_DECAY
    return jnp.abs(jnp.linspace(min_decay, max_decay, HY_DIM, dtype=F32))[None, :]


def _spread_rope(a):
    zero = jnp.zeros(a.shape[:-1] + (ROPE_HALF,), a.dtype)
    return jnp.concatenate([a[..., :ROPE_HALF], zero, a[..., ROPE_HALF:], zero], axis=-1)


def _pad_to(a, shape):
    return jnp.pad(a, [(0, t - s) for s, t in zip(a.shape, shape)])


def _layer_weights(p, l):
    row = lambda a: a[l][None, :].astype(F32)
    w_in = p["w_in"][l]
    o1, o2, o3 = POOL_DIM + Q_LORA, POOL_DIM + Q_LORA + KV_LORA, POOL_DIM + Q_LORA + KV_LORA + QK_ROPE
    w_in_p = jnp.concatenate([w_in[:, o3:], w_in[:, :o2], _spread_rope(w_in[:, o2:o3])], axis=1).astype(BF16)
    wq = p["mla_w_uq"][l].reshape(Q_LORA, MLA_HEADS, QK_HEAD)
    wq_p = jnp.concatenate([wq[..., :QK_NOPE], _spread_rope(wq[..., QK_NOPE:])], axis=-1)
    wkv = p["mla_w_ukv"][l].reshape(KV_LORA, MLA_HEADS, QK_NOPE + V_HEAD)
    qhg, khg = p["mla_q_head_norm_g"][l], p["mla_k_head_norm_g"][l]
    hid = (LANES, LANES)
    return dict(
        attn_g=row(p["attn_norm_g"]), w_in=w_in_p,
        pool_w=p["pool_w"][l].astype(BF16), pool_scale=row(p["pool_scale"]),
        qn=row(p["mla_q_norm_g"]), wq=wq_p.reshape(Q_LORA, MLA_HEADS * HEAD_PAD).astype(BF16),
        kvn=row(p["mla_kv_norm_g"]),
        wk=wkv[..., :QK_NOPE].reshape(KV_LORA, MLA_HEADS * QK_NOPE).astype(BF16),
        wv=wkv[..., QK_NOPE:].reshape(KV_LORA, MLA_DIM).astype(BF16),
        qhg=jnp.concatenate([qhg[:QK_NOPE], _spread_rope(qhg[QK_NOPE:])])[None, :],
        khn=khg[None, :QK_NOPE], khr=_spread_rope(khg[QK_NOPE:])[None, :],
        hy_cw=p["hy_conv_w"][l], hy_cb=row(p["hy_conv_b"]), hy_bias=row(p["hy_bias"]),
        f_w1=_pad_to(p["hy_filt_w1"][l], hid), f_b1=_pad_to(row(p["hy_filt_b1"]), (1, LANES)),
        f_fr=_pad_to(row(p["hy_filt_freq"]), (1, LANES)),
        f_w2=_pad_to(p["hy_filt_w2"][l], hid), f_b2=_pad_to(row(p["hy_filt_b2"]), (1, LANES)),
        f_w3=_pad_to(p["hy_filt_w3"][l], (LANES, 2 * HY_DIM)), f_b3=row(p["hy_filt_b3"]),
        g_pool=p["grp_norm_g"][l][None, :POOL_DIM], g_mla=p["grp_norm_g"][l][None, POOL_DIM:POOL_DIM + MLA_DIM],
        g_hy=p["grp_norm_g"][l][None, POOL_DIM + MLA_DIM:],
        ffn_g=row(p["ffn_norm_g"]), ffn_cw=p["ffn_conv_w"][l], ffn_cb=row(p["ffn_conv_b"]),
    )


def _tile(n, want):
    return min(n, want)


def _layer(x2, b, seq, layer, w, stacks, kr, ki, rope, tables):
    cos_t, sin_t = rope
    m = b * seq
    q2, k2, v2, m_pool, z, x0 = _in_proj(
        x2, seq, w["attn_g"], w["w_in"], cos_t, sin_t, w["qn"], w["wq"], w["kvn"], w["wk"], w["wv"], w["qhg"],
        w["khn"], w["khr"], w["pool_w"], w["pool_scale"], w["g_pool"], w["hy_cw"], w["hy_cb"], _tile(seq, 512))
    hp = MLA_HEADS * HEAD_PAD
    m_mla = _attention(q2.reshape(b, seq, hp), k2.reshape(b, seq, hp), v2.reshape(b, seq, hp),
                       w["g_mla"], _tile(seq, 256))
    m_hy = _hy_conv(tables, z, x0, kr, ki, w["hy_bias"], w["g_hy"])
    x2 = _out_proj(x2, m_pool.reshape(m, POOL_DIM), m_mla.reshape(m, MLA_DIM), m_hy.reshape(m, HY_DIM),
                   stacks["w_out"], layer, _tile(seq, 512))
    return _ffn(x2, seq, w["ffn_g"], stacks["w_up"], w["ffn_cw"], w["ffn_cb"], stacks["w_down"], layer,
                _tile(seq, 1024), 512)


def kernel(x_prompt, x_sample, attn_norm_g, w_in, pool_w, pool_scale, mla_q_norm_g, mla_w_uq, mla_kv_norm_g, mla_w_ukv, mla_q_head_norm_g, mla_k_head_norm_g, hy_conv_w, hy_conv_b, hy_filt_w1, hy_filt_b1, hy_filt_freq, hy_filt_w2, hy_filt_b2, hy_filt_w3, hy_filt_b3, hy_bias, grp_norm_g, w_out, ffn_norm_g, ffn_w_up, ffn_conv_w, ffn_conv_b, ffn_w_down):
    p = dict(attn_norm_g=attn_norm_g, w_in=w_in, pool_w=pool_w, pool_scale=pool_scale, mla_q_norm_g=mla_q_norm_g,
             mla_w_uq=mla_w_uq, mla_kv_norm_g=mla_kv_norm_g, mla_w_ukv=mla_w_ukv,
             mla_q_head_norm_g=mla_q_head_norm_g, mla_k_head_norm_g=mla_k_head_norm_g, hy_conv_w=hy_conv_w,
             hy_conv_b=hy_conv_b, hy_filt_w1=hy_filt_w1, hy_filt_b1=hy_filt_b1, hy_filt_freq=hy_filt_freq,
             hy_filt_w2=hy_filt_w2, hy_filt_b2=hy_filt_b2, hy_filt_w3=hy_filt_w3, hy_filt_b3=hy_filt_b3,
             hy_bias=hy_bias, grp_norm_g=grp_norm_g, w_out=w_out, ffn_norm_g=ffn_norm_g, ffn_w_up=ffn_w_up,
             ffn_conv_w=ffn_conv_w, ffn_conv_b=ffn_conv_b, ffn_w_down=ffn_w_down)
    depth = attn_norm_g.shape[0]
    seq = x_prompt.shape[1]
    assert x_sample.shape[1] == seq and x_prompt.shape[2] == D_MODEL
    rope = _rope_tables(seq)
    tables = _dft_tables(seq)
    feat = _filter_features(seq)
    rates = _decay_rates()
    stacks = dict(w_out=w_out.astype(BF16), w_up=ffn_w_up.astype(BF16), w_down=ffn_w_down.astype(BF16))
    xs = [x.reshape(-1, D_MODEL) for x in (x_prompt, x_sample)]
    for l in range(depth):
        w = _layer_weights(p, l)
        hs, hd = _hy_filter(feat, w["f_w1"], w["f_b1"], w["f_fr"], w["f_w2"], w["f_b2"], w["f_w3"], w["f_b3"],
                            rates, _tile(seq, 512))
        half = (2, seq // 2, HY_DIM)
        kr, ki = _hy_spec(tables, hs.reshape(half), hd.reshape(half))
        xs = [_layer(x2, x.shape[0], seq, l, w, stacks, kr, ki, rope, tables)
              for x2, x in zip(xs, (x_prompt, x_sample))]
    return (xs[0].reshape(x_prompt.shape), xs[1].reshape(x_sample.shape))
```

```python
import functools
import math

import jax
import jax.numpy as jnp
from jax import lax
from jax.experimental import pallas as pl
from jax.experimental.pallas import tpu as pltpu

F32 = jnp.float32
BF16 = jnp.bfloat16

D_MODEL = 2048
DEPTH = 4
POOL_DIM = 512
POOL_WINDOWS = (2, 4, 8, 16)
POOL_GROUP = 128
MLA_HEADS = 8
QK_NOPE = 128
QK_ROPE = 64
QK_HEAD = QK_NOPE + QK_ROPE
V_HEAD = 128
MLA_DIM = MLA_HEADS * V_HEAD
Q_LORA = 512
KV_LORA = 256
ROPE_THETA = 10000.0
HY_DIM = 512
N_BANDS = 8
N_POS_FEAT = 1 + 2 * N_BANDS
FILT_HIDDEN = 64
HY_FAST_DECAY = 0.3
HY_SLOW_DECAY = 1.5
HY_DECAY_TARGET = 1e-2
D_FF = 5632
NORM_EPS = 1e-6

LANES = 128
HEAD_PAD = 2 * LANES
ROPE_HALF = QK_ROPE // 2
HALF_CH = 2 * LANES
COL_POOL = 3 * HY_DIM
COL_CQ = COL_POOL + POOL_DIM
HALO = 16
FFN_PARTS = 2
VMEM_LIMIT = 56 * 1024 * 1024


def _params(*sem):
    return pltpu.CompilerParams(dimension_semantics=sem, vmem_limit_bytes=VMEM_LIMIT)


def _resident(shape):
    nd = len(shape)
    return pl.BlockSpec(shape, lambda *_: (0,) * nd, pipeline_mode=pl.Buffered(1))


def _mm(a, b):
    return jnp.dot(a, b, preferred_element_type=F32)


def _rms(x, g, n=None):
    n = x.shape[-1] if n is None else n
    ms = jnp.sum(x * x, axis=-1, keepdims=True) * (1.0 / n)
    return x * lax.rsqrt(ms + NORM_EPS) * g


def _rope(r, cos_t, sin_t):
    return r * cos_t + pltpu.roll(r, LANES // 2, 1) * sin_t


def _in_proj_kernel(x_ref, xp_ref, xn_ref, g_ref, w_ref, cos_ref, sin_ref, qn_ref, wq_ref, kvn_ref, wk_ref, wv_ref,
                    qhg_ref, khn_ref, khr_ref, pw_ref, ps_ref, pg_ref, cw_ref, cb_ref,
                    q_ref, k_ref, v_ref, mp_ref, z_ref, x0_ref, h_s, t_s, y_s, *, blocks_per_seq, seq):
    tm = x_ref.shape[0]
    g = g_ref[...]
    pos = pl.program_id(0) % blocks_per_seq
    h = _rms(x_ref[...], g).astype(BF16)
    h_s[HALO:HALO + tm, :] = h
    pad = jnp.zeros((HALO - 8, x_ref.shape[1]), F32)
    hp = jnp.where(pos == 0, 0.0, _rms(xp_ref[...], g))
    hn = jnp.where(pos == blocks_per_seq - 1, 0.0, _rms(xn_ref[...], g))
    h_s[0:HALO, :] = jnp.concatenate([pad, hp], axis=0).astype(BF16)
    h_s[HALO + tm:, :] = jnp.concatenate([hn, pad], axis=0).astype(BF16)
    lat = _mm(h, w_ref[:, COL_CQ:])
    cq = _rms(lat[:, :Q_LORA], qn_ref[...]).astype(BF16)
    ckv = _rms(lat[:, Q_LORA:Q_LORA + KV_LORA], kvn_ref[...]).astype(BF16)
    kr = lat[:, Q_LORA + KV_LORA:]
    qf = _mm(cq, wq_ref[...])
    kn = _mm(ckv, wk_ref[...])
    v_ref[...] = _mm(ckv, wv_ref[...]).astype(BF16)
    pool_u = _mm(h_s[...], w_ref[:, COL_POOL:COL_CQ])
    hy_u = [[_mm(h_s[...], w_ref[:, sec * HY_DIM + cc * HALF_CH:sec * HY_DIM + (cc + 1) * HALF_CH])
             for sec in range(3)] for cc in range(HY_DIM // HALF_CH)]

    cos_t = cos_ref[...]
    sin_t = sin_ref[...]
    scale = QK_HEAD ** -0.5 * math.log2(math.e)
    qg = qhg_ref[...] * scale
    qg_n, qg_r = qg[:, :LANES], qg[:, LANES:]
    rope_a = qg_r * cos_t
    rope_b = pltpu.roll(qg_r, LANES // 2, 1) * sin_t
    for hd in range(MLA_HEADS):
        qh = qf[:, hd * HEAD_PAD:(hd + 1) * HEAD_PAD]
        rs = lax.rsqrt(jnp.sum(qh * qh, axis=-1, keepdims=True) * (1.0 / QK_HEAD) + NORM_EPS)
        qr = qh[:, LANES:]
        q_ref[:, hd * HEAD_PAD:hd * HEAD_PAD + LANES] = (qh[:, :LANES] * rs * qg_n).astype(BF16)
        q_ref[:, hd * HEAD_PAD + LANES:(hd + 1) * HEAD_PAD] = (
            (qr * rope_a + pltpu.roll(qr, LANES // 2, 1) * rope_b) * rs).astype(BF16)
    ss_r = jnp.sum(kr * kr, axis=-1, keepdims=True)
    kr_rot = _rope(kr * khr_ref[...], cos_t, sin_t)
    for hd in range(MLA_HEADS):
        knh = kn[:, hd * LANES:(hd + 1) * LANES]
        ms = (jnp.sum(knh * knh, axis=-1, keepdims=True) + ss_r) * (1.0 / QK_HEAD)
        rs = lax.rsqrt(ms + NORM_EPS)
        k_ref[:, hd * HEAD_PAD:hd * HEAD_PAD + LANES] = (knh * rs * khn_ref[...]).astype(BF16)
        k_ref[:, hd * HEAD_PAD + LANES:(hd + 1) * HEAD_PAD] = (kr_rot * rs).astype(BF16)

    ext = tm + 2 * HALO
    main = slice(HALO, HALO + tm)
    down = lambda a, n: pltpu.roll(a, n, 0)
    up = lambda a, n: pltpu.roll(a, ext - n, 0)

    tok = lax.broadcasted_iota(jnp.int32, (tm, 1), 0) + pos * tm
    for gi, win in enumerate(POOL_WINDOWS):
        half = win // 2
        cols = slice(gi * POOL_GROUP, (gi + 1) * POOL_GROUP)
        xg = pool_u[:, cols]
        ahead = xg
        behind = down(xg, 1)
        span = 1
        while span < half:
            ahead = ahead + up(ahead, span)
            behind = behind + down(behind, span)
            span *= 2
        tot = (ahead + behind)[main]
        cnt = (jnp.minimum(tok + half, seq) - jnp.maximum(tok - half, 0)).astype(F32)
        pooled = (tot / cnt - xg[main]).astype(BF16)
        y_s[:, cols] = _mm(pooled, pw_ref[gi]) * ps_ref[:, cols]
    mp_ref[...] = _rms(y_s[...], pg_ref[...]).astype(BF16)

    def conv3(xc, col):
        cols = slice(col, col + LANES)
        w = cw_ref[:, cols]
        return (down(xc, 1) * w[0:1] + xc * w[1:2] + up(xc, 1) * w[2:3])[main] + cb_ref[:, cols]

    def store_split(dst_ref, slab, cols, val):
        t_s[slab] = val
        dst_ref[0, 0, :, cols] = t_s[slab, pl.ds(0, tm // 2, stride=2), :].astype(BF16)
        dst_ref[0, 1, :, cols] = t_s[slab, pl.ds(1, tm // 2, stride=2), :].astype(BF16)

    for c in range(HY_DIM // LANES):
        cols = slice(c * LANES, (c + 1) * LANES)
        x0u, x1u, vu = (u[:, (c % 2) * LANES:(c % 2 + 1) * LANES] for u in hy_u[c // 2])
        store_split(x0_ref, 2 * c, cols, conv3(x0u, c * LANES))
        store_split(z_ref, 2 * c + 1, cols, conv3(vu, 2 * HY_DIM + c * LANES) * conv3(x1u, HY_DIM + c * LANES))


def _in_proj(x2, seq, g, w, cos_t, sin_t, qn, wq, kvn, wk, wv, qhg, khn, khr, pool_w, pool_scale, g_pool,
             hy_cw, hy_cb, tm):
    m, d = x2.shape
    nseq = seq // tm
    hp = MLA_HEADS * HEAD_PAD
    r8 = tm // 8
    last8 = m // 8 - 1
    row = lambda width: pl.BlockSpec((tm, width), lambda i: (i, 0))
    tab = pl.BlockSpec((tm, LANES), lambda i: (i % nseq, 0))
    split = pl.BlockSpec((1, 2, tm // 2, HY_DIM), lambda i: (i // nseq, 0, i % nseq, 0))
    split_shape = jax.ShapeDtypeStruct((m // seq, 2, seq // 2, HY_DIM), BF16)
    consts = (qn, wq, kvn, wk, wv, qhg, khn, khr, pool_w, pool_scale, g_pool, hy_cw, hy_cb)
    return pl.pallas_call(
        functools.partial(_in_proj_kernel, blocks_per_seq=nseq, seq=seq),
        grid=(m // tm,),
        in_specs=[row(d),
                  pl.BlockSpec((8, d), lambda i: (jnp.maximum(i * r8 - 1, 0), 0)),
                  pl.BlockSpec((8, d), lambda i: (jnp.minimum((i + 1) * r8, last8), 0)),
                  _resident((1, d)), _resident(w.shape), tab, tab] + [_resident(a.shape) for a in consts],
        out_specs=[row(hp), row(hp), row(MLA_DIM), row(POOL_DIM), split, split],
        out_shape=[jax.ShapeDtypeStruct((m, hp), BF16), jax.ShapeDtypeStruct((m, hp), BF16),
                   jax.ShapeDtypeStruct((m, MLA_DIM), BF16), jax.ShapeDtypeStruct((m, POOL_DIM), BF16),
                   split_shape, split_shape],
        scratch_shapes=[pltpu.VMEM((tm + 2 * HALO, d), BF16), pltpu.VMEM((2 * HY_DIM // LANES, tm, LANES), F32),
                        pltpu.VMEM((tm, POOL_DIM), F32)],
        compiler_params=_params("parallel"),
        name="in_proj",
    )(x2, x2, x2, g, w, cos_t, sin_t, *consts)


def _attn_kernel(q_ref, k_ref, v_ref, g_ref, o_ref, y_s, v_s):
    @pl.when(pl.program_id(1) == 0)
    def _():
        ones_col = (lax.broadcasted_iota(jnp.int32, (v_s.shape[0], LANES), 1) == 0).astype(BF16)
        for h in range(MLA_HEADS):
            v_s[:, h * HEAD_PAD:h * HEAD_PAD + V_HEAD] = v_ref[0, :, h * V_HEAD:(h + 1) * V_HEAD]
            v_s[:, h * HEAD_PAD + V_HEAD:(h + 1) * HEAD_PAD] = ones_col

    def scores(h):
        q = q_ref[0, :, h * HEAD_PAD:(h + 1) * HEAD_PAD]
        k = k_ref[0, :, h * HEAD_PAD:(h + 1) * HEAD_PAD]
        return lax.dot_general(q, k, (((1,), (1,)), ((), ())), preferred_element_type=F32)

    s_next = scores(0)
    for h in range(MLA_HEADS):
        s = s_next
        if h + 1 < MLA_HEADS:
            s_next = scores(h + 1)
        p = jnp.exp2(s - jnp.max(s, axis=-1, keepdims=True)).astype(BF16)
        o = _mm(p, v_s[:, h * HEAD_PAD:(h + 1) * HEAD_PAD])
        y_s[:, h * V_HEAD:(h + 1) * V_HEAD] = o[:, :V_HEAD] / o[:, V_HEAD:V_HEAD + 1]
    o_ref[0] = _rms(y_s[...], g_ref[...]).astype(BF16)


def _attention(q3, k3, v3, g, tq):
    b, seq, hp = q3.shape
    return pl.pallas_call(
        _attn_kernel,
        grid=(b, seq // tq),
        in_specs=[pl.BlockSpec((1, tq, hp), lambda i, j: (i, j, 0)),
                  pl.BlockSpec((1, seq, hp), lambda i, j: (i, 0, 0)),
                  pl.BlockSpec((1, seq, MLA_DIM), lambda i, j: (i, 0, 0)),
                  _resident((1, MLA_DIM))],
        out_specs=pl.BlockSpec((1, tq, MLA_DIM), lambda i, j: (i, j, 0)),
        out_shape=jax.ShapeDtypeStruct((b, seq, MLA_DIM), BF16),
        scratch_shapes=[pltpu.VMEM((tq, MLA_DIM), F32), pltpu.VMEM((seq, hp), BF16)],
        compiler_params=_params("parallel", "arbitrary"),
        name="attention",
    )(q3, k3, v3, g)


def _hy_filter_kernel(feat_ref, w1_ref, b1_ref, fr_ref, w2_ref, b2_ref, w3_ref, b3_ref, dl_ref, hs_ref, hd_ref):
    hi = lax.Precision.HIGHEST
    feat = feat_ref[...]
    fr = fr_ref[...]
    h = jnp.sin(fr * (jnp.dot(feat, w1_ref[...], precision=hi, preferred_element_type=F32) + b1_ref[...]))
    h = jnp.sin(fr * (jnp.dot(h, w2_ref[...], precision=hi, preferred_element_type=F32) + b2_ref[...]))
    h = jnp.dot(h, w3_ref[...], precision=hi, preferred_element_type=F32) + b3_ref[...]
    decay = jnp.exp(-feat[:, 0:1] * dl_ref[...])
    fwd = h[:, :HY_DIM] * decay
    row = lax.broadcasted_iota(jnp.int32, (feat.shape[0], 1), 0) + pl.program_id(0) * feat.shape[0]
    bwd = jnp.where(row > 0, h[:, HY_DIM:] * decay, 0.0)
    hs_ref[...] = (fwd + bwd).astype(BF16)
    hd_ref[...] = (bwd - fwd).astype(BF16)


def _hy_filter(feat, w1, b1, fr, w2, b2, w3, b3, dl, tl):
    seq = feat.shape[0]
    out = jax.ShapeDtypeStruct((seq, HY_DIM), BF16)
    return pl.pallas_call(
        _hy_filter_kernel,
        grid=(seq // tl,),
        in_specs=[pl.BlockSpec((tl, LANES), lambda i: (i, 0))] + [_resident(a.shape) for a in (w1, b1, fr, w2, b2, w3, b3, dl)],
        out_specs=[pl.BlockSpec((tl, HY_DIM), lambda i: (i, 0))] * 2,
        out_shape=[out, out],
        compiler_params=_params("parallel"),
        name="hy_filter",
    )(feat, w1, b1, fr, w2, b2, w3, b3, dl)


def _hy_spec_kernel(ce_ref, co_ref, se_ref, so_ref, hs_ref, hd_ref, kr_ref, ki_ref):
    norm = 0.5 / ce_ref.shape[1]
    ae, ao = _mm(ce_ref[...], hs_ref[0]), _mm(co_ref[...], hs_ref[1])
    be, bo = _mm(se_ref[...], hd_ref[0]), _mm(so_ref[...], hd_ref[1])
    kr_ref[0] = (ae + ao) * norm
    kr_ref[1] = (ae - ao) * norm
    ki_ref[0] = (be + bo) * norm
    ki_ref[1] = (bo - be) * norm


def _hy_spec(tables, hs, hd):
    half = hs.shape[1]
    blk = pl.BlockSpec((2, half, HALF_CH), lambda c: (0, 0, c))
    out = jax.ShapeDtypeStruct((2, half, HY_DIM), F32)
    return pl.pallas_call(
        _hy_spec_kernel,
        grid=(HY_DIM // HALF_CH,),
        in_specs=[_resident(t.shape) for t in tables[:4]] + [blk, blk],
        out_specs=[blk, blk],
        out_shape=[out, out],
        compiler_params=_params("parallel"),
        name="hy_spec",
    )(*tables[:4], hs, hd)


def _hy_conv_kernel(z_ref, x0_ref, ce_ref, co_ref, se_ref, so_ref, cet_ref, cot_ref, set_ref, sot_ref,
                    kr_ref, ki_ref, b_ref, g_ref, o_ref, y_s):
    c = pl.program_id(1)
    half = z_ref.shape[2]
    ze, zo = z_ref[0, 0], z_ref[0, 1]
    ae, ao = _mm(ce_ref[...], ze), _mm(co_ref[...], zo)
    be, bo = _mm(se_ref[...], ze), _mm(so_ref[...], zo)
    u_lo, u_hi, v_lo, v_hi = ae + ao, ae - ao, be + bo, bo - be
    kr_lo, kr_hi, ki_lo, ki_hi = kr_ref[0], kr_ref[1], ki_ref[0], ki_ref[1]
    yr_lo, w_lo = u_lo * kr_lo + v_lo * ki_lo, v_lo * kr_lo - u_lo * ki_lo
    yr_hi, w_hi = u_hi * kr_hi + v_hi * ki_hi, v_hi * kr_hi - u_hi * ki_hi
    ye = _mm(cet_ref[...], (yr_lo + yr_hi).astype(BF16)) + _mm(set_ref[...], (w_lo - w_hi).astype(BF16))
    yo = _mm(cot_ref[...], (yr_lo - yr_hi).astype(BF16)) + _mm(sot_ref[...], (w_lo + w_hi).astype(BF16))
    bias = b_ref[...]
    ye = (ye + ze.astype(F32) * bias) * x0_ref[0, 0].astype(F32)
    yo = (yo + zo.astype(F32) * bias) * x0_ref[0, 1].astype(F32)
    per = HALF_CH // LANES
    for k in range(per):
        y_s[c * per + k, pl.ds(0, half, stride=2), :] = ye[:, k * LANES:(k + 1) * LANES]
        y_s[c * per + k, pl.ds(1, half, stride=2), :] = yo[:, k * LANES:(k + 1) * LANES]

    @pl.when(c == pl.num_programs(1) - 1)
    def _():
        slabs = [y_s[k] for k in range(HY_DIM // LANES)]
        ms = sum(jnp.sum(y * y, axis=-1, keepdims=True) for y in slabs) * (1.0 / HY_DIM)
        rs = lax.rsqrt(ms + NORM_EPS)
        for k, y in enumerate(slabs):
            cols = slice(k * LANES, (k + 1) * LANES)
            o_ref[0, :, cols] = (y * rs * g_ref[:, cols]).astype(BF16)


def _hy_conv(tables, z, x0, kr, ki, bias, g):
    b, _, half, _ = z.shape
    data = pl.BlockSpec((1, 2, half, HALF_CH), lambda i, c: (i, 0, 0, c))
    spec = pl.BlockSpec((2, half, HALF_CH), lambda i, c: (0, 0, c))
    return pl.pallas_call(
        _hy_conv_kernel,
        grid=(b, HY_DIM // HALF_CH),
        in_specs=[data, data] + [_resident(t.shape) for t in tables] + [spec, spec,
                  pl.BlockSpec((1, HALF_CH), lambda i, c: (0, c)), _resident((1, HY_DIM))],
        out_specs=pl.BlockSpec((1, 2 * half, HY_DIM), lambda i, c: (i, 0, 0)),
        out_shape=jax.ShapeDtypeStruct((b, 2 * half, HY_DIM), BF16),
        scratch_shapes=[pltpu.VMEM((HY_DIM // LANES, 2 * half, LANES), F32)],
        compiler_params=_params("parallel", "arbitrary"),
        name="hy_conv",
    )(z, x0, *tables, kr, ki, bias, g)


def _out_proj_kernel(x_ref, a_ref, b_ref, c_ref, w_ref, o_ref):
    acc = jnp.dot(a_ref[...], w_ref[0:POOL_DIM, :], preferred_element_type=F32)
    acc = acc + jnp.dot(b_ref[...], w_ref[POOL_DIM:POOL_DIM + MLA_DIM, :], preferred_element_type=F32)
    acc = acc + jnp.dot(c_ref[...], w_ref[POOL_DIM + MLA_DIM:, :], preferred_element_type=F32)
    o_ref[...] = x_ref[...] + acc


def _out_proj(x2, ma, mb, mc, w_all, layer, tm):
    m, d = x2.shape
    row = lambda width: pl.BlockSpec((tm, width), lambda i: (i, 0))
    w_spec = pl.BlockSpec((None,) + w_all.shape[1:], lambda i: (layer, 0, 0), pipeline_mode=pl.Buffered(1))
    return pl.pallas_call(
        _out_proj_kernel,
        grid=(m // tm,),
        in_specs=[row(d), row(POOL_DIM), row(MLA_DIM), row(HY_DIM), w_spec],
        out_specs=row(d),
        out_shape=jax.ShapeDtypeStruct((m, d), F32),
        compiler_params=_params("parallel"),
        name="out_proj",
    )(x2, ma, mb, mc, w_all)


def _ffn_kernel(x_ref, xp_ref, xn_ref, g_ref, wg_ref, wv_ref, cw_ref, cb_ref, wd_ref, o_ref, h_s, up_s, *, blocks_per_seq):
    tm = x_ref.shape[0]
    f = pl.program_id(1)

    @pl.when(f == 0)
    def _():
        x = x_ref[...]
        g = g_ref[...]
        o_ref[...] = x
        h_s[HALO:HALO + tm, :] = _rms(x, g).astype(BF16)
        pos = pl.program_id(0) % blocks_per_seq
        pad = jnp.zeros((HALO - 8, x.shape[1]), F32)
        hp = jnp.where(pos == 0, 0.0, _rms(xp_ref[...], g))
        hn = jnp.where(pos == blocks_per_seq - 1, 0.0, _rms(xn_ref[...], g))
        h_s[0:HALO, :] = jnp.concatenate([pad, hp], axis=0).astype(BF16)
        h_s[HALO + tm:, :] = jnp.concatenate([hn, pad], axis=0).astype(BF16)

    parts = up_s.shape[0]
    sub = tm // parts
    vals = []
    for r in range(parts):
        lo = r * sub
        up_s[r] = _mm(h_s[lo:lo + sub + 2 * HALO, :], wg_ref[...])
        vals.append(_mm(h_s[HALO + lo:HALO + lo + sub, :], wv_ref[...]))
    cw = cw_ref[...]
    for r in range(parts):
        gate = (up_s[r, HALO - 1:HALO - 1 + sub, :] * cw[0:1] + up_s[r, HALO:HALO + sub, :] * cw[1:2]
                + up_s[r, HALO + 1:HALO + 1 + sub, :] * cw[2:3] + cb_ref[...])
        act = (gate * jax.nn.sigmoid(gate) * vals[r]).astype(BF16)
        o_ref[r * sub:(r + 1) * sub, :] += _mm(act, wd_ref[...])


def _ffn(x2, seq, g, w_up, cw, cb, w_down, layer, tm, tf):
    m, d = x2.shape
    nf = D_FF // tf
    r8 = tm // 8
    last8 = m // 8 - 1
    kern = functools.partial(_ffn_kernel, blocks_per_seq=seq // tm)
    return pl.pallas_call(
        kern,
        grid=(m // tm, nf),
        in_specs=[pl.BlockSpec((tm, d), lambda i, f: (i, 0)),
                  pl.BlockSpec((8, d), lambda i, f: (jnp.maximum(i * r8 - 1, 0), 0)),
                  pl.BlockSpec((8, d), lambda i, f: (jnp.minimum((i + 1) * r8, last8), 0)),
                  _resident((1, d)),
                  pl.BlockSpec((None, d, tf), lambda i, f: (layer, 0, f)),
                  pl.BlockSpec((None, d, tf), lambda i, f: (layer, 0, f + nf)),
                  pl.BlockSpec((3, tf), lambda i, f: (0, f)),
                  pl.BlockSpec((1, tf), lambda i, f: (0, f)),
                  pl.BlockSpec((None, tf, d), lambda i, f: (layer, f, 0))],
        out_specs=pl.BlockSpec((tm, d), lambda i, f: (i, 0)),
        out_shape=jax.ShapeDtypeStruct((m, d), F32),
        scratch_shapes=[pltpu.VMEM((tm + 2 * HALO, d), BF16),
                        pltpu.VMEM((FFN_PARTS, tm // FFN_PARTS + 2 * HALO, tf), F32)],
        compiler_params=_params("parallel", "arbitrary"),
        name="ffn",
    )(x2, x2, x2, g, w_up, w_up, cw, cb, w_down)


def _rope_tables(seq):
    freqs = ROPE_THETA ** (-jnp.arange(0, QK_ROPE, 2, dtype=F32) / QK_ROPE)
    ang = jnp.arange(seq, dtype=F32)[:, None] * freqs[None, :]
    cos, sin, zero = jnp.cos(ang), jnp.sin(ang), jnp.zeros((seq, ROPE_HALF), F32)
    return (jnp.concatenate([cos, zero, cos, zero], axis=1), jnp.concatenate([-sin, zero, sin, zero], axis=1))


def _dft_tables(seq):
    half = seq // 2
    f = jnp.arange(half, dtype=jnp.int32)[:, None]
    j = jnp.arange(half, dtype=jnp.int32)[None, :]
    out = []
    for tok in (2 * j, 2 * j + 1):
        ang = (((2 * f + 1) * tok) % (4 * seq)).astype(F32) * (math.pi / (2 * seq))
        out.append((jnp.cos(ang).astype(BF16), jnp.sin(ang).astype(BF16)))
    (ce, se), (co, so) = out
    return ce, co, se, so, ce.T, co.T, se.T, so.T


def _filter_features(seq):
    t = jnp.linspace(0.0, 1.0, seq, dtype=F32)[:, None]
    w = 2.0 * math.pi * jnp.arange(seq, dtype=F32)[:, None] / seq
    bands = jnp.linspace(1e-4, N_BANDS - 1, N_BANDS, dtype=F32)[None, :]
    z = jnp.concatenate([t, jnp.cos(bands * w), -jnp.sin(bands * w)], axis=-1)
    z = jnp.concatenate([z[0::2], z[1::2]], axis=0)
    return jnp.pad(z, ((0, 0), (0, LANES - N_POS_FEAT)))


def _decay_rates():
    max_decay = math.log(HY_DECAY_TARGET) / HY_FAST_DECAY
    min_decay = math.log(HY_DECAY_TARGET) / HY_SLOW_DECAY
    return jnp.abs(jnp.linspace(min_decay, max_decay, HY_DIM, dtype=F32))[None, :]


def _spread_rope(a):
    zero = jnp.zeros(a.shape[:-1] + (ROPE_HALF,), a.dtype)
    return jnp.concatenate([a[..., :ROPE_HALF], zero, a[..., ROPE_HALF:], zero], axis=-1)


def _pad_to(a, shape):
    return jnp.pad(a, [(0, t - s) for s, t in zip(a.shape, shape)])


def _layer_weights(p, l):
    row = lambda a: a[l][None, :].astype(F32)
    w_in = p["w_in"][l]
    o1, o2, o3 = POOL_DIM + Q_LORA, POOL_DIM + Q_LORA + KV_LORA, POOL_DIM + Q_LORA + KV_LORA + QK_ROPE
    w_in_p = jnp.concatenate([w_in[:, o3:], w_in[:, :o2], _spread_rope(w_in[:, o2:o3])], axis=1).astype(BF16)
    wq = p["mla_w_uq"][l].reshape(Q_LORA, MLA_HEADS, QK_HEAD)
    wq_p = jnp.concatenate([wq[..., :QK_NOPE], _spread_rope(wq[..., QK_NOPE:])], axis=-1)
    wkv = p["mla_w_ukv"][l].reshape(KV_LORA, MLA_HEADS, QK_NOPE + V_HEAD)
    qhg, khg = p["mla_q_head_norm_g"][l], p["mla_k_head_norm_g"][l]
    hid = (LANES, LANES)
    return dict(
        attn_g=row(p["attn_norm_g"]), w_in=w_in_p,
        pool_w=p["pool_w"][l].astype(BF16), pool_scale=row(p["pool_scale"]),
        qn=row(p["mla_q_norm_g"]), wq=wq_p.reshape(Q_LORA, MLA_HEADS * HEAD_PAD).astype(BF16),
        kvn=row(p["mla_kv_norm_g"]),
        wk=wkv[..., :QK_NOPE].reshape(KV_LORA, MLA_HEADS * QK_NOPE).astype(BF16),
        wv=wkv[..., QK_NOPE:].reshape(KV_LORA, MLA_DIM).astype(BF16),
        qhg=jnp.concatenate([qhg[:QK_NOPE], _spread_rope(qhg[QK_NOPE:])])[None, :],
        khn=khg[None, :QK_NOPE], khr=_spread_rope(khg[QK_NOPE:])[None, :],
        hy_cw=p["hy_conv_w"][l], hy_cb=row(p["hy_conv_b"]), hy_bias=row(p["hy_bias"]),
        f_w1=_pad_to(p["hy_filt_w1"][l], hid), f_b1=_pad_to(row(p["hy_filt_b1"]), (1, LANES)),
        f_fr=_pad_to(row(p["hy_filt_freq"]), (1, LANES)),
        f_w2=_pad_to(p["hy_filt_w2"][l], hid), f_b2=_pad_to(row(p["hy_filt_b2"]), (1, LANES)),
        f_w3=_pad_to(p["hy_filt_w3"][l], (LANES, 2 * HY_DIM)), f_b3=row(p["hy_filt_b3"]),
        g_pool=p["grp_norm_g"][l][None, :POOL_DIM], g_mla=p["grp_norm_g"][l][None, POOL_DIM:POOL_DIM + MLA_DIM],
        g_hy=p["grp_norm_g"][l][None, POOL_DIM + MLA_DIM:],
        ffn_g=row(p["ffn_norm_g"]), ffn_cw=p["ffn_conv_w"][l], ffn_cb=row(p["ffn_conv_b"]),
    )


def _tile(n, want):
    return min(n, want)


def _layer(x2, b, seq, layer, w, stacks, kr, ki, rope, tables):
    cos_t, sin_t = rope
    m = b * seq
    q2, k2, v2, m_pool, z, x0 = _in_proj(
        x2, seq, w["attn_g"], w["w_in"], cos_t, sin_t, w["qn"], w["wq"], w["kvn"], w["wk"], w["wv"], w["qhg"],
        w["khn"], w["khr"], w["pool_w"], w["pool_scale"], w["g_pool"], w["hy_cw"], w["hy_cb"], _tile(seq, 512))
    hp = MLA_HEADS * HEAD_PAD
    m_mla = _attention(q2.reshape(b, seq, hp), k2.reshape(b, seq, hp), v2.reshape(b, seq, MLA_DIM),
                       w["g_mla"], _tile(seq, 256))
    m_hy = _hy_conv(tables, z, x0, kr, ki, w["hy_bias"], w["g_hy"])
    x2 = _out_proj(x2, m_pool.reshape(m, POOL_DIM), m_mla.reshape(m, MLA_DIM), m_hy.reshape(m, HY_DIM),
                   stacks["w_out"], layer, _tile(seq, 512))
    return _ffn(x2, seq, w["ffn_g"], stacks["w_up"], w["ffn_cw"], w["ffn_cb"], stacks["w_down"], layer,
                _tile(seq, 1024), 512)


def kernel(x_prompt, x_sample, attn_norm_g, w_in, pool_w, pool_scale, mla_q_norm_g, mla_w_uq, mla_kv_norm_g, mla_w_ukv, mla_q_head_norm_g, mla_k_head_norm_g, hy_conv_w, hy_conv_b, hy_filt_w1, hy_filt_b1, hy_filt_freq, hy_filt_w2, hy_filt_b2, hy_filt_w3, hy_filt_b3, hy_bias, grp_norm_g, w_out, ffn_norm_g, ffn_w_up, ffn_conv_w, ffn_conv_b, ffn_w_down):
    p = dict(attn_norm_g=attn_norm_g, w_in=w_in, pool_w=pool_w, pool_scale=pool_scale, mla_q_norm_g=mla_q_norm_g,
             mla_w_uq=mla_w_uq, mla_kv_norm_g=mla_kv_norm_g, mla_w_ukv=mla_w_ukv,
             mla_q_head_norm_g=mla_q_head_norm_g, mla_k_head_norm_g=mla_k_head_norm_g, hy_conv_w=hy_conv_w,
             hy_conv_b=hy_conv_b, hy_filt_w1=hy_filt_w1, hy_filt_b1=hy_filt_b1, hy_filt_freq=hy_filt_freq,
             hy_filt_w2=hy_filt_w2, hy_filt_b2=hy_filt_b2, hy_filt_w3=hy_filt_w3, hy_filt_b3=hy_filt_b3,
             hy_bias=hy_bias, grp_norm_g=grp_norm_g, w_out=w_out, ffn_norm_g=ffn_norm_g, ffn_w_up=ffn_w_up,
             ffn_conv_w=ffn_conv_w, ffn_conv_b=ffn_conv_b, ffn_w_down=ffn_w_down)
    depth = attn_norm_g.shape[0]
    seq = x_prompt.shape[1]
    assert x_sample.shape[1] == seq and x_prompt.shape[2] == D_MODEL
    rope = _rope_tables(seq)
    tables = _dft_tables(seq)
    feat = _filter_features(seq)
    rates = _decay_rates()
    stacks = dict(w_out=w_out.astype(BF16), w_up=ffn_w_up.astype(BF16), w_down=ffn_w_down.astype(BF16))
    xs = [x.reshape(-1, D_MODEL) for x in (x_prompt, x_sample)]
    for l in range(depth):
        w = _layer_weights(p, l)
        hs, hd = _hy_filter(feat, w["f_w1"], w["f_b1"], w["f_fr"], w["f_w2"], w["f_b2"], w["f_w3"], w["f_b3"],
                            rates, _tile(seq, 512))
        half = (2, seq // 2, HY_DIM)
        kr, ki = _hy_spec(tables, hs.reshape(half), hd.reshape(half))
        xs = [_layer(x2, x.shape[0], seq, l, w, stacks, kr, ki, rope, tables)
              for x2, x in zip(xs, (x_prompt, x_sample))]
    return (xs[0].reshape(x_prompt.shape), xs[1].reshape(x_sample.shape))
```

```python
import functools
import math

import jax
import jax.numpy as jnp
from jax import lax
from jax.experimental import pallas as pl
from jax.experimental.pallas import tpu as pltpu

F32 = jnp.float32
BF16 = jnp.bfloat16

D_MODEL = 2048
DEPTH = 4
POOL_DIM = 512
POOL_WINDOWS = (2, 4, 8, 16)
POOL_GROUP = 128
MLA_HEADS = 8
QK_NOPE = 128
QK_ROPE = 64
QK_HEAD = QK_NOPE + QK_ROPE
V_HEAD = 128
MLA_DIM = MLA_HEADS * V_HEAD
Q_LORA = 512
KV_LORA = 256
ROPE_THETA = 10000.0
HY_DIM = 512
N_BANDS = 8
N_POS_FEAT = 1 + 2 * N_BANDS
FILT_HIDDEN = 64
HY_FAST_DECAY = 0.3
HY_SLOW_DECAY = 1.5
HY_DECAY_TARGET = 1e-2
D_FF = 5632
NORM_EPS = 1e-6

LANES = 128
HEAD_PAD = 2 * LANES
ROPE_HALF = QK_ROPE // 2
HALF_CH = 2 * LANES
COL_POOL = 3 * HY_DIM
COL_CQ = COL_POOL + POOL_DIM
HALO = 16
FFN_PARTS = 2
VMEM_LIMIT = 56 * 1024 * 1024


def _params(*sem):
    return pltpu.CompilerParams(dimension_semantics=sem, vmem_limit_bytes=VMEM_LIMIT)


def _resident(shape):
    nd = len(shape)
    return pl.BlockSpec(shape, lambda *_: (0,) * nd, pipeline_mode=pl.Buffered(1))


def _mm(a, b):
    return jnp.dot(a, b, preferred_element_type=F32)


def _rms(x, g, n=None):
    n = x.shape[-1] if n is None else n
    ms = jnp.sum(x * x, axis=-1, keepdims=True) * (1.0 / n)
    return x * lax.rsqrt(ms + NORM_EPS) * g


def _rope(r, cos_t, sin_t):
    return r * cos_t + pltpu.roll(r, LANES // 2, 1) * sin_t


def _in_proj_kernel(x_ref, xp_ref, xn_ref, g_ref, w_ref, cos_ref, sin_ref, qn_ref, wq_ref, kvn_ref, wk_ref, wv_ref,
                    qhg_ref, khn_ref, khr_ref, pw_ref, ps_ref, pg_ref, cw_ref, cb_ref,
                    q_ref, k_ref, v_ref, mp_ref, z_ref, x0_ref, h_s, t_s, y_s, *, blocks_per_seq, seq):
    tm = x_ref.shape[0]
    g = g_ref[...]
    pos = pl.program_id(0) % blocks_per_seq
    h = _rms(x_ref[...], g).astype(BF16)
    h_s[HALO:HALO + tm, :] = h
    pad = jnp.zeros((HALO - 8, x_ref.shape[1]), F32)
    hp = jnp.where(pos == 0, 0.0, _rms(xp_ref[...], g))
    hn = jnp.where(pos == blocks_per_seq - 1, 0.0, _rms(xn_ref[...], g))
    h_s[0:HALO, :] = jnp.concatenate([pad, hp], axis=0).astype(BF16)
    h_s[HALO + tm:, :] = jnp.concatenate([hn, pad], axis=0).astype(BF16)
    lat = _mm(h, w_ref[:, COL_CQ:])
    cq = _rms(lat[:, :Q_LORA], qn_ref[...]).astype(BF16)
    ckv = _rms(lat[:, Q_LORA:Q_LORA + KV_LORA], kvn_ref[...]).astype(BF16)
    kr = lat[:, Q_LORA + KV_LORA:]
    qf = _mm(cq, wq_ref[...])
    kn = _mm(ckv, wk_ref[...])
    v_ref[...] = _mm(ckv, wv_ref[...]).astype(BF16)
    pool_u = _mm(h_s[...], w_ref[:, COL_POOL:COL_CQ])
    hy_u = [[_mm(h_s[...], w_ref[:, sec * HY_DIM + cc * HALF_CH:sec * HY_DIM + (cc + 1) * HALF_CH])
             for sec in range(3)] for cc in range(HY_DIM // HALF_CH)]

    cos_t = cos_ref[...]
    sin_t = sin_ref[...]
    scale = QK_HEAD ** -0.5 * math.log2(math.e)
    qg = qhg_ref[...] * scale
    qg_n, qg_r = qg[:, :LANES], qg[:, LANES:]
    rope_a = qg_r * cos_t
    rope_b = pltpu.roll(qg_r, LANES // 2, 1) * sin_t
    for hd in range(MLA_HEADS):
        qh = qf[:, hd * HEAD_PAD:(hd + 1) * HEAD_PAD]
        rs = lax.rsqrt(jnp.sum(qh * qh, axis=-1, keepdims=True) * (1.0 / QK_HEAD) + NORM_EPS)
        qr = qh[:, LANES:]
        q_ref[:, hd * HEAD_PAD:hd * HEAD_PAD + LANES] = (qh[:, :LANES] * rs * qg_n).astype(BF16)
        q_ref[:, hd * HEAD_PAD + LANES:(hd + 1) * HEAD_PAD] = (
            (qr * rope_a + pltpu.roll(qr, LANES // 2, 1) * rope_b) * rs).astype(BF16)
    ss_r = jnp.sum(kr * kr, axis=-1, keepdims=True)
    kr_rot = _rope(kr * khr_ref[...], cos_t, sin_t)
    for hd in range(MLA_HEADS):
        knh = kn[:, hd * LANES:(hd + 1) * LANES]
        ms = (jnp.sum(knh * knh, axis=-1, keepdims=True) + ss_r) * (1.0 / QK_HEAD)
        rs = lax.rsqrt(ms + NORM_EPS)
        k_ref[:, hd * HEAD_PAD:hd * HEAD_PAD + LANES] = (knh * rs * khn_ref[...]).astype(BF16)
        k_ref[:, hd * HEAD_PAD + LANES:(hd + 1) * HEAD_PAD] = (kr_rot * rs).astype(BF16)

    ext = tm + 2 * HALO
    main = slice(HALO, HALO + tm)
    down = lambda a, n: pltpu.roll(a, n, 0)
    up = lambda a, n: pltpu.roll(a, ext - n, 0)

    tok = lax.broadcasted_iota(jnp.int32, (tm, 1), 0) + pos * tm
    for gi, win in enumerate(POOL_WINDOWS):
        half = win // 2
        cols = slice(gi * POOL_GROUP, (gi + 1) * POOL_GROUP)
        xg = pool_u[:, cols]
        ahead = xg
        behind = down(xg, 1)
        span = 1
        while span < half:
            ahead = ahead + up(ahead, span)
            behind = behind + down(behind, span)
            span *= 2
        tot = (ahead + behind)[main]
        cnt = (jnp.minimum(tok + half, seq) - jnp.maximum(tok - half, 0)).astype(F32)
        pooled = (tot / cnt - xg[main]).astype(BF16)
        y_s[:, cols] = _mm(pooled, pw_ref[gi]) * ps_ref[:, cols]
    mp_ref[...] = _rms(y_s[...], pg_ref[...]).astype(BF16)

    def conv3(xc, col):
        cols = slice(col, col + LANES)
        w = cw_ref[:, cols]
        return (down(xc, 1) * w[0:1] + xc * w[1:2] + up(xc, 1) * w[2:3])[main] + cb_ref[:, cols]

    def store_split(dst_ref, slab, cols, val):
        t_s[slab] = val
        dst_ref[0, 0, :, cols] = t_s[slab, pl.ds(0, tm // 2, stride=2), :].astype(BF16)
        dst_ref[0, 1, :, cols] = t_s[slab, pl.ds(1, tm // 2, stride=2), :].astype(BF16)

    for c in range(HY_DIM // LANES):
        cols = slice(c * LANES, (c + 1) * LANES)
        x0u, x1u, vu = (u[:, (c % 2) * LANES:(c % 2 + 1) * LANES] for u in hy_u[c // 2])
        store_split(x0_ref, 2 * c, cols, conv3(x0u, c * LANES))
        store_split(z_ref, 2 * c + 1, cols, conv3(vu, 2 * HY_DIM + c * LANES) * conv3(x1u, HY_DIM + c * LANES))


def _in_proj(x2, seq, g, w, cos_t, sin_t, qn, wq, kvn, wk, wv, qhg, khn, khr, pool_w, pool_scale, g_pool,
             hy_cw, hy_cb, tm):
    m, d = x2.shape
    nseq = seq // tm
    hp = MLA_HEADS * HEAD_PAD
    r8 = tm // 8
    last8 = m // 8 - 1
    row = lambda width: pl.BlockSpec((tm, width), lambda i: (i, 0))
    tab = pl.BlockSpec((tm, LANES), lambda i: (i % nseq, 0))
    split = pl.BlockSpec((1, 2, tm // 2, HY_DIM), lambda i: (i // nseq, 0, i % nseq, 0))
    split_shape = jax.ShapeDtypeStruct((m // seq, 2, seq // 2, HY_DIM), BF16)
    consts = (qn, wq, kvn, wk, wv, qhg, khn, khr, pool_w, pool_scale, g_pool, hy_cw, hy_cb)
    return pl.pallas_call(
        functools.partial(_in_proj_kernel, blocks_per_seq=nseq, seq=seq),
        grid=(m // tm,),
        in_specs=[row(d),
                  pl.BlockSpec((8, d), lambda i: (jnp.maximum(i * r8 - 1, 0), 0)),
                  pl.BlockSpec((8, d), lambda i: (jnp.minimum((i + 1) * r8, last8), 0)),
                  _resident((1, d)), _resident(w.shape), tab, tab] + [_resident(a.shape) for a in consts],
        out_specs=[row(hp), row(hp), row(MLA_DIM), row(POOL_DIM), split, split],
        out_shape=[jax.ShapeDtypeStruct((m, hp), BF16), jax.ShapeDtypeStruct((m, hp), BF16),
                   jax.ShapeDtypeStruct((m, MLA_DIM), BF16), jax.ShapeDtypeStruct((m, POOL_DIM), BF16),
                   split_shape, split_shape],
        scratch_shapes=[pltpu.VMEM((tm + 2 * HALO, d), BF16), pltpu.VMEM((2 * HY_DIM // LANES, tm, LANES), F32),
                        pltpu.VMEM((tm, POOL_DIM), F32)],
        compiler_params=_params("parallel"),
        name="in_proj",
    )(x2, x2, x2, g, w, cos_t, sin_t, *consts)


def _attn_kernel(q_ref, k_ref, v_ref, g_ref, o_ref, y_s, v_s):
    @pl.when(pl.program_id(1) == 0)
    def _():
        ones_col = (lax.broadcasted_iota(jnp.int32, (v_s.shape[0], LANES), 1) == 0).astype(BF16)
        for h in range(MLA_HEADS):
            v_s[:, h * HEAD_PAD:h * HEAD_PAD + V_HEAD] = v_ref[0, :, h * V_HEAD:(h + 1) * V_HEAD]
            v_s[:, h * HEAD_PAD + V_HEAD:(h + 1) * HEAD_PAD] = ones_col

    def scores(h):
        q = q_ref[0, :, h * HEAD_PAD:(h + 1) * HEAD_PAD]
        k = k_ref[0, :, h * HEAD_PAD:(h + 1) * HEAD_PAD]
        return lax.dot_general(q, k, (((1,), (1,)), ((), ())), preferred_element_type=F32)

    s_next = scores(0)
    for h in range(MLA_HEADS):
        s = s_next
        if h + 1 < MLA_HEADS:
            s_next = scores(h + 1)
        p = jnp.exp2(s - jnp.max(s, axis=-1, keepdims=True)).astype(BF16)
        o = _mm(p, v_s[:, h * HEAD_PAD:(h + 1) * HEAD_PAD])
        y_s[:, h * V_HEAD:(h + 1) * V_HEAD] = o[:, :V_HEAD] / o[:, V_HEAD:V_HEAD + 1]
    o_ref[0] = _rms(y_s[...], g_ref[...]).astype(BF16)


def _attention(q3, k3, v3, g, tq):
    b, seq, hp = q3.shape
    return pl.pallas_call(
        _attn_kernel,
        grid=(b, seq // tq),
        in_specs=[pl.BlockSpec((1, tq, hp), lambda i, j: (i, j, 0)),
                  pl.BlockSpec((1, seq, hp), lambda i, j: (i, 0, 0)),
                  pl.BlockSpec((1, seq, MLA_DIM), lambda i, j: (i, 0, 0)),
                  _resident((1, MLA_DIM))],
        out_specs=pl.BlockSpec((1, tq, MLA_DIM), lambda i, j: (i, j, 0)),
        out_shape=jax.ShapeDtypeStruct((b, seq, MLA_DIM), BF16),
        scratch_shapes=[pltpu.VMEM((tq, MLA_DIM), F32), pltpu.VMEM((seq, hp), BF16)],
        compiler_params=_params("parallel", "arbitrary"),
        name="attention",
    )(q3, k3, v3, g)


def _hy_filter_kernel(feat_ref, w1_ref, b1_ref, fr_ref, w2_ref, b2_ref, w3_ref, b3_ref, dl_ref, hs_ref, hd_ref):
    hi = lax.Precision.HIGHEST
    feat = feat_ref[...]
    fr = fr_ref[...]
    h = jnp.sin(fr * (jnp.dot(feat, w1_ref[...], precision=hi, preferred_element_type=F32) + b1_ref[...]))
    h = jnp.sin(fr * (jnp.dot(h, w2_ref[...], precision=hi, preferred_element_type=F32) + b2_ref[...]))
    h = jnp.dot(h, w3_ref[...], precision=hi, preferred_element_type=F32) + b3_ref[...]
    decay = jnp.exp(-feat[:, 0:1] * dl_ref[...])
    fwd = h[:, :HY_DIM] * decay
    row = lax.broadcasted_iota(jnp.int32, (feat.shape[0], 1), 0) + pl.program_id(0) * feat.shape[0]
    bwd = jnp.where(row > 0, h[:, HY_DIM:] * decay, 0.0)
    hs_ref[...] = (fwd + bwd).astype(BF16)
    hd_ref[...] = (bwd - fwd).astype(BF16)


def _hy_filter(feat, w1, b1, fr, w2, b2, w3, b3, dl, tl):
    seq = feat.shape[0]
    out = jax.ShapeDtypeStruct((seq, HY_DIM), BF16)
    return pl.pallas_call(
        _hy_filter_kernel,
        grid=(seq // tl,),
        in_specs=[pl.BlockSpec((tl, LANES), lambda i: (i, 0))] + [_resident(a.shape) for a in (w1, b1, fr, w2, b2, w3, b3, dl)],
        out_specs=[pl.BlockSpec((tl, HY_DIM), lambda i: (i, 0))] * 2,
        out_shape=[out, out],
        compiler_params=_params("parallel"),
        name="hy_filter",
    )(feat, w1, b1, fr, w2, b2, w3, b3, dl)


def _hy_spec_kernel(ce_ref, co_ref, se_ref, so_ref, hs_ref, hd_ref, kr_ref, ki_ref):
    norm = 0.5 / ce_ref.shape[1]
    ae, ao = _mm(ce_ref[...], hs_ref[0]), _mm(co_ref[...], hs_ref[1])
    be, bo = _mm(se_ref[...], hd_ref[0]), _mm(so_ref[...], hd_ref[1])
    kr_ref[0] = (ae + ao) * norm
    kr_ref[1] = (ae - ao) * norm
    ki_ref[0] = (be + bo) * norm
    ki_ref[1] = (bo - be) * norm


def _hy_spec(tables, hs, hd):
    half = hs.shape[1]
    blk = pl.BlockSpec((2, half, HALF_CH), lambda c: (0, 0, c))
    out = jax.ShapeDtypeStruct((2, half, HY_DIM), F32)
    return pl.pallas_call(
        _hy_spec_kernel,
        grid=(HY_DIM // HALF_CH,),
        in_specs=[_resident(t.shape) for t in tables[:4]] + [blk, blk],
        out_specs=[blk, blk],
        out_shape=[out, out],
        compiler_params=_params("parallel"),
        name="hy_spec",
    )(*tables[:4], hs, hd)


def _hy_conv_kernel(z_ref, x0_ref, ce_ref, co_ref, se_ref, so_ref, cet_ref, cot_ref, set_ref, sot_ref,
                    kr_ref, ki_ref, b_ref, g_ref, o_ref, y_s):
    c = pl.program_id(1)
    half = z_ref.shape[2]
    ze, zo = z_ref[0, 0], z_ref[0, 1]
    ae, ao = _mm(ce_ref[...], ze), _mm(co_ref[...], zo)
    be, bo = _mm(se_ref[...], ze), _mm(so_ref[...], zo)
    u_lo, u_hi, v_lo, v_hi = ae + ao, ae - ao, be + bo, bo - be
    kr_lo, kr_hi, ki_lo, ki_hi = kr_ref[0], kr_ref[1], ki_ref[0], ki_ref[1]
    yr_lo, w_lo = u_lo * kr_lo + v_lo * ki_lo, v_lo * kr_lo - u_lo * ki_lo
    yr_hi, w_hi = u_hi * kr_hi + v_hi * ki_hi, v_hi * kr_hi - u_hi * ki_hi
    ye = _mm(cet_ref[...], (yr_lo + yr_hi).astype(BF16)) + _mm(set_ref[...], (w_lo - w_hi).astype(BF16))
    yo = _mm(cot_ref[...], (yr_lo - yr_hi).astype(BF16)) + _mm(sot_ref[...], (w_lo + w_hi).astype(BF16))
    bias = b_ref[...]
    ye = (ye + ze.astype(F32) * bias) * x0_ref[0, 0].astype(F32)
    yo = (yo + zo.astype(F32) * bias) * x0_ref[0, 1].astype(F32)
    per = HALF_CH // LANES
    for k in range(per):
        y_s[c * per + k, pl.ds(0, half, stride=2), :] = ye[:, k * LANES:(k + 1) * LANES]
        y_s[c * per + k, pl.ds(1, half, stride=2), :] = yo[:, k * LANES:(k + 1) * LANES]

    @pl.when(c == pl.num_programs(1) - 1)
    def _():
        slabs = [y_s[k] for k in range(HY_DIM // LANES)]
        ms = sum(jnp.sum(y * y, axis=-1, keepdims=True) for y in slabs) * (1.0 / HY_DIM)
        rs = lax.rsqrt(ms + NORM_EPS)
        for k, y in enumerate(slabs):
            cols = slice(k * LANES, (k + 1) * LANES)
            o_ref[0, :, cols] = (y * rs * g_ref[:, cols]).astype(BF16)


def _hy_conv(tables, z, x0, kr, ki, bias, g):
    b, _, half, _ = z.shape
    data = pl.BlockSpec((1, 2, half, HALF_CH), lambda i, c: (i, 0, 0, c))
    spec = pl.BlockSpec((2, half, HALF_CH), lambda i, c: (0, 0, c))
    return pl.pallas_call(
        _hy_conv_kernel,
        grid=(b, HY_DIM // HALF_CH),
        in_specs=[data, data] + [_resident(t.shape) for t in tables] + [spec, spec,
                  pl.BlockSpec((1, HALF_CH), lambda i, c: (0, c)), _resident((1, HY_DIM))],
        out_specs=pl.BlockSpec((1, 2 * half, HY_DIM), lambda i, c: (i, 0, 0)),
        out_shape=jax.ShapeDtypeStruct((b, 2 * half, HY_DIM), BF16),
        scratch_shapes=[pltpu.VMEM((HY_DIM // LANES, 2 * half, LANES), F32)],
        compiler_params=_params("parallel", "arbitrary"),
        name="hy_conv",
    )(z, x0, *tables, kr, ki, bias, g)


def _out_proj_kernel(x_ref, a_ref, b_ref, c_ref, w_ref, o_ref):
    acc = jnp.dot(a_ref[...], w_ref[0:POOL_DIM, :], preferred_element_type=F32)
    acc = acc + jnp.dot(b_ref[...], w_ref[POOL_DIM:POOL_DIM + MLA_DIM, :], preferred_element_type=F32)
    acc = acc + jnp.dot(c_ref[...], w_ref[POOL_DIM + MLA_DIM:, :], preferred_element_type=F32)
    o_ref[...] = x_ref[...] + acc


def _out_proj(x2, ma, mb, mc, w_all, layer, tm):
    m, d = x2.shape
    row = lambda width: pl.BlockSpec((tm, width), lambda i: (i, 0))
    w_spec = pl.BlockSpec((None,) + w_all.shape[1:], lambda i: (layer, 0, 0), pipeline_mode=pl.Buffered(1))
    return pl.pallas_call(
        _out_proj_kernel,
        grid=(m // tm,),
        in_specs=[row(d), row(POOL_DIM), row(MLA_DIM), row(HY_DIM), w_spec],
        out_specs=row(d),
        out_shape=jax.ShapeDtypeStruct((m, d), F32),
        compiler_params=_params("parallel"),
        name="out_proj",
    )(x2, ma, mb, mc, w_all)


def _ffn_kernel(x_ref, xp_ref, xn_ref, g_ref, wg_ref, wv_ref, cw_ref, cb_ref, wd_ref, o_ref, h_s, up_s, *, blocks_per_seq):
    tm = x_ref.shape[0]
    f = pl.program_id(1)

    @pl.when(f == 0)
    def _():
        x = x_ref[...]
        g = g_ref[...]
        h_s[HALO:HALO + tm, :] = _rms(x, g).astype(BF16)
        pos = pl.program_id(0) % blocks_per_seq
        pad = jnp.zeros((HALO - 8, x.shape[1]), F32)
        hp = jnp.where(pos == 0, 0.0, _rms(xp_ref[...], g))
        hn = jnp.where(pos == blocks_per_seq - 1, 0.0, _rms(xn_ref[...], g))
        h_s[0:HALO, :] = jnp.concatenate([pad, hp], axis=0).astype(BF16)
        h_s[HALO + tm:, :] = jnp.concatenate([hn, pad], axis=0).astype(BF16)

    parts = up_s.shape[0]
    sub = tm // parts
    vals = []
    for r in range(parts):
        lo = r * sub
        up_s[r] = _mm(h_s[lo:lo + sub + 2 * HALO, :], wg_ref[...])
        vals.append(_mm(h_s[HALO + lo:HALO + lo + sub, :], wv_ref[...]))
    cw = cw_ref[...]
    for r in range(parts):
        gate = (up_s[r, HALO - 1:HALO - 1 + sub, :] * cw[0:1] + up_s[r, HALO:HALO + sub, :] * cw[1:2]
                + up_s[r, HALO + 1:HALO + 1 + sub, :] * cw[2:3] + cb_ref[...])
        act = (gate * jax.nn.sigmoid(gate) * vals[r]).astype(BF16)
        rows = slice(r * sub, (r + 1) * sub)
        base = jnp.where(f == 0, x_ref[rows, :], o_ref[rows, :])
        o_ref[rows, :] = base + _mm(act, wd_ref[...])


def _ffn(x2, seq, g, w_up, cw, cb, w_down, layer, tm, tf):
    m, d = x2.shape
    nf = D_FF // tf
    r8 = tm // 8
    last8 = m // 8 - 1
    kern = functools.partial(_ffn_kernel, blocks_per_seq=seq // tm)
    return pl.pallas_call(
        kern,
        grid=(m // tm, nf),
        in_specs=[pl.BlockSpec((tm, d), lambda i, f: (i, 0)),
                  pl.BlockSpec((8, d), lambda i, f: (jnp.maximum(i * r8 - 1, 0), 0)),
                  pl.BlockSpec((8, d), lambda i, f: (jnp.minimum((i + 1) * r8, last8), 0)),
                  _resident((1, d)),
                  pl.BlockSpec((None, d, tf), lambda i, f: (layer, 0, f)),
                  pl.BlockSpec((None, d, tf), lambda i, f: (layer, 0, f + nf)),
                  pl.BlockSpec((3, tf), lambda i, f: (0, f)),
                  pl.BlockSpec((1, tf), lambda i, f: (0, f)),
                  pl.BlockSpec((None, tf, d), lambda i, f: (layer, f, 0))],
        out_specs=pl.BlockSpec((tm, d), lambda i, f: (i, 0)),
        out_shape=jax.ShapeDtypeStruct((m, d), F32),
        scratch_shapes=[pltpu.VMEM((tm + 2 * HALO, d), BF16),
                        pltpu.VMEM((FFN_PARTS, tm // FFN_PARTS + 2 * HALO, tf), F32)],
        compiler_params=_params("parallel", "arbitrary"),
        name="ffn",
    )(x2, x2, x2, g, w_up, w_up, cw, cb, w_down)


def _rope_tables(seq):
    freqs = ROPE_THETA ** (-jnp.arange(0, QK_ROPE, 2, dtype=F32) / QK_ROPE)
    ang = jnp.arange(seq, dtype=F32)[:, None] * freqs[None, :]
    cos, sin, zero = jnp.cos(ang), jnp.sin(ang), jnp.zeros((seq, ROPE_HALF), F32)
    return (jnp.concatenate([cos, zero, cos, zero], axis=1), jnp.concatenate([-sin, zero, sin, zero], axis=1))


def _dft_tables(seq):
    half = seq // 2
    f = jnp.arange(half, dtype=jnp.int32)[:, None]
    j = jnp.arange(half, dtype=jnp.int32)[None, :]
    out = []
    for tok in (2 * j, 2 * j + 1):
        ang = (((2 * f + 1) * tok) % (4 * seq)).astype(F32) * (math.pi / (2 * seq))
        out.append((jnp.cos(ang).astype(BF16), jnp.sin(ang).astype(BF16)))
    (ce, se), (co, so) = out
    return ce, co, se, so, ce.T, co.T, se.T, so.T


def _filter_features(seq):
    t = jnp.linspace(0.0, 1.0, seq, dtype=F32)[:, None]
    w = 2.0 * math.pi * jnp.arange(seq, dtype=F32)[:, None] / seq
    bands = jnp.linspace(1e-4, N_BANDS - 1, N_BANDS, dtype=F32)[None, :]
    z = jnp.concatenate([t, jnp.cos(bands * w), -jnp.sin(bands * w)], axis=-1)
    z = jnp.concatenate([z[0::2], z[1::2]], axis=0)
    return jnp.pad(z, ((0, 0), (0, LANES - N_POS_FEAT)))


def _decay_rates():
    max_decay = math.log(HY_DECAY_TARGET) / HY_FAST_DECAY
    min_decay = math.log(HY_DECAY_TARGET) / HY_SLOW_DECAY
    return jnp.abs(jnp.linspace(min_decay, max_decay, HY_DIM, dtype=F32))[None, :]


def _spread_rope(a):
    zero = jnp.zeros(a.shape[:-1] + (ROPE_HALF,), a.dtype)
    return jnp.concatenate([a[..., :ROPE_HALF], zero, a[..., ROPE_HALF:], zero], axis=-1)


def _pad_to(a, shape):
    return jnp.pad(a, [(0, t - s) for s, t in zip(a.shape, shape)])


def _layer_weights(p, l):
    row = lambda a: a[l][None, :].astype(F32)
    w_in = p["w_in"][l]
    o1, o2, o3 = POOL_DIM + Q_LORA, POOL_DIM + Q_LORA + KV_LORA, POOL_DIM + Q_LORA + KV_LORA + QK_ROPE
    w_in_p = jnp.concatenate([w_in[:, o3:], w_in[:, :o2], _spread_rope(w_in[:, o2:o3])], axis=1).astype(BF16)
    wq = p["mla_w_uq"][l].reshape(Q_LORA, MLA_HEADS, QK_HEAD)
    wq_p = jnp.concatenate([wq[..., :QK_NOPE], _spread_rope(wq[..., QK_NOPE:])], axis=-1)
    wkv = p["mla_w_ukv"][l].reshape(KV_LORA, MLA_HEADS, QK_NOPE + V_HEAD)
    qhg, khg = p["mla_q_head_norm_g"][l], p["mla_k_head_norm_g"][l]
    hid = (LANES, LANES)
    return dict(
        attn_g=row(p["attn_norm_g"]), w_in=w_in_p,
        pool_w=p["pool_w"][l].astype(BF16), pool_scale=row(p["pool_scale"]),
        qn=row(p["mla_q_norm_g"]), wq=wq_p.reshape(Q_LORA, MLA_HEADS * HEAD_PAD).astype(BF16),
        kvn=row(p["mla_kv_norm_g"]),
        wk=wkv[..., :QK_NOPE].reshape(KV_LORA, MLA_HEADS * QK_NOPE).astype(BF16),
        wv=wkv[..., QK_NOPE:].reshape(KV_LORA, MLA_DIM).astype(BF16),
        qhg=jnp.concatenate([qhg[:QK_NOPE], _spread_rope(qhg[QK_NOPE:])])[None, :],
        khn=khg[None, :QK_NOPE], khr=_spread_rope(khg[QK_NOPE:])[None, :],
        hy_cw=p["hy_conv_w"][l], hy_cb=row(p["hy_conv_b"]), hy_bias=row(p["hy_bias"]),
        f_w1=_pad_to(p["hy_filt_w1"][l], hid), f_b1=_pad_to(row(p["hy_filt_b1"]), (1, LANES)),
        f_fr=_pad_to(row(p["hy_filt_freq"]), (1, LANES)),
        f_w2=_pad_to(p["hy_filt_w2"][l], hid), f_b2=_pad_to(row(p["hy_filt_b2"]), (1, LANES)),
        f_w3=_pad_to(p["hy_filt_w3"][l], (LANES, 2 * HY_DIM)), f_b3=row(p["hy_filt_b3"]),
        g_pool=p["grp_norm_g"][l][None, :POOL_DIM], g_mla=p["grp_norm_g"][l][None, POOL_DIM:POOL_DIM + MLA_DIM],
        g_hy=p["grp_norm_g"][l][None, POOL_DIM + MLA_DIM:],
        ffn_g=row(p["ffn_norm_g"]), ffn_cw=p["ffn_conv_w"][l], ffn_cb=row(p["ffn_conv_b"]),
    )


def _tile(n, want):
    return min(n, want)


def _layer(x2, b, seq, layer, w, stacks, kr, ki, rope, tables):
    cos_t, sin_t = rope
    m = b * seq
    q2, k2, v2, m_pool, z, x0 = _in_proj(
        x2, seq, w["attn_g"], w["w_in"], cos_t, sin_t, w["qn"], w["wq"], w["kvn"], w["wk"], w["wv"], w["qhg"],
        w["khn"], w["khr"], w["pool_w"], w["pool_scale"], w["g_pool"], w["hy_cw"], w["hy_cb"], _tile(seq, 512))
    hp = MLA_HEADS * HEAD_PAD
    m_mla = _attention(q2.reshape(b, seq, hp), k2.reshape(b, seq, hp), v2.reshape(b, seq, MLA_DIM),
                       w["g_mla"], _tile(seq, 512))
    m_hy = _hy_conv(tables, z, x0, kr, ki, w["hy_bias"], w["g_hy"])
    x2 = _out_proj(x2, m_pool.reshape(m, POOL_DIM), m_mla.reshape(m, MLA_DIM), m_hy.reshape(m, HY_DIM),
                   stacks["w_out"], layer, _tile(seq, 1024))
    return _ffn(x2, seq, w["ffn_g"], stacks["w_up"], w["ffn_cw"], w["ffn_cb"], stacks["w_down"], layer,
                _tile(seq, 1024), 512)


def kernel(x_prompt, x_sample, attn_norm_g, w_in, pool_w, pool_scale, mla_q_norm_g, mla_w_uq, mla_kv_norm_g, mla_w_ukv, mla_q_head_norm_g, mla_k_head_norm_g, hy_conv_w, hy_conv_b, hy_filt_w1, hy_filt_b1, hy_filt_freq, hy_filt_w2, hy_filt_b2, hy_filt_w3, hy_filt_b3, hy_bias, grp_norm_g, w_out, ffn_norm_g, ffn_w_up, ffn_conv_w, ffn_conv_b, ffn_w_down):
    p = dict(attn_norm_g=attn_norm_g, w_in=w_in, pool_w=pool_w, pool_scale=pool_scale, mla_q_norm_g=mla_q_norm_g,
             mla_w_uq=mla_w_uq, mla_kv_norm_g=mla_kv_norm_g, mla_w_ukv=mla_w_ukv,
             mla_q_head_norm_g=mla_q_head_norm_g, mla_k_head_norm_g=mla_k_head_norm_g, hy_conv_w=hy_conv_w,
             hy_conv_b=hy_conv_b, hy_filt_w1=hy_filt_w1, hy_filt_b1=hy_filt_b1, hy_filt_freq=hy_filt_freq,
             hy_filt_w2=hy_filt_w2, hy_filt_b2=hy_filt_b2, hy_filt_w3=hy_filt_w3, hy_filt_b3=hy_filt_b3,
             hy_bias=hy_bias, grp_norm_g=grp_norm_g, w_out=w_out, ffn_norm_g=ffn_norm_g, ffn_w_up=ffn_w_up,
             ffn_conv_w=ffn_conv_w, ffn_conv_b=ffn_conv_b, ffn_w_down=ffn_w_down)
    depth = attn_norm_g.shape[0]
    seq = x_prompt.shape[1]
    assert x_sample.shape[1] == seq and x_prompt.shape[2] == D_MODEL
    rope = _rope_tables(seq)
    tables = _dft_tables(seq)
    feat = _filter_features(seq)
    rates = _decay_rates()
    stacks = dict(w_out=w_out.astype(BF16), w_up=ffn_w_up.astype(BF16), w_down=ffn_w_down.astype(BF16))
    xs = [x.reshape(-1, D_MODEL) for x in (x_prompt, x_sample)]
    for l in range(depth):
        w = _layer_weights(p, l)
        hs, hd = _hy_filter(feat, w["f_w1"], w["f_b1"], w["f_fr"], w["f_w2"], w["f_b2"], w["f_w3"], w["f_b3"],
                            rates, _tile(seq, 512))
        half = (2, seq // 2, HY_DIM)
        kr, ki = _hy_spec(tables, hs.reshape(half), hd.reshape(half))
        xs = [_layer(x2, x.shape[0], seq, l, w, stacks, kr, ki, rope, tables)
              for x2, x in zip(xs, (x_prompt, x_sample))]
    return (xs[0].reshape(x_prompt.shape), xs[1].reshape(x_sample.shape))
```

```python
import functools
import math

import jax
import jax.numpy as jnp
from jax import lax
from jax.experimental import pallas as pl
from jax.experimental.pallas import tpu as pltpu

F32 = jnp.float32
BF16 = jnp.bfloat16

D_MODEL = 2048
POOL_DIM = 512
POOL_WINDOWS = (2, 4, 8, 16)
POOL_GROUP = 128
MLA_HEADS = 8
QK_NOPE = 128
QK_ROPE = 64
QK_HEAD = QK_NOPE + QK_ROPE
V_HEAD = 128
MLA_DIM = MLA_HEADS * V_HEAD
Q_LORA = 512
KV_LORA = 256
ROPE_THETA = 10000.0
HY_DIM = 512
N_BANDS = 8
N_POS_FEAT = 1 + 2 * N_BANDS
HY_FAST_DECAY = 0.3
HY_SLOW_DECAY = 1.5
HY_DECAY_TARGET = 1e-2
D_FF = 5632
NORM_EPS = 1e-6

LANES = 128
SUBLANES = 8
HEAD_PAD = 2 * LANES
ROPE_HALF = QK_ROPE // 2
HALF_CH = 2 * LANES
COL_POOL = 3 * HY_DIM
COL_CQ = COL_POOL + POOL_DIM
HALO = 16
FFN_PARTS = 2
IN_PROJ_ROWS = 512
ATTN_QUERIES = 512
OUT_PROJ_ROWS = 1024
FFN_ROWS = 1024
FFN_COLS = 512
FILTER_ROWS = 512
VMEM_LIMIT = 56 * 1024 * 1024


def _params(*sem):
    return pltpu.CompilerParams(dimension_semantics=sem, vmem_limit_bytes=VMEM_LIMIT)


def _resident(shape):
    nd = len(shape)
    return pl.BlockSpec(shape, lambda *_: (0,) * nd, pipeline_mode=pl.Buffered(1))


def _mm(a, b):
    return jnp.dot(a, b, preferred_element_type=F32)


def _rms(x, g):
    ms = jnp.sum(x * x, axis=-1, keepdims=True) * (1.0 / x.shape[-1])
    return x * lax.rsqrt(ms + NORM_EPS) * g


def _rope(r, cos_t, sin_t):
    return r * cos_t + pltpu.roll(r, LANES // 2, 1) * sin_t


def _in_proj_kernel(x_ref, xp_ref, xn_ref, g_ref, w_ref, cos_ref, sin_ref, qn_ref, wq_ref, kvn_ref, wk_ref, wv_ref,
                    qhg_ref, khn_ref, khr_ref, pw_ref, ps_ref, pg_ref, cw_ref, cb_ref,
                    q_ref, k_ref, v_ref, mp_ref, z_ref, x0_ref, h_s, t_s, y_s, *, blocks_per_seq, seq):
    tm = x_ref.shape[0]
    g = g_ref[...]
    pos = pl.program_id(0) % blocks_per_seq
    h = _rms(x_ref[...], g).astype(BF16)
    h_s[HALO:HALO + tm, :] = h
    pad = jnp.zeros((HALO - SUBLANES, x_ref.shape[1]), F32)
    hp = jnp.where(pos == 0, 0.0, _rms(xp_ref[...], g))
    hn = jnp.where(pos == blocks_per_seq - 1, 0.0, _rms(xn_ref[...], g))
    h_s[0:HALO, :] = jnp.concatenate([pad, hp], axis=0).astype(BF16)
    h_s[HALO + tm:, :] = jnp.concatenate([hn, pad], axis=0).astype(BF16)
    lat = _mm(h, w_ref[:, COL_CQ:])
    cq = _rms(lat[:, :Q_LORA], qn_ref[...]).astype(BF16)
    ckv = _rms(lat[:, Q_LORA:Q_LORA + KV_LORA], kvn_ref[...]).astype(BF16)
    kr = lat[:, Q_LORA + KV_LORA:]
    qf = _mm(cq, wq_ref[...])
    kn = _mm(ckv, wk_ref[...])
    v_ref[...] = _mm(ckv, wv_ref[...]).astype(BF16)
    pool_u = _mm(h_s[...], w_ref[:, COL_POOL:COL_CQ])
    hy_u = [[_mm(h_s[...], w_ref[:, sec * HY_DIM + cc * HALF_CH:sec * HY_DIM + (cc + 1) * HALF_CH])
             for sec in range(3)] for cc in range(HY_DIM // HALF_CH)]

    cos_t = cos_ref[...]
    sin_t = sin_ref[...]
    scale = QK_HEAD ** -0.5 * math.log2(math.e)
    qg = qhg_ref[...] * scale
    qg_n, qg_r = qg[:, :LANES], qg[:, LANES:]
    rope_a = qg_r * cos_t
    rope_b = pltpu.roll(qg_r, LANES // 2, 1) * sin_t
    for hd in range(MLA_HEADS):
        qh = qf[:, hd * HEAD_PAD:(hd + 1) * HEAD_PAD]
        rs = lax.rsqrt(jnp.sum(qh * qh, axis=-1, keepdims=True) * (1.0 / QK_HEAD) + NORM_EPS)
        qr = qh[:, LANES:]
        q_ref[:, hd * HEAD_PAD:hd * HEAD_PAD + LANES] = (qh[:, :LANES] * rs * qg_n).astype(BF16)
        q_ref[:, hd * HEAD_PAD + LANES:(hd + 1) * HEAD_PAD] = (
            (qr * rope_a + pltpu.roll(qr, LANES // 2, 1) * rope_b) * rs).astype(BF16)
    ss_r = jnp.sum(kr * kr, axis=-1, keepdims=True)
    kr_rot = _rope(kr * khr_ref[...], cos_t, sin_t)
    for hd in range(MLA_HEADS):
        knh = kn[:, hd * LANES:(hd + 1) * LANES]
        ms = (jnp.sum(knh * knh, axis=-1, keepdims=True) + ss_r) * (1.0 / QK_HEAD)
        rs = lax.rsqrt(ms + NORM_EPS)
        k_ref[:, hd * HEAD_PAD:hd * HEAD_PAD + LANES] = (knh * rs * khn_ref[...]).astype(BF16)
        k_ref[:, hd * HEAD_PAD + LANES:(hd + 1) * HEAD_PAD] = (kr_rot * rs).astype(BF16)

    ext = tm + 2 * HALO
    main = slice(HALO, HALO + tm)
    down = lambda a, n: pltpu.roll(a, n, 0)
    up = lambda a, n: pltpu.roll(a, ext - n, 0)

    tok = lax.broadcasted_iota(jnp.int32, (tm, 1), 0) + pos * tm
    for gi, win in enumerate(POOL_WINDOWS):
        half = win // 2
        cols = slice(gi * POOL_GROUP, (gi + 1) * POOL_GROUP)
        xg = pool_u[:, cols]
        ahead = xg
        behind = down(xg, 1)
        span = 1
        while span < half:
            ahead = ahead + up(ahead, span)
            behind = behind + down(behind, span)
            span *= 2
        tot = (ahead + behind)[main]
        cnt = (jnp.minimum(tok + half, seq) - jnp.maximum(tok - half, 0)).astype(F32)
        pooled = (tot / cnt - xg[main]).astype(BF16)
        y_s[:, cols] = _mm(pooled, pw_ref[gi]) * ps_ref[:, cols]
    mp_ref[...] = _rms(y_s[...], pg_ref[...]).astype(BF16)

    def conv3(xc, col):
        cols = slice(col, col + LANES)
        w = cw_ref[:, cols]
        return (down(xc, 1) * w[0:1] + xc * w[1:2] + up(xc, 1) * w[2:3])[main] + cb_ref[:, cols]

    def store_split(dst_ref, slab, cols, val):
        t_s[slab] = val
        dst_ref[0, 0, :, cols] = t_s[slab, pl.ds(0, tm // 2, stride=2), :].astype(BF16)
        dst_ref[0, 1, :, cols] = t_s[slab, pl.ds(1, tm // 2, stride=2), :].astype(BF16)

    for c in range(HY_DIM // LANES):
        cols = slice(c * LANES, (c + 1) * LANES)
        x0u, x1u, vu = (u[:, (c % 2) * LANES:(c % 2 + 1) * LANES] for u in hy_u[c // 2])
        store_split(x0_ref, 2 * c, cols, conv3(x0u, c * LANES))
        store_split(z_ref, 2 * c + 1, cols, conv3(vu, 2 * HY_DIM + c * LANES) * conv3(x1u, HY_DIM + c * LANES))


def _in_proj(x2, seq, g, w, cos_t, sin_t, qn, wq, kvn, wk, wv, qhg, khn, khr, pool_w, pool_scale, g_pool,
             hy_cw, hy_cb, tm):
    m, d = x2.shape
    nseq = seq // tm
    hp = MLA_HEADS * HEAD_PAD
    r8 = tm // SUBLANES
    last8 = m // SUBLANES - 1
    row = lambda width: pl.BlockSpec((tm, width), lambda i: (i, 0))
    tab = pl.BlockSpec((tm, LANES), lambda i: (i % nseq, 0))
    split = pl.BlockSpec((1, 2, tm // 2, HY_DIM), lambda i: (i // nseq, 0, i % nseq, 0))
    split_shape = jax.ShapeDtypeStruct((m // seq, 2, seq // 2, HY_DIM), BF16)
    consts = (qn, wq, kvn, wk, wv, qhg, khn, khr, pool_w, pool_scale, g_pool, hy_cw, hy_cb)
    return pl.pallas_call(
        functools.partial(_in_proj_kernel, blocks_per_seq=nseq, seq=seq),
        grid=(m // tm,),
        in_specs=[row(d),
                  pl.BlockSpec((SUBLANES, d), lambda i: (jnp.maximum(i * r8 - 1, 0), 0)),
                  pl.BlockSpec((SUBLANES, d), lambda i: (jnp.minimum((i + 1) * r8, last8), 0)),
                  _resident((1, d)), _resident(w.shape), tab, tab] + [_resident(a.shape) for a in consts],
        out_specs=[row(hp), row(hp), row(MLA_DIM), row(POOL_DIM), split, split],
        out_shape=[jax.ShapeDtypeStruct((m, hp), BF16), jax.ShapeDtypeStruct((m, hp), BF16),
                   jax.ShapeDtypeStruct((m, MLA_DIM), BF16), jax.ShapeDtypeStruct((m, POOL_DIM), BF16),
                   split_shape, split_shape],
        scratch_shapes=[pltpu.VMEM((tm + 2 * HALO, d), BF16), pltpu.VMEM((2 * HY_DIM // LANES, tm, LANES), F32),
                        pltpu.VMEM((tm, POOL_DIM), F32)],
        compiler_params=_params("parallel"),
        name="in_proj",
    )(x2, x2, x2, g, w, cos_t, sin_t, *consts)


def _attn_kernel(q_ref, k_ref, v_ref, g_ref, o_ref, y_s, v_s):
    @pl.when(pl.program_id(1) == 0)
    def _():
        ones_col = (lax.broadcasted_iota(jnp.int32, (v_s.shape[0], LANES), 1) == 0).astype(BF16)
        for h in range(MLA_HEADS):
            v_s[:, h * HEAD_PAD:h * HEAD_PAD + V_HEAD] = v_ref[0, :, h * V_HEAD:(h + 1) * V_HEAD]
            v_s[:, h * HEAD_PAD + V_HEAD:(h + 1) * HEAD_PAD] = ones_col

    def scores(h):
        q = q_ref[0, :, h * HEAD_PAD:(h + 1) * HEAD_PAD]
        k = k_ref[0, :, h * HEAD_PAD:(h + 1) * HEAD_PAD]
        return lax.dot_general(q, k, (((1,), (1,)), ((), ())), preferred_element_type=F32)

    s_next = scores(0)
    for h in range(MLA_HEADS):
        s = s_next
        if h + 1 < MLA_HEADS:
            s_next = scores(h + 1)
        p = jnp.exp2(s - jnp.max(s, axis=-1, keepdims=True)).astype(BF16)
        o = _mm(p, v_s[:, h * HEAD_PAD:(h + 1) * HEAD_PAD])
        y_s[:, h * V_HEAD:(h + 1) * V_HEAD] = o[:, :V_HEAD] / o[:, V_HEAD:V_HEAD + 1]
    o_ref[0] = _rms(y_s[...], g_ref[...]).astype(BF16)


def _attention(q3, k3, v3, g, tq):
    b, seq, hp = q3.shape
    return pl.pallas_call(
        _attn_kernel,
        grid=(b, seq // tq),
        in_specs=[pl.BlockSpec((1, tq, hp), lambda i, j: (i, j, 0)),
                  pl.BlockSpec((1, seq, hp), lambda i, j: (i, 0, 0)),
                  pl.BlockSpec((1, seq, MLA_DIM), lambda i, j: (i, 0, 0)),
                  _resident((1, MLA_DIM))],
        out_specs=pl.BlockSpec((1, tq, MLA_DIM), lambda i, j: (i, j, 0)),
        out_shape=jax.ShapeDtypeStruct((b, seq, MLA_DIM), BF16),
        scratch_shapes=[pltpu.VMEM((tq, MLA_DIM), F32), pltpu.VMEM((seq, hp), BF16)],
        compiler_params=_params("parallel", "arbitrary"),
        name="attention",
    )(q3, k3, v3, g)


def _hy_filter_kernel(feat_ref, w1_ref, b1_ref, fr_ref, w2_ref, b2_ref, w3_ref, b3_ref, dl_ref, hs_ref, hd_ref):
    hi = lax.Precision.HIGHEST
    feat = feat_ref[...]
    fr = fr_ref[...]
    h = jnp.sin(fr * (jnp.dot(feat, w1_ref[...], precision=hi, preferred_element_type=F32) + b1_ref[...]))
    h = jnp.sin(fr * (jnp.dot(h, w2_ref[...], precision=hi, preferred_element_type=F32) + b2_ref[...]))
    h = jnp.dot(h, w3_ref[...], precision=hi, preferred_element_type=F32) + b3_ref[...]
    decay = jnp.exp(-feat[:, 0:1] * dl_ref[...])
    fwd = h[:, :HY_DIM] * decay
    row = lax.broadcasted_iota(jnp.int32, (feat.shape[0], 1), 0) + pl.program_id(0) * feat.shape[0]
    bwd = jnp.where(row > 0, h[:, HY_DIM:] * decay, 0.0)
    hs_ref[...] = (fwd + bwd).astype(BF16)
    hd_ref[...] = (bwd - fwd).astype(BF16)


def _hy_filter(feat, w1, b1, fr, w2, b2, w3, b3, dl, tl):
    seq = feat.shape[0]
    out = jax.ShapeDtypeStruct((seq, HY_DIM), BF16)
    return pl.pallas_call(
        _hy_filter_kernel,
        grid=(seq // tl,),
        in_specs=[pl.BlockSpec((tl, LANES), lambda i: (i, 0))] + [_resident(a.shape) for a in (w1, b1, fr, w2, b2, w3, b3, dl)],
        out_specs=[pl.BlockSpec((tl, HY_DIM), lambda i: (i, 0))] * 2,
        out_shape=[out, out],
        compiler_params=_params("parallel"),
        name="hy_filter",
    )(feat, w1, b1, fr, w2, b2, w3, b3, dl)


def _hy_spec_kernel(ce_ref, co_ref, se_ref, so_ref, hs_ref, hd_ref, kr_ref, ki_ref):
    norm = 0.5 / ce_ref.shape[1]
    ae, ao = _mm(ce_ref[...], hs_ref[0]), _mm(co_ref[...], hs_ref[1])
    be, bo = _mm(se_ref[...], hd_ref[0]), _mm(so_ref[...], hd_ref[1])
    kr_ref[0] = (ae + ao) * norm
    kr_ref[1] = (ae - ao) * norm
    ki_ref[0] = (be + bo) * norm
    ki_ref[1] = (bo - be) * norm


def _hy_spec(tables, hs, hd):
    half = hs.shape[1]
    blk = pl.BlockSpec((2, half, HALF_CH), lambda c: (0, 0, c))
    out = jax.ShapeDtypeStruct((2, half, HY_DIM), F32)
    return pl.pallas_call(
        _hy_spec_kernel,
        grid=(HY_DIM // HALF_CH,),
        in_specs=[_resident(t.shape) for t in tables[:4]] + [blk, blk],
        out_specs=[blk, blk],
        out_shape=[out, out],
        compiler_params=_params("parallel"),
        name="hy_spec",
    )(*tables[:4], hs, hd)


def _hy_conv_kernel(z_ref, x0_ref, ce_ref, co_ref, se_ref, so_ref, cet_ref, cot_ref, set_ref, sot_ref,
                    kr_ref, ki_ref, b_ref, g_ref, o_ref, y_s):
    c = pl.program_id(1)
    half = z_ref.shape[2]
    ze, zo = z_ref[0, 0], z_ref[0, 1]
    ae, ao = _mm(ce_ref[...], ze), _mm(co_ref[...], zo)
    be, bo = _mm(se_ref[...], ze), _mm(so_ref[...], zo)
    u_lo, u_hi, v_lo, v_hi = ae + ao, ae - ao, be + bo, bo - be
    kr_lo, kr_hi, ki_lo, ki_hi = kr_ref[0], kr_ref[1], ki_ref[0], ki_ref[1]
    yr_lo, w_lo = u_lo * kr_lo + v_lo * ki_lo, v_lo * kr_lo - u_lo * ki_lo
    yr_hi, w_hi = u_hi * kr_hi + v_hi * ki_hi, v_hi * kr_hi - u_hi * ki_hi
    ye = _mm(cet_ref[...], (yr_lo + yr_hi).astype(BF16)) + _mm(set_ref[...], (w_lo - w_hi).astype(BF16))
    yo = _mm(cot_ref[...], (yr_lo - yr_hi).astype(BF16)) + _mm(sot_ref[...], (w_lo + w_hi).astype(BF16))
    bias = b_ref[...]
    ye = (ye + ze.astype(F32) * bias) * x0_ref[0, 0].astype(F32)
    yo = (yo + zo.astype(F32) * bias) * x0_ref[0, 1].astype(F32)
    per = HALF_CH // LANES
    for k in range(per):
        y_s[c * per + k, pl.ds(0, half, stride=2), :] = ye[:, k * LANES:(k + 1) * LANES]
        y_s[c * per + k, pl.ds(1, half, stride=2), :] = yo[:, k * LANES:(k + 1) * LANES]

    @pl.when(c == pl.num_programs(1) - 1)
    def _():
        slabs = [y_s[k] for k in range(HY_DIM // LANES)]
        ms = sum(jnp.sum(y * y, axis=-1, keepdims=True) for y in slabs) * (1.0 / HY_DIM)
        rs = lax.rsqrt(ms + NORM_EPS)
        for k, y in enumerate(slabs):
            cols = slice(k * LANES, (k + 1) * LANES)
            o_ref[0, :, cols] = (y * rs * g_ref[:, cols]).astype(BF16)


def _hy_conv(tables, z, x0, kr, ki, bias, g):
    b, _, half, _ = z.shape
    data = pl.BlockSpec((1, 2, half, HALF_CH), lambda i, c: (i, 0, 0, c))
    spec = pl.BlockSpec((2, half, HALF_CH), lambda i, c: (0, 0, c))
    return pl.pallas_call(
        _hy_conv_kernel,
        grid=(b, HY_DIM // HALF_CH),
        in_specs=[data, data] + [_resident(t.shape) for t in tables] + [spec, spec,
                  pl.BlockSpec((1, HALF_CH), lambda i, c: (0, c)), _resident((1, HY_DIM))],
        out_specs=pl.BlockSpec((1, 2 * half, HY_DIM), lambda i, c: (i, 0, 0)),
        out_shape=jax.ShapeDtypeStruct((b, 2 * half, HY_DIM), BF16),
        scratch_shapes=[pltpu.VMEM((HY_DIM // LANES, 2 * half, LANES), F32)],
        compiler_params=_params("parallel", "arbitrary"),
        name="hy_conv",
    )(z, x0, *tables, kr, ki, bias, g)


def _out_proj_kernel(x_ref, a_ref, b_ref, c_ref, w_ref, o_ref):
    acc = jnp.dot(a_ref[...], w_ref[0:POOL_DIM, :], preferred_element_type=F32)
    acc = acc + jnp.dot(b_ref[...], w_ref[POOL_DIM:POOL_DIM + MLA_DIM, :], preferred_element_type=F32)
    acc = acc + jnp.dot(c_ref[...], w_ref[POOL_DIM + MLA_DIM:, :], preferred_element_type=F32)
    o_ref[...] = x_ref[...] + acc


def _out_proj(x2, ma, mb, mc, w_all, layer, tm):
    m, d = x2.shape
    row = lambda width: pl.BlockSpec((tm, width), lambda i: (i, 0))
    w_spec = pl.BlockSpec((None,) + w_all.shape[1:], lambda i: (layer, 0, 0), pipeline_mode=pl.Buffered(1))
    return pl.pallas_call(
        _out_proj_kernel,
        grid=(m // tm,),
        in_specs=[row(d), row(POOL_DIM), row(MLA_DIM), row(HY_DIM), w_spec],
        out_specs=row(d),
        out_shape=jax.ShapeDtypeStruct((m, d), F32),
        compiler_params=_params("parallel"),
        name="out_proj",
    )(x2, ma, mb, mc, w_all)


def _ffn_kernel(x_ref, xp_ref, xn_ref, g_ref, wg_ref, wv_ref, cw_ref, cb_ref, wd_ref, o_ref, h_s, up_s, *, blocks_per_seq):
    tm = x_ref.shape[0]
    f = pl.program_id(1)

    @pl.when(f == 0)
    def _():
        x = x_ref[...]
        g = g_ref[...]
        h_s[HALO:HALO + tm, :] = _rms(x, g).astype(BF16)
        pos = pl.program_id(0) % blocks_per_seq
        pad = jnp.zeros((HALO - SUBLANES, x.shape[1]), F32)
        hp = jnp.where(pos == 0, 0.0, _rms(xp_ref[...], g))
        hn = jnp.where(pos == blocks_per_seq - 1, 0.0, _rms(xn_ref[...], g))
        h_s[0:HALO, :] = jnp.concatenate([pad, hp], axis=0).astype(BF16)
        h_s[HALO + tm:, :] = jnp.concatenate([hn, pad], axis=0).astype(BF16)

    parts = up_s.shape[0]
    sub = tm // parts
    vals = []
    for r in range(parts):
        lo = r * sub
        up_s[r] = _mm(h_s[lo:lo + sub + 2 * HALO, :], wg_ref[...])
        vals.append(_mm(h_s[HALO + lo:HALO + lo + sub, :], wv_ref[...]))
    cw = cw_ref[...]
    for r in range(parts):
        gate = (up_s[r, HALO - 1:HALO - 1 + sub, :] * cw[0:1] + up_s[r, HALO:HALO + sub, :] * cw[1:2]
                + up_s[r, HALO + 1:HALO + 1 + sub, :] * cw[2:3] + cb_ref[...])
        act = (gate * jax.nn.sigmoid(gate) * vals[r]).astype(BF16)
        rows = slice(r * sub, (r + 1) * sub)
        base = jnp.where(f == 0, x_ref[rows, :], o_ref[rows, :])
        o_ref[rows, :] = base + _mm(act, wd_ref[...])


def _ffn(x2, seq, g, w_up, cw, cb, w_down, layer, tm, tf):
    m, d = x2.shape
    nf = D_FF // tf
    r8 = tm // SUBLANES
    last8 = m // SUBLANES - 1
    kern = functools.partial(_ffn_kernel, blocks_per_seq=seq // tm)
    return pl.pallas_call(
        kern,
        grid=(m // tm, nf),
        in_specs=[pl.BlockSpec((tm, d), lambda i, f: (i, 0)),
                  pl.BlockSpec((SUBLANES, d), lambda i, f: (jnp.maximum(i * r8 - 1, 0), 0)),
                  pl.BlockSpec((SUBLANES, d), lambda i, f: (jnp.minimum((i + 1) * r8, last8), 0)),
                  _resident((1, d)),
                  pl.BlockSpec((None, d, tf), lambda i, f: (layer, 0, f)),
                  pl.BlockSpec((None, d, tf), lambda i, f: (layer, 0, f + nf)),
                  pl.BlockSpec((3, tf), lambda i, f: (0, f)),
                  pl.BlockSpec((1, tf), lambda i, f: (0, f)),
                  pl.BlockSpec((None, tf, d), lambda i, f: (layer, f, 0))],
        out_specs=pl.BlockSpec((tm, d), lambda i, f: (i, 0)),
        out_shape=jax.ShapeDtypeStruct((m, d), F32),
        scratch_shapes=[pltpu.VMEM((tm + 2 * HALO, d), BF16),
                        pltpu.VMEM((FFN_PARTS, tm // FFN_PARTS + 2 * HALO, tf), F32)],
        compiler_params=_params("parallel", "arbitrary"),
        name="ffn",
    )(x2, x2, x2, g, w_up, w_up, cw, cb, w_down)


def _rope_tables(seq):
    freqs = ROPE_THETA ** (-jnp.arange(0, QK_ROPE, 2, dtype=F32) / QK_ROPE)
    ang = jnp.arange(seq, dtype=F32)[:, None] * freqs[None, :]
    cos, sin, zero = jnp.cos(ang), jnp.sin(ang), jnp.zeros((seq, ROPE_HALF), F32)
    return (jnp.concatenate([cos, zero, cos, zero], axis=1), jnp.concatenate([-sin, zero, sin, zero], axis=1))


def _dft_tables(seq):
    half = seq // 2
    f = jnp.arange(half, dtype=jnp.int32)[:, None]
    j = jnp.arange(half, dtype=jnp.int32)[None, :]
    out = []
    for tok in (2 * j, 2 * j + 1):
        ang = (((2 * f + 1) * tok) % (4 * seq)).astype(F32) * (math.pi / (2 * seq))
        out.append((jnp.cos(ang).astype(BF16), jnp.sin(ang).astype(BF16)))
    (ce, se), (co, so) = out
    return ce, co, se, so, ce.T, co.T, se.T, so.T


def _filter_features(seq):
    t = jnp.linspace(0.0, 1.0, seq, dtype=F32)[:, None]
    w = 2.0 * math.pi * jnp.arange(seq, dtype=F32)[:, None] / seq
    bands = jnp.linspace(1e-4, N_BANDS - 1, N_BANDS, dtype=F32)[None, :]
    z = jnp.concatenate([t, jnp.cos(bands * w), -jnp.sin(bands * w)], axis=-1)
    z = jnp.concatenate([z[0::2], z[1::2]], axis=0)
    return jnp.pad(z, ((0, 0), (0, LANES - N_POS_FEAT)))


def _decay_rates():
    max_decay = math.log(HY_DECAY_TARGET) / HY_FAST_DECAY
    min_decay = math.log(HY_DECAY_TARGET) / HY_SLOW_DECAY
    return jnp.abs(jnp.linspace(min_decay, max_decay, HY_DIM, dtype=F32))[None, :]


def _spread_rope(a):
    zero = jnp.zeros(a.shape[:-1] + (ROPE_HALF,), a.dtype)
    return jnp.concatenate([a[..., :ROPE_HALF], zero, a[..., ROPE_HALF:], zero], axis=-1)


def _pad_to(a, shape):
    return jnp.pad(a, [(0, t - s) for s, t in zip(a.shape, shape)])


def _layer_weights(p, l):
    row = lambda a: a[l][None, :].astype(F32)
    w_in = p["w_in"][l]
    o1, o2, o3 = POOL_DIM + Q_LORA, POOL_DIM + Q_LORA + KV_LORA, POOL_DIM + Q_LORA + KV_LORA + QK_ROPE
    w_in_p = jnp.concatenate([w_in[:, o3:], w_in[:, :o2], _spread_rope(w_in[:, o2:o3])], axis=1).astype(BF16)
    wq = p["mla_w_uq"][l].reshape(Q_LORA, MLA_HEADS, QK_HEAD)
    wq_p = jnp.concatenate([wq[..., :QK_NOPE], _spread_rope(wq[..., QK_NOPE:])], axis=-1)
    wkv = p["mla_w_ukv"][l].reshape(KV_LORA, MLA_HEADS, QK_NOPE + V_HEAD)
    qhg, khg = p["mla_q_head_norm_g"][l], p["mla_k_head_norm_g"][l]
    hid = (LANES, LANES)
    return dict(
        attn_g=row(p["attn_norm_g"]), w_in=w_in_p,
        pool_w=p["pool_w"][l].astype(BF16), pool_scale=row(p["pool_scale"]),
        qn=row(p["mla_q_norm_g"]), wq=wq_p.reshape(Q_LORA, MLA_HEADS * HEAD_PAD).astype(BF16),
        kvn=row(p["mla_kv_norm_g"]),
        wk=wkv[..., :QK_NOPE].reshape(KV_LORA, MLA_HEADS * QK_NOPE).astype(BF16),
        wv=wkv[..., QK_NOPE:].reshape(KV_LORA, MLA_DIM).astype(BF16),
        qhg=jnp.concatenate([qhg[:QK_NOPE], _spread_rope(qhg[QK_NOPE:])])[None, :],
        khn=khg[None, :QK_NOPE], khr=_spread_rope(khg[QK_NOPE:])[None, :],
        hy_cw=p["hy_conv_w"][l], hy_cb=row(p["hy_conv_b"]), hy_bias=row(p["hy_bias"]),
        f_w1=_pad_to(p["hy_filt_w1"][l], hid), f_b1=_pad_to(row(p["hy_filt_b1"]), (1, LANES)),
        f_fr=_pad_to(row(p["hy_filt_freq"]), (1, LANES)),
        f_w2=_pad_to(p["hy_filt_w2"][l], hid), f_b2=_pad_to(row(p["hy_filt_b2"]), (1, LANES)),
        f_w3=_pad_to(p["hy_filt_w3"][l], (LANES, 2 * HY_DIM)), f_b3=row(p["hy_filt_b3"]),
        g_pool=p["grp_norm_g"][l][None, :POOL_DIM], g_mla=p["grp_norm_g"][l][None, POOL_DIM:POOL_DIM + MLA_DIM],
        g_hy=p["grp_norm_g"][l][None, POOL_DIM + MLA_DIM:],
        ffn_g=row(p["ffn_norm_g"]), ffn_cw=p["ffn_conv_w"][l], ffn_cb=row(p["ffn_conv_b"]),
    )


def _tile(n, want):
    return min(n, want)


def _layer(x2, b, seq, layer, w, stacks, kr, ki, rope, tables):
    cos_t, sin_t = rope
    m = b * seq
    q2, k2, v2, m_pool, z, x0 = _in_proj(
        x2, seq, w["attn_g"], w["w_in"], cos_t, sin_t, w["qn"], w["wq"], w["kvn"], w["wk"], w["wv"], w["qhg"],
        w["khn"], w["khr"], w["pool_w"], w["pool_scale"], w["g_pool"], w["hy_cw"], w["hy_cb"],
        _tile(seq, IN_PROJ_ROWS))
    hp = MLA_HEADS * HEAD_PAD
    m_mla = _attention(q2.reshape(b, seq, hp), k2.reshape(b, seq, hp), v2.reshape(b, seq, MLA_DIM),
                       w["g_mla"], _tile(seq, ATTN_QUERIES))
    m_hy = _hy_conv(tables, z, x0, kr, ki, w["hy_bias"], w["g_hy"])
    x2 = _out_proj(x2, m_pool.reshape(m, POOL_DIM), m_mla.reshape(m, MLA_DIM), m_hy.reshape(m, HY_DIM),
                   stacks["w_out"], layer, _tile(seq, OUT_PROJ_ROWS))
    return _ffn(x2, seq, w["ffn_g"], stacks["w_up"], w["ffn_cw"], w["ffn_cb"], stacks["w_down"], layer,
                _tile(seq, FFN_ROWS), FFN_COLS)


def kernel(x_prompt, x_sample, attn_norm_g, w_in, pool_w, pool_scale, mla_q_norm_g, mla_w_uq, mla_kv_norm_g, mla_w_ukv, mla_q_head_norm_g, mla_k_head_norm_g, hy_conv_w, hy_conv_b, hy_filt_w1, hy_filt_b1, hy_filt_freq, hy_filt_w2, hy_filt_b2, hy_filt_w3, hy_filt_b3, hy_bias, grp_norm_g, w_out, ffn_norm_g, ffn_w_up, ffn_conv_w, ffn_conv_b, ffn_w_down):
    p = dict(attn_norm_g=attn_norm_g, w_in=w_in, pool_w=pool_w, pool_scale=pool_scale, mla_q_norm_g=mla_q_norm_g,
             mla_w_uq=mla_w_uq, mla_kv_norm_g=mla_kv_norm_g, mla_w_ukv=mla_w_ukv,
             mla_q_head_norm_g=mla_q_head_norm_g, mla_k_head_norm_g=mla_k_head_norm_g, hy_conv_w=hy_conv_w,
             hy_conv_b=hy_conv_b, hy_filt_w1=hy_filt_w1, hy_filt_b1=hy_filt_b1, hy_filt_freq=hy_filt_freq,
             hy_filt_w2=hy_filt_w2, hy_filt_b2=hy_filt_b2, hy_filt_w3=hy_filt_w3, hy_filt_b3=hy_filt_b3,
             hy_bias=hy_bias, grp_norm_g=grp_norm_g, w_out=w_out, ffn_norm_g=ffn_norm_g, ffn_w_up=ffn_w_up,
             ffn_conv_w=ffn_conv_w, ffn_conv_b=ffn_conv_b, ffn_w_down=ffn_w_down)
    depth = attn_norm_g.shape[0]
    seq = x_prompt.shape[1]
    assert x_sample.shape[1] == seq and x_prompt.shape[2] == D_MODEL
    rope = _rope_tables(seq)
    tables = _dft_tables(seq)
    feat = _filter_features(seq)
    rates = _decay_rates()
    stacks = dict(w_out=w_out.astype(BF16), w_up=ffn_w_up.astype(BF16), w_down=ffn_w_down.astype(BF16))
    xs = [x.reshape(-1, D_MODEL) for x in (x_prompt, x_sample)]
    for l in range(depth):
        w = _layer_weights(p, l)
        hs, hd = _hy_filter(feat, w["f_w1"], w["f_b1"], w["f_fr"], w["f_w2"], w["f_b2"], w["f_w3"], w["f_b3"],
                            rates, _tile(seq, FILTER_ROWS))
        half = (2, seq // 2, HY_DIM)
        kr, ki = _hy_spec(tables, hs.reshape(half), hd.reshape(half))
        xs = [_layer(x2, x.shape[0], seq, l, w, stacks, kr, ki, rope, tables)
              for x2, x in zip(xs, (x_prompt, x_sample))]
    return (xs[0].reshape(x_prompt.shape), xs[1].reshape(x_sample.shape))
```

```python
import functools
import math

import jax
import jax.numpy as jnp
from jax import lax
from jax.experimental import pallas as pl
from jax.experimental.pallas import tpu as pltpu

F32 = jnp.float32
BF16 = jnp.bfloat16

D_MODEL = 2048
POOL_DIM = 512
POOL_WINDOWS = (2, 4, 8, 16)
POOL_GROUP = 128
MLA_HEADS = 8
QK_NOPE = 128
QK_ROPE = 64
QK_HEAD = QK_NOPE + QK_ROPE
V_HEAD = 128
MLA_DIM = MLA_HEADS * V_HEAD
Q_LORA = 512
KV_LORA = 256
ROPE_THETA = 10000.0
HY_DIM = 512
N_BANDS = 8
N_POS_FEAT = 1 + 2 * N_BANDS
HY_FAST_DECAY = 0.3
HY_SLOW_DECAY = 1.5
HY_DECAY_TARGET = 1e-2
D_FF = 5632
NORM_EPS = 1e-6

LANES = 128
SUBLANES = 8
HEAD_PAD = 2 * LANES
ROPE_HALF = QK_ROPE // 2
HALF_CH = 2 * LANES
COL_POOL = 3 * HY_DIM
COL_CQ = COL_POOL + POOL_DIM
HALO = 16
FFN_PARTS = 2
IN_PROJ_ROWS = 512
ATTN_QUERIES = 512
OUT_PROJ_ROWS = 512
FFN_ROWS = 1024
FFN_COLS = 512
FILTER_ROWS = 512
VMEM_LIMIT = 56 * 1024 * 1024


def _params(*sem):
    return pltpu.CompilerParams(dimension_semantics=sem, vmem_limit_bytes=VMEM_LIMIT)


def _resident(shape):
    nd = len(shape)
    return pl.BlockSpec(shape, lambda *_: (0,) * nd, pipeline_mode=pl.Buffered(1))


def _mm(a, b):
    return jnp.dot(a, b, preferred_element_type=F32)


def _rms(x, g):
    ms = jnp.sum(x * x, axis=-1, keepdims=True) * (1.0 / x.shape[-1])
    return x * lax.rsqrt(ms + NORM_EPS) * g


def _rope(r, cos_t, sin_t):
    return r * cos_t + pltpu.roll(r, LANES // 2, 1) * sin_t


def _in_proj_kernel(x_ref, xp_ref, xn_ref, g_ref, w_ref, cos_ref, sin_ref, qn_ref, wq_ref, kvn_ref, wk_ref, wv_ref,
                    qhg_ref, khn_ref, khr_ref, pw_ref, ps_ref, pg_ref, cw_ref, cb_ref,
                    q_ref, k_ref, v_ref, mp_ref, z_ref, x0_ref, h_s, t_s, y_s, *, blocks_per_seq, seq):
    tm = x_ref.shape[0]
    g = g_ref[...]
    pos = pl.program_id(0) % blocks_per_seq
    h = _rms(x_ref[...], g).astype(BF16)
    h_s[HALO:HALO + tm, :] = h
    pad = jnp.zeros((HALO - SUBLANES, x_ref.shape[1]), F32)
    hp = jnp.where(pos == 0, 0.0, _rms(xp_ref[...], g))
    hn = jnp.where(pos == blocks_per_seq - 1, 0.0, _rms(xn_ref[...], g))
    h_s[0:HALO, :] = jnp.concatenate([pad, hp], axis=0).astype(BF16)
    h_s[HALO + tm:, :] = jnp.concatenate([hn, pad], axis=0).astype(BF16)
    lat = _mm(h, w_ref[:, COL_CQ:])
    cq = _rms(lat[:, :Q_LORA], qn_ref[...]).astype(BF16)
    ckv = _rms(lat[:, Q_LORA:Q_LORA + KV_LORA], kvn_ref[...]).astype(BF16)
    kr = lat[:, Q_LORA + KV_LORA:]
    qf = _mm(cq, wq_ref[...])
    kn = _mm(ckv, wk_ref[...])
    v_ref[...] = _mm(ckv, wv_ref[...]).astype(BF16)
    pool_u = _mm(h_s[...], w_ref[:, COL_POOL:COL_CQ])
    hy_u = [[_mm(h_s[...], w_ref[:, sec * HY_DIM + cc * HALF_CH:sec * HY_DIM + (cc + 1) * HALF_CH])
             for sec in range(3)] for cc in range(HY_DIM // HALF_CH)]

    cos_t = cos_ref[...]
    sin_t = sin_ref[...]
    scale = QK_HEAD ** -0.5 * math.log2(math.e)
    qg = qhg_ref[...] * scale
    qg_n, qg_r = qg[:, :LANES], qg[:, LANES:]
    rope_a = qg_r * cos_t
    rope_b = pltpu.roll(qg_r, LANES // 2, 1) * sin_t
    for hd in range(MLA_HEADS):
        qh = qf[:, hd * HEAD_PAD:(hd + 1) * HEAD_PAD]
        rs = lax.rsqrt(jnp.sum(qh * qh, axis=-1, keepdims=True) * (1.0 / QK_HEAD) + NORM_EPS)
        qr = qh[:, LANES:]
        q_ref[:, hd * HEAD_PAD:hd * HEAD_PAD + LANES] = (qh[:, :LANES] * rs * qg_n).astype(BF16)
        q_ref[:, hd * HEAD_PAD + LANES:(hd + 1) * HEAD_PAD] = (
            (qr * rope_a + pltpu.roll(qr, LANES // 2, 1) * rope_b) * rs).astype(BF16)
    ss_r = jnp.sum(kr * kr, axis=-1, keepdims=True)
    kr_rot = _rope(kr * khr_ref[...], cos_t, sin_t)
    for hd in range(MLA_HEADS):
        knh = kn[:, hd * LANES:(hd + 1) * LANES]
        ms = (jnp.sum(knh * knh, axis=-1, keepdims=True) + ss_r) * (1.0 / QK_HEAD)
        rs = lax.rsqrt(ms + NORM_EPS)
        k_ref[:, hd * HEAD_PAD:hd * HEAD_PAD + LANES] = (knh * rs * khn_ref[...]).astype(BF16)
        k_ref[:, hd * HEAD_PAD + LANES:(hd + 1) * HEAD_PAD] = (kr_rot * rs).astype(BF16)

    ext = tm + 2 * HALO
    main = slice(HALO, HALO + tm)
    down = lambda a, n: pltpu.roll(a, n, 0)
    up = lambda a, n: pltpu.roll(a, ext - n, 0)

    tok = lax.broadcasted_iota(jnp.int32, (tm, 1), 0) + pos * tm
    for gi, win in enumerate(POOL_WINDOWS):
        half = win // 2
        cols = slice(gi * POOL_GROUP, (gi + 1) * POOL_GROUP)
        xg = pool_u[:, cols]
        ahead = xg
        behind = down(xg, 1)
        span = 1
        while span < half:
            ahead = ahead + up(ahead, span)
            behind = behind + down(behind, span)
            span *= 2
        tot = (ahead + behind)[main]
        cnt = (jnp.minimum(tok + half, seq) - jnp.maximum(tok - half, 0)).astype(F32)
        pooled = (tot / cnt - xg[main]).astype(BF16)
        y_s[:, cols] = _mm(pooled, pw_ref[gi]) * ps_ref[:, cols]
    mp_ref[...] = _rms(y_s[...], pg_ref[...]).astype(BF16)

    def conv3(xc, col):
        cols = slice(col, col + LANES)
        w = cw_ref[:, cols]
        return (down(xc, 1) * w[0:1] + xc * w[1:2] + up(xc, 1) * w[2:3])[main] + cb_ref[:, cols]

    def store_split(dst_ref, slab, cols, val):
        t_s[slab] = val
        dst_ref[0, 0, :, cols] = t_s[slab, pl.ds(0, tm // 2, stride=2), :].astype(BF16)
        dst_ref[0, 1, :, cols] = t_s[slab, pl.ds(1, tm // 2, stride=2), :].astype(BF16)

    for c in range(HY_DIM // LANES):
        cols = slice(c * LANES, (c + 1) * LANES)
        x0u, x1u, vu = (u[:, (c % 2) * LANES:(c % 2 + 1) * LANES] for u in hy_u[c // 2])
        store_split(x0_ref, 2 * c, cols, conv3(x0u, c * LANES))
        store_split(z_ref, 2 * c + 1, cols, conv3(vu, 2 * HY_DIM + c * LANES) * conv3(x1u, HY_DIM + c * LANES))


def _in_proj(x2, seq, g, w, cos_t, sin_t, qn, wq, kvn, wk, wv, qhg, khn, khr, pool_w, pool_scale, g_pool,
             hy_cw, hy_cb, tm):
    m, d = x2.shape
    nseq = seq // tm
    hp = MLA_HEADS * HEAD_PAD
    r8 = tm // SUBLANES
    last8 = m // SUBLANES - 1
    row = lambda width: pl.BlockSpec((tm, width), lambda i: (i, 0))
    tab = pl.BlockSpec((tm, LANES), lambda i: (i % nseq, 0))
    split = pl.BlockSpec((1, 2, tm // 2, HY_DIM), lambda i: (i // nseq, 0, i % nseq, 0))
    split_shape = jax.ShapeDtypeStruct((m // seq, 2, seq // 2, HY_DIM), BF16)
    consts = (qn, wq, kvn, wk, wv, qhg, khn, khr, pool_w, pool_scale, g_pool, hy_cw, hy_cb)
    return pl.pallas_call(
        functools.partial(_in_proj_kernel, blocks_per_seq=nseq, seq=seq),
        grid=(m // tm,),
        in_specs=[row(d),
                  pl.BlockSpec((SUBLANES, d), lambda i: (jnp.maximum(i * r8 - 1, 0), 0)),
                  pl.BlockSpec((SUBLANES, d), lambda i: (jnp.minimum((i + 1) * r8, last8), 0)),
                  _resident((1, d)), _resident(w.shape), tab, tab] + [_resident(a.shape) for a in consts],
        out_specs=[row(hp), row(hp), row(MLA_DIM), row(POOL_DIM), split, split],
        out_shape=[jax.ShapeDtypeStruct((m, hp), BF16), jax.ShapeDtypeStruct((m, hp), BF16),
                   jax.ShapeDtypeStruct((m, MLA_DIM), BF16), jax.ShapeDtypeStruct((m, POOL_DIM), BF16),
                   split_shape, split_shape],
        scratch_shapes=[pltpu.VMEM((tm + 2 * HALO, d), BF16), pltpu.VMEM((2 * HY_DIM // LANES, tm, LANES), F32),
                        pltpu.VMEM((tm, POOL_DIM), F32)],
        compiler_params=_params("parallel"),
        name="in_proj",
    )(x2, x2, x2, g, w, cos_t, sin_t, *consts)


def _attn_kernel(q_ref, k_ref, v_ref, o_ref, v_s):
    @pl.when(pl.program_id(1) == 0)
    def _():
        ones_col = (lax.broadcasted_iota(jnp.int32, (v_s.shape[0], LANES), 1) == 0).astype(BF16)
        for h in range(MLA_HEADS):
            v_s[:, h * HEAD_PAD:h * HEAD_PAD + V_HEAD] = v_ref[0, :, h * V_HEAD:(h + 1) * V_HEAD]
            v_s[:, h * HEAD_PAD + V_HEAD:(h + 1) * HEAD_PAD] = ones_col

    def scores(h):
        q = q_ref[0, :, h * HEAD_PAD:(h + 1) * HEAD_PAD]
        k = k_ref[0, :, h * HEAD_PAD:(h + 1) * HEAD_PAD]
        return lax.dot_general(q, k, (((1,), (1,)), ((), ())), preferred_element_type=F32)

    s_next = scores(0)
    for h in range(MLA_HEADS):
        s = s_next
        if h + 1 < MLA_HEADS:
            s_next = scores(h + 1)
        p = jnp.exp2(s - jnp.max(s, axis=-1, keepdims=True)).astype(BF16)
        o = _mm(p, v_s[:, h * HEAD_PAD:(h + 1) * HEAD_PAD])
        o_ref[0, :, h * V_HEAD:(h + 1) * V_HEAD] = o[:, :V_HEAD] / o[:, V_HEAD:V_HEAD + 1]


def _attention(q3, k3, v3, tq):
    b, seq, hp = q3.shape
    return pl.pallas_call(
        _attn_kernel,
        grid=(b, seq // tq),
        in_specs=[pl.BlockSpec((1, tq, hp), lambda i, j: (i, j, 0)),
                  pl.BlockSpec((1, seq, hp), lambda i, j: (i, 0, 0)),
                  pl.BlockSpec((1, seq, MLA_DIM), lambda i, j: (i, 0, 0))],
        out_specs=pl.BlockSpec((1, tq, MLA_DIM), lambda i, j: (i, j, 0)),
        out_shape=jax.ShapeDtypeStruct((b, seq, MLA_DIM), F32),
        scratch_shapes=[pltpu.VMEM((seq, hp), BF16)],
        compiler_params=_params("parallel", "arbitrary"),
        name="attention",
    )(q3, k3, v3)


def _hy_filter_kernel(feat_ref, w1_ref, b1_ref, fr_ref, w2_ref, b2_ref, w3_ref, b3_ref, dl_ref, hs_ref, hd_ref):
    hi = lax.Precision.HIGHEST
    feat = feat_ref[...]
    fr = fr_ref[...]
    h = jnp.sin(fr * (jnp.dot(feat, w1_ref[...], precision=hi, preferred_element_type=F32) + b1_ref[...]))
    h = jnp.sin(fr * (jnp.dot(h, w2_ref[...], precision=hi, preferred_element_type=F32) + b2_ref[...]))
    h = jnp.dot(h, w3_ref[...], precision=hi, preferred_element_type=F32) + b3_ref[...]
    decay = jnp.exp(-feat[:, 0:1] * dl_ref[...])
    fwd = h[:, :HY_DIM] * decay
    row = lax.broadcasted_iota(jnp.int32, (feat.shape[0], 1), 0) + pl.program_id(0) * feat.shape[0]
    bwd = jnp.where(row > 0, h[:, HY_DIM:] * decay, 0.0)
    hs_ref[...] = (fwd + bwd).astype(BF16)
    hd_ref[...] = (bwd - fwd).astype(BF16)


def _hy_filter(feat, w1, b1, fr, w2, b2, w3, b3, dl, tl):
    seq = feat.shape[0]
    out = jax.ShapeDtypeStruct((seq, HY_DIM), BF16)
    return pl.pallas_call(
        _hy_filter_kernel,
        grid=(seq // tl,),
        in_specs=[pl.BlockSpec((tl, LANES), lambda i: (i, 0))] + [_resident(a.shape) for a in (w1, b1, fr, w2, b2, w3, b3, dl)],
        out_specs=[pl.BlockSpec((tl, HY_DIM), lambda i: (i, 0))] * 2,
        out_shape=[out, out],
        compiler_params=_params("parallel"),
        name="hy_filter",
    )(feat, w1, b1, fr, w2, b2, w3, b3, dl)


def _hy_spec_kernel(ce_ref, co_ref, se_ref, so_ref, hs_ref, hd_ref, kr_ref, ki_ref):
    norm = 0.5 / ce_ref.shape[1]
    ae, ao = _mm(ce_ref[...], hs_ref[0]), _mm(co_ref[...], hs_ref[1])
    be, bo = _mm(se_ref[...], hd_ref[0]), _mm(so_ref[...], hd_ref[1])
    kr_ref[0] = (ae + ao) * norm
    kr_ref[1] = (ae - ao) * norm
    ki_ref[0] = (be + bo) * norm
    ki_ref[1] = (bo - be) * norm


def _hy_spec(tables, hs, hd):
    half = hs.shape[1]
    blk = pl.BlockSpec((2, half, HALF_CH), lambda c: (0, 0, c))
    out = jax.ShapeDtypeStruct((2, half, HY_DIM), F32)
    return pl.pallas_call(
        _hy_spec_kernel,
        grid=(HY_DIM // HALF_CH,),
        in_specs=[_resident(t.shape) for t in tables[:4]] + [blk, blk],
        out_specs=[blk, blk],
        out_shape=[out, out],
        compiler_params=_params("parallel"),
        name="hy_spec",
    )(*tables[:4], hs, hd)


def _hy_conv_kernel(z_ref, x0_ref, ce_ref, co_ref, se_ref, so_ref, cet_ref, cot_ref, set_ref, sot_ref,
                    kr_ref, ki_ref, b_ref, g_ref, o_ref, y_s):
    c = pl.program_id(1)
    half = z_ref.shape[2]
    ze, zo = z_ref[0, 0], z_ref[0, 1]
    ae, ao = _mm(ce_ref[...], ze), _mm(co_ref[...], zo)
    be, bo = _mm(se_ref[...], ze), _mm(so_ref[...], zo)
    u_lo, u_hi, v_lo, v_hi = ae + ao, ae - ao, be + bo, bo - be
    kr_lo, kr_hi, ki_lo, ki_hi = kr_ref[0], kr_ref[1], ki_ref[0], ki_ref[1]
    yr_lo, w_lo = u_lo * kr_lo + v_lo * ki_lo, v_lo * kr_lo - u_lo * ki_lo
    yr_hi, w_hi = u_hi * kr_hi + v_hi * ki_hi, v_hi * kr_hi - u_hi * ki_hi
    ye = _mm(cet_ref[...], (yr_lo + yr_hi).astype(BF16)) + _mm(set_ref[...], (w_lo - w_hi).astype(BF16))
    yo = _mm(cot_ref[...], (yr_lo - yr_hi).astype(BF16)) + _mm(sot_ref[...], (w_lo + w_hi).astype(BF16))
    bias = b_ref[...]
    ye = (ye + ze.astype(F32) * bias) * x0_ref[0, 0].astype(F32)
    yo = (yo + zo.astype(F32) * bias) * x0_ref[0, 1].astype(F32)
    per = HALF_CH // LANES
    for k in range(per):
        y_s[c * per + k, pl.ds(0, half, stride=2), :] = ye[:, k * LANES:(k + 1) * LANES]
        y_s[c * per + k, pl.ds(1, half, stride=2), :] = yo[:, k * LANES:(k + 1) * LANES]

    @pl.when(c == pl.num_programs(1) - 1)
    def _():
        slabs = [y_s[k] for k in range(HY_DIM // LANES)]
        ms = sum(jnp.sum(y * y, axis=-1, keepdims=True) for y in slabs) * (1.0 / HY_DIM)
        rs = lax.rsqrt(ms + NORM_EPS)
        for k, y in enumerate(slabs):
            cols = slice(k * LANES, (k + 1) * LANES)
            o_ref[0, :, cols] = (y * rs * g_ref[:, cols]).astype(BF16)


def _hy_conv(tables, z, x0, kr, ki, bias, g):
    b, _, half, _ = z.shape
    data = pl.BlockSpec((1, 2, half, HALF_CH), lambda i, c: (i, 0, 0, c))
    spec = pl.BlockSpec((2, half, HALF_CH), lambda i, c: (0, 0, c))
    return pl.pallas_call(
        _hy_conv_kernel,
        grid=(b, HY_DIM // HALF_CH),
        in_specs=[data, data] + [_resident(t.shape) for t in tables] + [spec, spec,
                  pl.BlockSpec((1, HALF_CH), lambda i, c: (0, c)), _resident((1, HY_DIM))],
        out_specs=pl.BlockSpec((1, 2 * half, HY_DIM), lambda i, c: (i, 0, 0)),
        out_shape=jax.ShapeDtypeStruct((b, 2 * half, HY_DIM), BF16),
        scratch_shapes=[pltpu.VMEM((HY_DIM // LANES, 2 * half, LANES), F32)],
        compiler_params=_params("parallel", "arbitrary"),
        name="hy_conv",
    )(z, x0, *tables, kr, ki, bias, g)


def _out_proj_kernel(x_ref, a_ref, b_ref, c_ref, g_ref, w_ref, o_ref):
    mla = _rms(b_ref[...], g_ref[...]).astype(BF16)
    acc = jnp.dot(a_ref[...], w_ref[0:POOL_DIM, :], preferred_element_type=F32)
    acc = acc + jnp.dot(mla, w_ref[POOL_DIM:POOL_DIM + MLA_DIM, :], preferred_element_type=F32)
    acc = acc + jnp.dot(c_ref[...], w_ref[POOL_DIM + MLA_DIM:, :], preferred_element_type=F32)
    o_ref[...] = x_ref[...] + acc


def _out_proj(x2, ma, mb, mc, g_mla, w_all, layer, tm):
    m, d = x2.shape
    row = lambda width: pl.BlockSpec((tm, width), lambda i: (i, 0))
    w_spec = pl.BlockSpec((None,) + w_all.shape[1:], lambda i: (layer, 0, 0), pipeline_mode=pl.Buffered(1))
    return pl.pallas_call(
        _out_proj_kernel,
        grid=(m // tm,),
        in_specs=[row(d), row(POOL_DIM), row(MLA_DIM), row(HY_DIM), _resident((1, MLA_DIM)), w_spec],
        out_specs=row(d),
        out_shape=jax.ShapeDtypeStruct((m, d), F32),
        compiler_params=_params("parallel"),
        name="out_proj",
    )(x2, ma, mb, mc, g_mla, w_all)


def _ffn_kernel(x_ref, xp_ref, xn_ref, g_ref, wg_ref, wv_ref, cw_ref, cb_ref, wd_ref, o_ref, h_s, up_s, *, blocks_per_seq):
    tm = x_ref.shape[0]
    f = pl.program_id(1)

    @pl.when(f == 0)
    def _():
        x = x_ref[...]
        g = g_ref[...]
        h_s[HALO:HALO + tm, :] = _rms(x, g).astype(BF16)
        pos = pl.program_id(0) % blocks_per_seq
        pad = jnp.zeros((HALO - SUBLANES, x.shape[1]), F32)
        hp = jnp.where(pos == 0, 0.0, _rms(xp_ref[...], g))
        hn = jnp.where(pos == blocks_per_seq - 1, 0.0, _rms(xn_ref[...], g))
        h_s[0:HALO, :] = jnp.concatenate([pad, hp], axis=0).astype(BF16)
        h_s[HALO + tm:, :] = jnp.concatenate([hn, pad], axis=0).astype(BF16)

    parts = up_s.shape[0]
    sub = tm // parts
    vals = []
    for r in range(parts):
        lo = r * sub
        up_s[r] = _mm(h_s[lo:lo + sub + 2 * HALO, :], wg_ref[...])
        vals.append(_mm(h_s[HALO + lo:HALO + lo + sub, :], wv_ref[...]))
    cw = cw_ref[...]
    for r in range(parts):
        gate = (up_s[r, HALO - 1:HALO - 1 + sub, :] * cw[0:1] + up_s[r, HALO:HALO + sub, :] * cw[1:2]
                + up_s[r, HALO + 1:HALO + 1 + sub, :] * cw[2:3] + cb_ref[...])
        act = (gate * jax.nn.sigmoid(gate) * vals[r]).astype(BF16)
        rows = slice(r * sub, (r + 1) * sub)
        base = jnp.where(f == 0, x_ref[rows, :], o_ref[rows, :])
        o_ref[rows, :] = base + _mm(act, wd_ref[...])


def _ffn(x2, seq, g, w_up, cw, cb, w_down, layer, tm, tf):
    m, d = x2.shape
    nf = D_FF // tf
    r8 = tm // SUBLANES
    last8 = m // SUBLANES - 1
    kern = functools.partial(_ffn_kernel, blocks_per_seq=seq // tm)
    return pl.pallas_call(
        kern,
        grid=(m // tm, nf),
        in_specs=[pl.BlockSpec((tm, d), lambda i, f: (i, 0)),
                  pl.BlockSpec((SUBLANES, d), lambda i, f: (jnp.maximum(i * r8 - 1, 0), 0)),
                  pl.BlockSpec((SUBLANES, d), lambda i, f: (jnp.minimum((i + 1) * r8, last8), 0)),
                  _resident((1, d)),
                  pl.BlockSpec((None, d, tf), lambda i, f: (layer, 0, f)),
                  pl.BlockSpec((None, d, tf), lambda i, f: (layer, 0, f + nf)),
                  pl.BlockSpec((3, tf), lambda i, f: (0, f)),
                  pl.BlockSpec((1, tf), lambda i, f: (0, f)),
                  pl.BlockSpec((None, tf, d), lambda i, f: (layer, f, 0))],
        out_specs=pl.BlockSpec((tm, d), lambda i, f: (i, 0)),
        out_shape=jax.ShapeDtypeStruct((m, d), F32),
        scratch_shapes=[pltpu.VMEM((tm + 2 * HALO, d), BF16),
                        pltpu.VMEM((FFN_PARTS, tm // FFN_PARTS + 2 * HALO, tf), F32)],
        compiler_params=_params("parallel", "arbitrary"),
        name="ffn",
    )(x2, x2, x2, g, w_up, w_up, cw, cb, w_down)


def _rope_tables(seq):
    freqs = ROPE_THETA ** (-jnp.arange(0, QK_ROPE, 2, dtype=F32) / QK_ROPE)
    ang = jnp.arange(seq, dtype=F32)[:, None] * freqs[None, :]
    cos, sin, zero = jnp.cos(ang), jnp.sin(ang), jnp.zeros((seq, ROPE_HALF), F32)
    return (jnp.concatenate([cos, zero, cos, zero], axis=1), jnp.concatenate([-sin, zero, sin, zero], axis=1))


def _dft_tables(seq):
    half = seq // 2
    f = jnp.arange(half, dtype=jnp.int32)[:, None]
    j = jnp.arange(half, dtype=jnp.int32)[None, :]
    out = []
    for tok in (2 * j, 2 * j + 1):
        ang = (((2 * f + 1) * tok) % (4 * seq)).astype(F32) * (math.pi / (2 * seq))
        out.append((jnp.cos(ang).astype(BF16), jnp.sin(ang).astype(BF16)))
    (ce, se), (co, so) = out
    return ce, co, se, so, ce.T, co.T, se.T, so.T


def _filter_features(seq):
    t = jnp.linspace(0.0, 1.0, seq, dtype=F32)[:, None]
    w = 2.0 * math.pi * jnp.arange(seq, dtype=F32)[:, None] / seq
    bands = jnp.linspace(1e-4, N_BANDS - 1, N_BANDS, dtype=F32)[None, :]
    z = jnp.concatenate([t, jnp.cos(bands * w), -jnp.sin(bands * w)], axis=-1)
    z = jnp.concatenate([z[0::2], z[1::2]], axis=0)
    return jnp.pad(z, ((0, 0), (0, LANES - N_POS_FEAT)))


def _decay_rates():
    max_decay = math.log(HY_DECAY_TARGET) / HY_FAST_DECAY
    min_decay = math.log(HY_DECAY_TARGET) / HY_SLOW_DECAY
    return jnp.abs(jnp.linspace(min_decay, max_decay, HY_DIM, dtype=F32))[None, :]


def _spread_rope(a):
    zero = jnp.zeros(a.shape[:-1] + (ROPE_HALF,), a.dtype)
    return jnp.concatenate([a[..., :ROPE_HALF], zero, a[..., ROPE_HALF:], zero], axis=-1)


def _pad_to(a, shape):
    return jnp.pad(a, [(0, t - s) for s, t in zip(a.shape, shape)])


def _layer_weights(p, l):
    row = lambda a: a[l][None, :].astype(F32)
    w_in = p["w_in"][l]
    o1, o2, o3 = POOL_DIM + Q_LORA, POOL_DIM + Q_LORA + KV_LORA, POOL_DIM + Q_LORA + KV_LORA + QK_ROPE
    w_in_p = jnp.concatenate([w_in[:, o3:], w_in[:, :o2], _spread_rope(w_in[:, o2:o3])], axis=1).astype(BF16)
    wq = p["mla_w_uq"][l].reshape(Q_LORA, MLA_HEADS, QK_HEAD)
    wq_p = jnp.concatenate([wq[..., :QK_NOPE], _spread_rope(wq[..., QK_NOPE:])], axis=-1)
    wkv = p["mla_w_ukv"][l].reshape(KV_LORA, MLA_HEADS, QK_NOPE + V_HEAD)
    qhg, khg = p["mla_q_head_norm_g"][l], p["mla_k_head_norm_g"][l]
    hid = (LANES, LANES)
    return dict(
        attn_g=row(p["attn_norm_g"]), w_in=w_in_p,
        pool_w=p["pool_w"][l].astype(BF16), pool_scale=row(p["pool_scale"]),
        qn=row(p["mla_q_norm_g"]), wq=wq_p.reshape(Q_LORA, MLA_HEADS * HEAD_PAD).astype(BF16),
        kvn=row(p["mla_kv_norm_g"]),
        wk=wkv[..., :QK_NOPE].reshape(KV_LORA, MLA_HEADS * QK_NOPE).astype(BF16),
        wv=wkv[..., QK_NOPE:].reshape(KV_LORA, MLA_DIM).astype(BF16),
        qhg=jnp.concatenate([qhg[:QK_NOPE], _spread_rope(qhg[QK_NOPE:])])[None, :],
        khn=khg[None, :QK_NOPE], khr=_spread_rope(khg[QK_NOPE:])[None, :],
        hy_cw=p["hy_conv_w"][l], hy_cb=row(p["hy_conv_b"]), hy_bias=row(p["hy_bias"]),
        f_w1=_pad_to(p["hy_filt_w1"][l], hid), f_b1=_pad_to(row(p["hy_filt_b1"]), (1, LANES)),
        f_fr=_pad_to(row(p["hy_filt_freq"]), (1, LANES)),
        f_w2=_pad_to(p["hy_filt_w2"][l], hid), f_b2=_pad_to(row(p["hy_filt_b2"]), (1, LANES)),
        f_w3=_pad_to(p["hy_filt_w3"][l], (LANES, 2 * HY_DIM)), f_b3=row(p["hy_filt_b3"]),
        g_pool=p["grp_norm_g"][l][None, :POOL_DIM], g_mla=p["grp_norm_g"][l][None, POOL_DIM:POOL_DIM + MLA_DIM],
        g_hy=p["grp_norm_g"][l][None, POOL_DIM + MLA_DIM:],
        ffn_g=row(p["ffn_norm_g"]), ffn_cw=p["ffn_conv_w"][l], ffn_cb=row(p["ffn_conv_b"]),
    )


def _tile(n, want):
    return min(n, want)


def _layer(x2, b, seq, layer, w, stacks, kr, ki, rope, tables):
    cos_t, sin_t = rope
    m = b * seq
    q2, k2, v2, m_pool, z, x0 = _in_proj(
        x2, seq, w["attn_g"], w["w_in"], cos_t, sin_t, w["qn"], w["wq"], w["kvn"], w["wk"], w["wv"], w["qhg"],
        w["khn"], w["khr"], w["pool_w"], w["pool_scale"], w["g_pool"], w["hy_cw"], w["hy_cb"],
        _tile(seq, IN_PROJ_ROWS))
    hp = MLA_HEADS * HEAD_PAD
    y_mla = _attention(q2.reshape(b, seq, hp), k2.reshape(b, seq, hp), v2.reshape(b, seq, MLA_DIM),
                       _tile(seq, ATTN_QUERIES))
    m_hy = _hy_conv(tables, z, x0, kr, ki, w["hy_bias"], w["g_hy"])
    x2 = _out_proj(x2, m_pool.reshape(m, POOL_DIM), y_mla.reshape(m, MLA_DIM), m_hy.reshape(m, HY_DIM),
                   w["g_mla"], stacks["w_out"], layer, _tile(seq, OUT_PROJ_ROWS))
    return _ffn(x2, seq, w["ffn_g"], stacks["w_up"], w["ffn_cw"], w["ffn_cb"], stacks["w_down"], layer,
                _tile(seq, FFN_ROWS), FFN_COLS)


def kernel(x_prompt, x_sample, attn_norm_g, w_in, pool_w, pool_scale, mla_q_norm_g, mla_w_uq, mla_kv_norm_g, mla_w_ukv, mla_q_head_norm_g, mla_k_head_norm_g, hy_conv_w, hy_conv_b, hy_filt_w1, hy_filt_b1, hy_filt_freq, hy_filt_w2, hy_filt_b2, hy_filt_w3, hy_filt_b3, hy_bias, grp_norm_g, w_out, ffn_norm_g, ffn_w_up, ffn_conv_w, ffn_conv_b, ffn_w_down):
    p = dict(attn_norm_g=attn_norm_g, w_in=w_in, pool_w=pool_w, pool_scale=pool_scale, mla_q_norm_g=mla_q_norm_g,
             mla_w_uq=mla_w_uq, mla_kv_norm_g=mla_kv_norm_g, mla_w_ukv=mla_w_ukv,
             mla_q_head_norm_g=mla_q_head_norm_g, mla_k_head_norm_g=mla_k_head_norm_g, hy_conv_w=hy_conv_w,
             hy_conv_b=hy_conv_b, hy_filt_w1=hy_filt_w1, hy_filt_b1=hy_filt_b1, hy_filt_freq=hy_filt_freq,
             hy_filt_w2=hy_filt_w2, hy_filt_b2=hy_filt_b2, hy_filt_w3=hy_filt_w3, hy_filt_b3=hy_filt_b3,
             hy_bias=hy_bias, grp_norm_g=grp_norm_g, w_out=w_out, ffn_norm_g=ffn_norm_g, ffn_w_up=ffn_w_up,
             ffn_conv_w=ffn_conv_w, ffn_conv_b=ffn_conv_b, ffn_w_down=ffn_w_down)
    depth = attn_norm_g.shape[0]
    seq = x_prompt.shape[1]
    assert x_sample.shape[1] == seq and x_prompt.shape[2] == D_MODEL
    rope = _rope_tables(seq)
    tables = _dft_tables(seq)
    feat = _filter_features(seq)
    rates = _decay_rates()
    stacks = dict(w_out=w_out.astype(BF16), w_up=ffn_w_up.astype(BF16), w_down=ffn_w_down.astype(BF16))
    xs = [x.reshape(-1, D_MODEL) for x in (x_prompt, x_sample)]
    for l in range(depth):
        w = _layer_weights(p, l)
        hs, hd = _hy_filter(feat, w["f_w1"], w["f_b1"], w["f_fr"], w["f_w2"], w["f_b2"], w["f_w3"], w["f_b3"],
                            rates, _tile(seq, FILTER_ROWS))
        half = (2, seq // 2, HY_DIM)
        kr, ki = _hy_spec(tables, hs.reshape(half), hd.reshape(half))
        xs = [_layer(x2, x.shape[0], seq, l, w, stacks, kr, ki, rope, tables)
              for x2, x in zip(xs, (x_prompt, x_sample))]
    return (xs[0].reshape(x_prompt.shape), xs[1].reshape(x_sample.shape))
```
